```python
import jax, jax.numpy as jnp
from jax import lax
import numpy as np

D_MODEL = 2048
BATCH = 2
SEQ = 16384
DEPTH = 2

N_HEADS = 16
HEAD_DIM = D_MODEL // N_HEADS
GRID_W = 64
WIN_ROWS = 8
WIN_COLS = 16
N_META = 16
CONV_W = 3
D_FF = 7 * D_MODEL // 2
N_EXPERTS = 8
TOP_K = 2
D_EXPERT = 7 * D_MODEL // 2
MOE_BLOCK = 1024
N_MIXERS = 2
RMS_EPS = 1e-6
N_ATTN_LAYERS = (DEPTH + 1) // 2
N_CONV_LAYERS = DEPTH // 2
N_DENSE_LAYERS = (DEPTH + 1) // 2
N_MOE_LAYERS = DEPTH // 2

kernel_name = "hybrid_natten_shortconv_moe_encoder"


def rms_norm(x, g):
    xf = x.astype(jnp.float32)
    y = xf * lax.rsqrt(jnp.mean(xf * xf, axis=-1, keepdims=True) + RMS_EPS)
    return (y * g.astype(jnp.float32)).astype(x.dtype)


def neighbourhood_attention(h, w_qkv, q_gain, k_gain, rel_bias, meta_bias, w_o):
    b, L, d = h.shape
    n_tok = L - N_META
    rows = n_tok // GRID_W
    kr = min(WIN_ROWS, rows)
    qkv = (h @ w_qkv).reshape(b, L, 3, N_HEADS, HEAD_DIM)
    q = rms_norm(qkv[:, :, 0], q_gain) * (HEAD_DIM ** -0.5)
    k = rms_norm(qkv[:, :, 1], k_gain)
    v = qkv[:, :, 2]
    q_m, k_m, v_m = q[:, :N_META], k[:, :N_META], v[:, :N_META]

    def to_grid(t):
        return t[:, N_META:].reshape(b, rows, GRID_W, N_HEADS, HEAD_DIM)

    q_g, k_g, v_g = to_grid(q), to_grid(k), to_grid(v)

    s_mm = jnp.einsum('bqhd,bmhd->bhqm', q_m, k_m).astype(jnp.float32) + meta_bias[None, :, None, :]
    p_mm = jax.nn.softmax(s_mm, axis=-1).astype(v.dtype)
    o_m = jnp.einsum('bhqm,bmhd->bqhd', p_mm, v_m).reshape(b, N_META, d)

    cols = np.arange(GRID_W)
    col_start = np.clip(cols - WIN_COLS // 2, 0, GRID_W - WIN_COLS)
    col_idx = col_start[:, None] + np.arange(WIN_COLS)[None, :]
    dc_idx = col_idx - cols[:, None] + (WIN_COLS - 1)
    n_loc = kr * WIN_COLS

    def row_attend(args):
        r, q_row = args
        r0 = jnp.clip(r - WIN_ROWS // 2, 0, rows - kr)
        k_rows = lax.dynamic_slice_in_dim(k_g, r0, kr, axis=1)
        v_rows = lax.dynamic_slice_in_dim(v_g, r0, kr, axis=1)
        k_sel = k_rows[:, :, col_idx]
        v_sel = v_rows[:, :, col_idx]
        dr_idx = r0 + jnp.arange(kr) - r + (WIN_ROWS - 1)
        bias = rel_bias[:, dr_idx[:, None, None], dc_idx[None]]
        s_loc = jnp.einsum('bqhd,biqjhd->bhqij', q_row, k_sel).astype(jnp.float32)
        s_loc = (s_loc + bias.transpose(0, 2, 1, 3)[None]).reshape(b, N_HEADS, GRID_W, n_loc)
        s_met = jnp.einsum('bqhd,bmhd->bhqm', q_row, k_m).astype(jnp.float32) + meta_bias[None, :, None, :]
        p = jax.nn.softmax(jnp.concatenate([s_loc, s_met], axis=-1), axis=-1).astype(v_g.dtype)
        p_loc = p[..., :n_loc].reshape(b, N_HEADS, GRID_W, kr, WIN_COLS)
        p_met = p[..., n_loc:]
        return (jnp.einsum('bhqij,biqjhd->bqhd', p_loc, v_sel)
                + jnp.einsum('bhqm,bmhd->bqhd', p_met, v_m))

    o_g = lax.map(row_attend, (jnp.arange(rows), jnp.moveaxis(q_g, 1, 0)))
    o_g = jnp.moveaxis(o_g, 0, 1).reshape(b, n_tok, d)
    return jnp.concatenate([o_m, o_g], axis=1) @ w_o


def short_conv_mixer(h, w_in, conv_w, w_out):
    L = h.shape[1]
    bg, cg, u = jnp.split(h @ w_in, 3, axis=-1)
    z = cg * u
    pad = CONV_W // 2
    zp = jnp.pad(z, ((0, 0), (pad, pad), (0, 0)))
    y = sum(zp[:, j:j + L] * conv_w[:, j] for j in range(CONV_W))
    return (bg * y) @ w_out


def swiglu(h, w_gate, w_up, w_down):
    return (jax.nn.silu(h @ w_gate) * (h @ w_up)) @ w_down


def moe_swiglu(h, w_router, w_gate, w_up, w_down):
    b, L, d = h.shape
    n = b * L
    xf = h.reshape(n, d)
    logits = (xf @ w_router).astype(jnp.float32)
    top_logit, top_e = lax.top_k(logits, TOP_K)
    gates = jax.nn.softmax(top_logit, axis=-1)
    flat_e = top_e.reshape(-1)
    flat_tok = jnp.arange(n * TOP_K) // TOP_K
    order = jnp.argsort(flat_e)
    sorted_e = flat_e[order]
    sorted_tok = flat_tok[order]
    counts = jnp.bincount(flat_e, length=N_EXPERTS)
    group_start = jnp.cumsum(counts) - counts
    padded = (counts + MOE_BLOCK - 1) // MOE_BLOCK * MOE_BLOCK
    padded_end = jnp.cumsum(padded)
    padded_start = padded_end - padded
    dest = padded_start[sorted_e] + (jnp.arange(n * TOP_K) - group_start[sorted_e])
    n_blocks = -(-(n * TOP_K) // MOE_BLOCK) + N_EXPERTS
    total = n_blocks * MOE_BLOCK
    buf_tok = jnp.full((total,), n, dtype=jnp.int32).at[dest].set(sorted_tok.astype(jnp.int32))
    x_ext = jnp.concatenate([xf, jnp.zeros((1, d), xf.dtype)], axis=0)
    x_buf = x_ext[buf_tok].reshape(n_blocks, MOE_BLOCK, d)
    block_e = jnp.minimum(jnp.searchsorted(padded_end, jnp.arange(n_blocks) * MOE_BLOCK, side='right'),
                          N_EXPERTS - 1)

    def expert_block(args):
        xb, e = args
        return (jax.nn.silu(xb @ w_gate[e]) * (xb @ w_up[e])) @ w_down[e]

    y_buf = lax.map(expert_block, (x_buf, block_e)).reshape(total, d)
    w_assign = gates.reshape(-1)[order].astype(h.dtype)
    y = jax.ops.segment_sum(y_buf[dest] * w_assign[:, None], sorted_tok, num_segments=n)
    return y.reshape(b, L, d)


def setup_inputs(seed: int = 0) -> dict:
    key = jax.random.key(seed)
    ks = jax.random.split(key, 24)
    D = D_MODEL

    def nrm(k, shape, fan_in):
        return jax.random.normal(k, shape, jnp.float32) * (fan_in ** -0.5)

    def gain(k, shape):
        return 1.0 + 0.02 * jax.random.normal(k, shape, jnp.float32)

    return {
        "x": jax.random.normal(ks[0], (BATCH, SEQ, D), jnp.float32),
        "meta_tokens": jax.random.normal(ks[1], (N_META, D), jnp.float32),
        "attn_norm": gain(ks[2], (N_ATTN_LAYERS, D)),
        "w_qkv": nrm(ks[3], (N_ATTN_LAYERS, D, 3 * D), D),
        "q_gain": gain(ks[4], (N_ATTN_LAYERS, HEAD_DIM)),
        "k_gain": gain(ks[5], (N_ATTN_LAYERS, HEAD_DIM)),
        "rel_bias": 0.1 * jax.random.normal(ks[6], (N_ATTN_LAYERS, N_HEADS, 2 * WIN_ROWS - 1, 2 * WIN_COLS - 1), jnp.float32),
        "meta_bias": 0.1 * jax.random.normal(ks[7], (N_ATTN_LAYERS, N_HEADS, N_META), jnp.float32),
        "w_o": nrm(ks[8], (N_ATTN_LAYERS, D, D), D),
        "conv_norm": gain(ks[9], (N_CONV_LAYERS, D)),
        "w_in": nrm(ks[10], (N_CONV_LAYERS, D, 3 * D), D),
        "conv_w": nrm(ks[11], (N_CONV_LAYERS, D, CONV_W), CONV_W),
        "w_out": nrm(ks[12], (N_CONV_LAYERS, D, D), D),
        "dense_norm": gain(ks[13], (N_DENSE_LAYERS, D)),
        "w_gate": nrm(ks[14], (N_DENSE_LAYERS, D, D_FF), D),
        "w_up": nrm(ks[15], (N_DENSE_LAYERS, D, D_FF), D),
        "w_down": nrm(ks[16], (N_DENSE_LAYERS, D_FF, D), D_FF),
        "moe_norm": gain(ks[17], (N_MOE_LAYERS, D)),
        "w_router": nrm(ks[18], (N_MOE_LAYERS, D, N_EXPERTS), D),
        "moe_gate": nrm(ks[19], (N_MOE_LAYERS, N_EXPERTS, D, D_EXPERT), D),
        "moe_up": nrm(ks[20], (N_MOE_LAYERS, N_EXPERTS, D, D_EXPERT), D),
        "moe_down": nrm(ks[21], (N_MOE_LAYERS, N_EXPERTS, D_EXPERT, D), D_EXPERT),
    }


def reference(x, meta_tokens, attn_norm, w_qkv, q_gain, k_gain, rel_bias, meta_bias, w_o,
              conv_norm, w_in, conv_w, w_out, dense_norm, w_gate, w_up, w_down,
              moe_norm, w_router, moe_gate, moe_up, moe_down):
    b = x.shape[0]
    meta = jnp.broadcast_to(meta_tokens[None].astype(x.dtype), (b, N_META, D_MODEL))
    h = jnp.concatenate([meta, x], axis=1)
    for i in range(DEPTH):
        j = i // N_MIXERS
        if i % N_MIXERS == 0:
            h = h + neighbourhood_attention(rms_norm(h, attn_norm[j]), w_qkv[j], q_gain[j], k_gain[j],
                                            rel_bias[j], meta_bias[j], w_o[j])
        else:
            h = h + short_conv_mixer(rms_norm(h, conv_norm[j]), w_in[j], conv_w[j], w_out[j])
        f = i // 2
        if i % 2 == 0:
            h = h + swiglu(rms_norm(h, dense_norm[f]), w_gate[f], w_up[f], w_down[f])
        else:
            h = h + moe_swiglu(rms_norm(h, moe_norm[f]), w_router[f], moe_gate[f], moe_up[f], moe_down[f])
    return h[:, N_META:]
```

```python
import functools

import numpy as np
import jax
import jax.numpy as jnp
from jax import lax
from jax.experimental import pallas as pl
from jax.experimental.pallas import tpu as pltpu

HEAD_DIM = 128
GRID_W = 64
WIN_ROWS = 8
WIN_COLS = 16
N_META = 16
TOP_K = 2
RMS_EPS = 1e-6
LANES = 128
BF16_SUBLANES = 16
VMEM_LIMIT_BYTES = 56 * 1024 * 1024
MASK_VALUE = -1e30
N_LOC = WIN_ROWS * GRID_W

F32 = jnp.float32
BF16 = jnp.bfloat16


def _params(*sem):
    return pltpu.CompilerParams(dimension_semantics=sem, vmem_limit_bytes=VMEM_LIMIT_BYTES)


def _rms(xf, g):
    return xf * lax.rsqrt(jnp.mean(xf * xf, axis=-1, keepdims=True) + RMS_EPS) * g


def _dot(a, b):
    return jnp.dot(a, b, preferred_element_type=F32)


def _dot_nt(a, b):
    return lax.dot_general(a, b, (((1,), (1,)), ((), ())), preferred_element_type=F32)


def _qkv_kernel(x_ref, g_ref, w_ref, hg_ref, o_ref, xn_ref, *, n_norm_blocks):
    j = pl.program_id(1)

    @pl.when(j == 0)
    def _():
        xn_ref[...] = _rms(x_ref[...], g_ref[...]).astype(BF16)

    acc = _dot(xn_ref[...], w_ref[...])

    @pl.when(j < n_norm_blocks)
    def _():
        for c in range(acc.shape[1] // HEAD_DIM):
            sl = slice(c * HEAD_DIM, (c + 1) * HEAD_DIM)
            o_ref[:, sl] = _rms(acc[:, sl], hg_ref[:, sl]).astype(BF16)

    @pl.when(j >= n_norm_blocks)
    def _():
        o_ref[...] = acc.astype(BF16)


def _qkv(x, g, w, hg, *, bm, bn):
    m, d = x.shape
    n = w.shape[1]
    bm, bn = min(bm, m), min(bn, d)
    return pl.pallas_call(
        functools.partial(_qkv_kernel, n_norm_blocks=2 * d // bn),
        grid=(m // bm, n // bn),
        in_specs=[
            pl.BlockSpec((bm, d), lambda i, j: (i, 0)),
            pl.BlockSpec((1, d), lambda i, j: (0, 0)),
            pl.BlockSpec((d, bn), lambda i, j: (0, j)),
            pl.BlockSpec((1, bn), lambda i, j: (0, j)),
        ],
        out_specs=pl.BlockSpec((bm, bn), lambda i, j: (i, j)),
        out_shape=jax.ShapeDtypeStruct((m, n), BF16),
        scratch_shapes=[pltpu.VMEM((bm, d), BF16)],
        compiler_params=_params("arbitrary", "arbitrary"),
        name="qkv_proj",
    )(x, g, w, hg)


def _attn_kernel(q_ref, k_ref, v_ref, km_ref, vm_ref, bias_ref, o_ref, *, rb, n_rows):
    rblk = pl.program_id(2)
    km = km_ref[0, 0]
    vm = vm_ref[0, 0]

    def body(rr, carry):
        r = rblk * rb + rr
        r0 = jnp.clip(r - WIN_ROWS // 2, 0, n_rows - WIN_ROWS)
        d = r - r0
        start = pl.multiple_of(r0 * GRID_W, GRID_W)
        qs = pl.multiple_of(rr * GRID_W, GRID_W)
        q = q_ref[pl.ds(qs, GRID_W), :]
        kw = k_ref[pl.ds(start, N_LOC), :]
        vw = v_ref[pl.ds(start, N_LOC), :]
        s = jnp.concatenate([_dot_nt(q, kw), _dot_nt(q, km)], axis=-1) + bias_ref[d, 0]
        p = jnp.exp(s - jnp.max(s, axis=-1, keepdims=True))
        l = jnp.sum(p, axis=-1, keepdims=True)
        pb = p.astype(BF16)
        o = _dot(pb[:, :N_LOC], vw) + _dot(pb[:, N_LOC:], vm)
        o_ref[pl.ds(qs, GRID_W), :] = (o / l).astype(BF16)
        return carry

    lax.fori_loop(0, rb, body, 0)


def _attention(qkv, km, vm, bias, *, batch, seq, rb):
    n, d3 = qkv.shape
    d = d3 // 3
    h = d // HEAD_DIM
    n_rows = seq // GRID_W
    rb = min(rb, n_rows)
    nrb = n_rows // rb
    kv_spec = lambda off: pl.BlockSpec((seq, HEAD_DIM), lambda b, hh, r: (b, off + hh))
    meta_spec = pl.BlockSpec((1, 1, LANES, HEAD_DIM), lambda b, hh, r: (b, hh, 0, 0))
    return pl.pallas_call(
        functools.partial(_attn_kernel, rb=rb, n_rows=n_rows),
        grid=(batch, h, nrb),
        in_specs=[
            pl.BlockSpec((rb * GRID_W, HEAD_DIM), lambda b, hh, r: (b * nrb + r, hh)),
            kv_spec(h),
            kv_spec(2 * h),
            meta_spec,
            meta_spec,
            pl.BlockSpec((WIN_ROWS, 1, GRID_W, N_LOC + LANES), lambda b, hh, r: (0, hh, 0, 0)),
        ],
        out_specs=pl.BlockSpec((rb * GRID_W, HEAD_DIM), lambda b, hh, r: (b * nrb + r, hh)),
        out_shape=jax.ShapeDtypeStruct((n, d), BF16),
        compiler_params=_params("arbitrary", "arbitrary", "arbitrary"),
        name="nbr_attention",
    )(qkv, qkv, qkv, km, vm, bias)


def _meta_attn_kernel(q_ref, k_ref, v_ref, mb_ref, o_ref):
    s = _dot_nt(q_ref[...], k_ref[...]) + mb_ref[0]
    p = jnp.exp(s - jnp.max(s, axis=-1, keepdims=True))
    l = jnp.sum(p, axis=-1, keepdims=True)
    o_ref[...] = (_dot(p.astype(BF16), v_ref[...]) / l).astype(BF16)


def _meta_attention(qkv_m, meta_bias, *, batch):
    d = qkv_m.shape[1] // 3
    h = d // HEAD_DIM
    spec = lambda off: pl.BlockSpec((N_META, HEAD_DIM), lambda b, hh: (b, off + hh))
    return pl.pallas_call(
        _meta_attn_kernel,
        grid=(batch, h),
        in_specs=[spec(0), spec(h), spec(2 * h),
                  pl.BlockSpec((1, 1, N_META), lambda b, hh: (hh, 0, 0))],
        out_specs=spec(0),
        out_shape=jax.ShapeDtypeStruct((batch * N_META, d), BF16),
        compiler_params=_params("arbitrary", "arbitrary"),
        name="meta_attention",
    )(qkv_m, qkv_m, qkv_m, meta_bias.reshape(h, 1, N_META))


def _attention_bias(rel_bias, meta_bias):
    h = rel_bias.shape[0]
    cq = np.arange(GRID_W)
    c0 = np.clip(cq - WIN_COLS // 2, 0, GRID_W - WIN_COLS)
    ck = np.arange(GRID_W)
    in_win = (ck[None, :] >= c0[:, None]) & (ck[None, :] < c0[:, None] + WIN_COLS)
    dc = np.clip(ck[None, :] - cq[:, None] + WIN_COLS - 1, 0, 2 * WIN_COLS - 2)
    off = np.arange(WIN_ROWS)
    dr = np.arange(WIN_ROWS)[None, :] - off[:, None] + WIN_ROWS - 1
    loc = rel_bias[:, dr[:, :, None, None], dc[None, None]]
    loc = jnp.where(in_win[None, None, None], loc, MASK_VALUE)
    loc = loc.transpose(1, 0, 3, 2, 4).reshape(WIN_ROWS, h, GRID_W, N_LOC)
    met = jnp.concatenate([meta_bias, jnp.full((h, LANES - N_META), MASK_VALUE, F32)], axis=-1)
    met = jnp.broadcast_to(met[None, :, None, :], (WIN_ROWS, h, GRID_W, LANES))
    return jnp.concatenate([loc, met], axis=-1).astype(F32)


def _proj_kernel(a_ref, w_ref, r_ref, o_ref):
    o_ref[...] = r_ref[...] + _dot(a_ref[...], w_ref[...])


def _proj_residual(a, w, r, *, bm):
    m, d = r.shape
    bm = min(bm, m)
    row = pl.BlockSpec((bm, d), lambda i: (i, 0))
    return pl.pallas_call(
        _proj_kernel,
        grid=(m // bm,),
        in_specs=[row, pl.BlockSpec((d, d), lambda i: (0, 0)), row],
        out_specs=row,
        out_shape=jax.ShapeDtypeStruct((m, d), F32),
        compiler_params=_params("arbitrary"),
        name="attn_out_proj",
    )(a, w, r)


def _conv_proj_kernel(bg_ref, z_ref, zp_ref, zn_ref, zm_ref, cw_ref, w_ref, r_ref, o_ref, *, blocks_per_seq):
    i = pl.program_id(0)
    z = z_ref[...].astype(F32)
    bm = z.shape[0]
    first = (i % blocks_per_seq) == 0
    last = (i % blocks_per_seq) == blocks_per_seq - 1
    prev_row = jnp.where(first, zm_ref[0], zp_ref[BF16_SUBLANES - 1:BF16_SUBLANES, :].astype(F32))
    next_row = jnp.where(last, 0.0, zn_ref[0:1, :].astype(F32))
    row = lax.broadcasted_iota(jnp.int32, z.shape, 0)
    z_prev = jnp.where(row == 0, prev_row, pltpu.roll(z, 1, axis=0))
    z_next = jnp.where(row == bm - 1, next_row, pltpu.roll(z, bm - 1, axis=0))
    y = z_prev * cw_ref[0:1, :] + z * cw_ref[1:2, :] + z_next * cw_ref[2:3, :]
    a = (bg_ref[...].astype(F32) * y).astype(BF16)
    o_ref[...] = r_ref[...] + _dot(a, w_ref[...])


def _conv_proj_residual(bg, z, z_meta_last, conv_w_t, w, r, *, seq, bm):
    m, d = r.shape
    bm = min(bm, seq)
    bps = seq // bm
    halo = bm // BF16_SUBLANES
    n_halo = m // BF16_SUBLANES
    row = pl.BlockSpec((bm, d), lambda i: (i, 0))
    return pl.pallas_call(
        functools.partial(_conv_proj_kernel, blocks_per_seq=bps),
        grid=(m // bm,),
        in_specs=[
            row, row,
            pl.BlockSpec((BF16_SUBLANES, d), lambda i: (jnp.maximum(i * halo - 1, 0), 0)),
            pl.BlockSpec((BF16_SUBLANES, d), lambda i: (jnp.minimum((i + 1) * halo, n_halo - 1), 0)),
            pl.BlockSpec((1, 1, d), lambda i: (i // bps, 0, 0)),
            pl.BlockSpec((3, d), lambda i: (0, 0)),
            pl.BlockSpec((d, d), lambda i: (0, 0)),
            row,
        ],
        out_specs=row,
        out_shape=jax.ShapeDtypeStruct((m, d), F32),
        compiler_params=_params("arbitrary"),
        name="conv_out_proj",
    )(bg, z, z, z, z_meta_last, conv_w_t, w, r)


def _win_kernel(h_ref, g_ref, wb_ref, wc_ref, wu_ref, bg_ref, z_ref, xn_ref):
    @pl.when(pl.program_id(1) == 0)
    def _():
        xn_ref[...] = _rms(h_ref[...], g_ref[...]).astype(BF16)

    xn = xn_ref[...]
    bg_ref[...] = _dot(xn, wb_ref[...]).astype(BF16)
    z_ref[...] = (_dot(xn, wc_ref[...]) * _dot(xn, wu_ref[...])).astype(BF16)


def _conv_in_proj(h, g, w_in, *, bm, bn):
    m, d = h.shape
    bm, bn = min(bm, m), min(bn, d)
    nb = d // bn
    wspec = lambda part: pl.BlockSpec((d, bn), lambda i, j: (0, part * nb + j))
    out = pl.BlockSpec((bm, bn), lambda i, j: (i, j))
    return pl.pallas_call(
        _win_kernel,
        grid=(m // bm, nb),
        in_specs=[pl.BlockSpec((bm, d), lambda i, j: (i, 0)), pl.BlockSpec((1, d), lambda i, j: (0, 0)),
                  wspec(0), wspec(1), wspec(2)],
        out_specs=[out, out],
        out_shape=[jax.ShapeDtypeStruct((m, d), BF16), jax.ShapeDtypeStruct((m, d), BF16)],
        scratch_shapes=[pltpu.VMEM((bm, d), BF16)],
        compiler_params=_params("arbitrary", "arbitrary"),
        name="conv_in_proj",
    )(h, g, w_in, w_in, w_in)


def _ffn_kernel(h_ref, g_ref, wg_ref, wu_ref, wd_ref, o_ref, xn_ref, acc_ref):
    f = pl.program_id(1)

    @pl.when(f == 0)
    def _():
        xn_ref[...] = _rms(h_ref[...], g_ref[...]).astype(BF16)
        acc_ref[...] = jnp.zeros_like(acc_ref)

    xn = xn_ref[...]
    a = (jax.nn.silu(_dot(xn, wg_ref[...])) * _dot(xn, wu_ref[...])).astype(BF16)
    acc_ref[...] += _dot(a, wd_ref[...])

    @pl.when(f == pl.num_programs(1) - 1)
    def _():
        o_ref[...] = h_ref[...] + acc_ref[...]


def _dense_ffn(h, g, wg, wu, wd, *, bm, bf):
    m, d = h.shape
    ff = wg.shape[1]
    bm, bf = min(bm, m), min(bf, ff)
    row = pl.BlockSpec((bm, d), lambda i, f: (i, 0))
    return pl.pallas_call(
        _ffn_kernel,
        grid=(m // bm, ff // bf),
        in_specs=[row, pl.BlockSpec((1, d), lambda i, f: (0, 0)),
                  pl.BlockSpec((d, bf), lambda i, f: (0, f)),
                  pl.BlockSpec((d, bf), lambda i, f: (0, f)),
                  pl.BlockSpec((bf, d), lambda i, f: (f, 0))],
        out_specs=row,
        out_shape=jax.ShapeDtypeStruct((m, d), F32),
        scratch_shapes=[pltpu.VMEM((bm, d), BF16), pltpu.VMEM((bm, d), F32)],
        compiler_params=_params("arbitrary", "arbitrary"),
        name="dense_swiglu",
    )(h, g, wg, wu, wd)


def _router_kernel(h_ref, g_ref, wr_ref, o_ref, *, n_experts):
    xn = _rms(h_ref[...], g_ref[...])
    logits = jnp.dot(xn, wr_ref[...], preferred_element_type=F32, precision=lax.Precision.HIGHEST)
    lane = lax.broadcasted_iota(jnp.int32, logits.shape, 1)
    neg = -jnp.inf
    l1 = jnp.where(lane < n_experts, logits, neg)
    m1 = jnp.max(l1, axis=-1, keepdims=True)
    i1 = jnp.min(jnp.where(l1 == m1, lane, LANES), axis=-1, keepdims=True)
    l2 = jnp.where(lane == i1, neg, l1)
    m2 = jnp.max(l2, axis=-1, keepdims=True)
    i2 = jnp.min(jnp.where(l2 == m2, lane, LANES), axis=-1, keepdims=True)
    t = jnp.exp(m2 - m1)
    g1 = 1.0 / (1.0 + t)
    g2 = t / (1.0 + t)
    o_ref[...] = jnp.where(lane == 0, i1.astype(F32),
                           jnp.where(lane == 1, i2.astype(F32),
                                     jnp.where(lane == 2, g1, jnp.where(lane == 3, g2, 0.0))))


def _router(h, g, wr_pad, *, n_experts, bm):
    m, d = h.shape
    bm = min(bm, m)
    return pl.pallas_call(
        functools.partial(_router_kernel, n_experts=n_experts),
        grid=(m // bm,),
        in_specs=[pl.BlockSpec((bm, d), lambda i: (i, 0)), pl.BlockSpec((1, d), lambda i: (0, 0)),
                  pl.BlockSpec((d, LANES), lambda i: (0, 0))],
        out_specs=pl.BlockSpec((bm, LANES), lambda i: (i, 0)),
        out_shape=jax.ShapeDtypeStruct((m, LANES), F32),
        compiler_params=_params("arbitrary"),
        name="moe_router",
    )(h, g, wr_pad)


def _gather_kernel(idx_ref, src_ref, dst_ref, sem, *, bm):
    i = pl.program_id(0)

    def row_copy(src_row, dst_row):
        return pltpu.make_async_copy(src_ref.at[pl.ds(src_row, 1)], dst_ref.at[pl.ds(dst_row, 1)], sem)

    def issue(r, c):
        row_copy(idx_ref[0, 0, r], i * bm + r).start()
        return c

    def drain(r, c):
        row_copy(0, i * bm + r).wait()
        return c

    lax.fori_loop(0, bm, issue, 0)
    lax.fori_loop(0, bm, drain, 0)


def _gather_rows(idx, src, *, bm):
    nblk = idx.shape[0] // bm
    d = src.shape[1]
    return pl.pallas_call(
        functools.partial(_gather_kernel, bm=bm),
        grid=(nblk,),
        in_specs=[pl.BlockSpec((1, 1, bm), lambda i: (i, 0, 0), memory_space=pltpu.SMEM),
                  pl.BlockSpec(memory_space=pl.ANY)],
        out_specs=pl.BlockSpec(memory_space=pl.ANY),
        out_shape=jax.ShapeDtypeStruct((nblk * bm, d), src.dtype),
        scratch_shapes=[pltpu.SemaphoreType.DMA(())],
        compiler_params=_params("arbitrary"),
        name="moe_gather",
    )(idx.reshape(nblk, 1, bm), src)


def _expert_kernel(be_ref, nreal_ref, x_ref, gate_ref, g_ref, wg_ref, wu_ref, wd_ref, o_ref, xn_ref, acc_ref):
    i = pl.program_id(0)
    f = pl.program_id(1)

    @pl.when(i < nreal_ref[0])
    def _():
        @pl.when(f == 0)
        def _():
            xn_ref[...] = _rms(x_ref[...], g_ref[...]).astype(BF16)
            acc_ref[...] = jnp.zeros_like(acc_ref)

        xn = xn_ref[...]
        a = (jax.nn.silu(_dot(xn, wg_ref[0])) * _dot(xn, wu_ref[0])).astype(BF16)
        acc_ref[...] += _dot(a, wd_ref[0])

        @pl.when(f == pl.num_programs(1) - 1)
        def _():
            o_ref[...] = acc_ref[...] * gate_ref[...]

    @pl.when(jnp.logical_and(i >= nreal_ref[0], f == pl.num_programs(1) - 1))
    def _():
        o_ref[...] = jnp.zeros_like(o_ref)


def _expert_ffn(block_expert, n_real, xbuf, gate, g, wg, wu, wd, *, bm, bf):
    rows, d = xbuf.shape
    ff = wg.shape[2]
    bf = min(bf, ff)
    nblk, nf = rows // bm, ff // bf

    def blk(i, nr):
        return jnp.minimum(i, nr[0] - 1)

    def fblk(i, f, nr):
        return jnp.where(i < nr[0], f, nf - 1)

    row = pl.BlockSpec((bm, d), lambda i, f, be, nr: (blk(i, nr), 0))
    return pl.pallas_call(
        _expert_kernel,
        grid_spec=pltpu.PrefetchScalarGridSpec(
            num_scalar_prefetch=2,
            grid=(nblk, nf),
            in_specs=[
                row,
                pl.BlockSpec((bm, 1), lambda i, f, be, nr: (blk(i, nr), 0)),
                pl.BlockSpec((1, d), lambda i, f, be, nr: (0, 0)),
                pl.BlockSpec((1, d, bf), lambda i, f, be, nr: (be[blk(i, nr)], 0, fblk(i, f, nr))),
                pl.BlockSpec((1, d, bf), lambda i, f, be, nr: (be[blk(i, nr)], 0, fblk(i, f, nr))),
                pl.BlockSpec((1, bf, d), lambda i, f, be, nr: (be[blk(i, nr)], fblk(i, f, nr), 0)),
            ],
            out_specs=pl.BlockSpec((bm, d), lambda i, f, be, nr: (i, 0)),
            scratch_shapes=[pltpu.VMEM((bm, d), BF16), pltpu.VMEM((bm, d), F32)],
        ),
        out_shape=jax.ShapeDtypeStruct((rows, d), F32),
        compiler_params=_params("arbitrary", "arbitrary"),
        name="moe_experts",
    )(block_expert, n_real, xbuf, gate, g, wg, wu, wd)


def _combine_kernel(p0_ref, p1_ref, h_ref, y_ref, o_ref, ybuf, sem, *, tm):
    def row_copy(src_row, k, r):
        return pltpu.make_async_copy(y_ref.at[pl.ds(src_row, 1)], ybuf.at[k, pl.ds(r, 1)], sem)

    def issue(r, c):
        row_copy(p0_ref[0, 0, r], 0, r).start()
        row_copy(p1_ref[0, 0, r], 1, r).start()
        return c

    def drain(r, c):
        row_copy(0, 0, r).wait()
        row_copy(0, 1, r).wait()
        return c

    lax.fori_loop(0, tm, issue, 0)
    lax.fori_loop(0, tm, drain, 0)
    o_ref[...] = h_ref[...] + ybuf[0] + ybuf[1]


def _combine(p0, p1, h, y, *, tm):
    m, d = h.shape
    tm = min(tm, m)
    nt = m // tm
    idx = pl.BlockSpec((1, 1, tm), lambda i: (i, 0, 0), memory_space=pltpu.SMEM)
    row = pl.BlockSpec((tm, d), lambda i: (i, 0))
    return pl.pallas_call(
        functools.partial(_combine_kernel, tm=tm),
        grid=(nt,),
        in_specs=[idx, idx, row, pl.BlockSpec(memory_space=pl.ANY)],
        out_specs=row,
        out_shape=jax.ShapeDtypeStruct((m, d), F32),
        scratch_shapes=[pltpu.VMEM((2, tm, d), F32), pltpu.SemaphoreType.DMA(())],
        compiler_params=_params("arbitrary"),
        name="moe_combine",
    )(p0.reshape(nt, 1, tm), p1.reshape(nt, 1, tm), h, y)


def _route_plan(route, *, n_experts, bm):
    n = route.shape[0]
    flat_e = route[:, :TOP_K].astype(jnp.int32).reshape(-1)
    gates = route[:, TOP_K:2 * TOP_K].reshape(-1)
    onehot = (flat_e[:, None] == jnp.arange(n_experts, dtype=jnp.int32)[None, :]).astype(jnp.int32)
    csum = jnp.cumsum(onehot, axis=0)
    rank = jnp.take_along_axis(csum, flat_e[:, None], axis=1)[:, 0] - 1
    counts = csum[-1]
    blocks = (counts + bm - 1) // bm
    blk_end = jnp.cumsum(blocks)
    blk_start = blk_end - blocks
    rows = blk_start[flat_e] * bm + rank
    nblk = -(-(n * TOP_K) // bm) + n_experts
    block_expert = jnp.minimum(jnp.searchsorted(blk_end, jnp.arange(nblk), side="right"),
                               n_experts - 1).astype(jnp.int32)
    buf_tok = jnp.zeros((nblk * bm,), jnp.int32).at[rows].set(jnp.arange(n * TOP_K, dtype=jnp.int32) // TOP_K)
    buf_gate = jnp.zeros((nblk * bm,), F32).at[rows].set(gates)
    n_real = blk_end[-1:].astype(jnp.int32)
    return block_expert, n_real, buf_tok, buf_gate.reshape(-1, 1), rows[0::TOP_K], rows[1::TOP_K]


def kernel(x, meta_tokens, attn_norm, w_qkv, q_gain, k_gain, rel_bias, meta_bias, w_o, conv_norm, w_in, conv_w,
           w_out, dense_norm, w_gate, w_up, w_down, moe_norm, w_router, moe_gate, moe_up, moe_down):
    batch, seq, d = x.shape
    n = batch * seq
    h = d // HEAD_DIM
    n_experts = w_router.shape[-1]
    assert d % HEAD_DIM == 0 and seq % GRID_W == 0 and seq // GRID_W >= WIN_ROWS
    assert meta_tokens.shape[0] == N_META and rel_bias.shape[1:] == (h, 2 * WIN_ROWS - 1, 2 * WIN_COLS - 1)
    assert all(p.shape[0] == 1 for p in (attn_norm, conv_norm, dense_norm, moe_norm)), "two-layer trunk only"

    xt = x.reshape(n, d)
    mt = jnp.broadcast_to(meta_tokens[None].astype(x.dtype), (batch, N_META, d)).reshape(batch * N_META, d)
    nm = batch * N_META

    g_attn = attn_norm[0].reshape(1, d)
    wqkv = w_qkv[0].astype(BF16)
    head_gain = jnp.concatenate([jnp.tile(q_gain[0] * (HEAD_DIM ** -0.5), h), jnp.tile(k_gain[0], h),
                                 jnp.ones((d,), F32)]).reshape(1, 3 * d)
    qkv = _qkv(xt, g_attn, wqkv, head_gain, bm=1024, bn=1024)
    qkv_m = _qkv(mt, g_attn, wqkv, head_gain, bm=nm, bn=1024)

    def meta_heads(part):
        t = qkv_m[:, part * d:(part + 1) * d].reshape(batch, N_META, h, HEAD_DIM).transpose(0, 2, 1, 3)
        return jnp.pad(t, ((0, 0), (0, 0), (0, LANES - N_META), (0, 0)))

    bias = _attention_bias(rel_bias[0], meta_bias[0])
    o = _attention(qkv, meta_heads(1), meta_heads(2), bias, batch=batch, seq=seq, rb=8)
    o_m = _meta_attention(qkv_m, meta_bias[0], batch=batch)
    wo = w_o[0].astype(BF16)
    h1 = _proj_residual(o, wo, xt, bm=512)
    h1_m = _proj_residual(o_m, wo, mt, bm=nm)

    g_dense = dense_norm[0].reshape(1, d)
    wg, wu, wd = w_gate[0].astype(BF16), w_up[0].astype(BF16), w_down[0].astype(BF16)
    h2 = _dense_ffn(h1, g_dense, wg, wu, wd, bm=512, bf=512)
    h2_m = _dense_ffn(h1_m, g_dense, wg, wu, wd, bm=nm, bf=512)

    g_conv = conv_norm[0].reshape(1, d)
    win = w_in[0].astype(BF16)
    bg, z = _conv_in_proj(h2, g_conv, win, bm=1024, bn=512)
    _, z_m = _conv_in_proj(h2_m, g_conv, win, bm=nm, bn=512)
    z_meta_last = z_m.reshape(batch, N_META, d)[:, N_META - 1:, :].astype(F32)
    h3 = _conv_proj_residual(bg, z, z_meta_last, conv_w[0].T, w_out[0].astype(BF16), h2, seq=seq, bm=512)

    g_moe = moe_norm[0].reshape(1, d)
    wr_pad = jnp.pad(w_router[0], ((0, 0), (0, LANES - n_experts)))
    route = _router(h3, g_moe, wr_pad, n_experts=n_experts, bm=512)
    bm_e = 512
    block_expert, n_real, buf_tok, buf_gate, p0, p1 = _route_plan(route, n_experts=n_experts, bm=bm_e)
    xbuf = _gather_rows(buf_tok, h3, bm=bm_e)
    ybuf = _expert_ffn(block_expert, n_real, xbuf, buf_gate, g_moe, moe_gate[0].astype(BF16),
                       moe_up[0].astype(BF16), moe_down[0].astype(BF16), bm=bm_e, bf=512)
    out = _combine(p0, p1, h3, ybuf, tm=256)
    return out.reshape(batch, seq, d)
```

```python
import functools

import numpy as np
import jax
import jax.numpy as jnp
from jax import lax
from jax.experimental import pallas as pl
from jax.experimental.pallas import tpu as pltpu

HEAD_DIM = 128
GRID_W = 64
WIN_ROWS = 8
WIN_COLS = 16
N_META = 16
TOP_K = 2
RMS_EPS = 1e-6
LANES = 128
BF16_SUBLANES = 16
VMEM_LIMIT_BYTES = 56 * 1024 * 1024
MASK_VALUE = -1e30
N_LOC = WIN_ROWS * GRID_W
ATTN_ROW_GROUP = 8

F32 = jnp.float32
BF16 = jnp.bfloat16


def _params(*sem):
    return pltpu.CompilerParams(dimension_semantics=sem, vmem_limit_bytes=VMEM_LIMIT_BYTES)


def _rms(xf, g):
    return xf * lax.rsqrt(jnp.mean(xf * xf, axis=-1, keepdims=True) + RMS_EPS) * g


def _dot(a, b):
    return jnp.dot(a, b, preferred_element_type=F32)


def _dot_nt(a, b):
    return lax.dot_general(a, b, (((1,), (1,)), ((), ())), preferred_element_type=F32)


def _qkv_kernel(x_ref, g_ref, w_ref, hg_ref, o_ref, xn_ref, *, n_norm_blocks):
    j = pl.program_id(1)

    @pl.when(j == 0)
    def _():
        xn_ref[...] = _rms(x_ref[...], g_ref[...]).astype(BF16)

    acc = _dot(xn_ref[...], w_ref[...])

    @pl.when(j < n_norm_blocks)
    def _():
        for c in range(acc.shape[1] // HEAD_DIM):
            sl = slice(c * HEAD_DIM, (c + 1) * HEAD_DIM)
            o_ref[:, sl] = _rms(acc[:, sl], hg_ref[:, sl]).astype(BF16)

    @pl.when(j >= n_norm_blocks)
    def _():
        o_ref[...] = acc.astype(BF16)


def _qkv(x, g, w, hg, *, bm, bn):
    m, d = x.shape
    n = w.shape[1]
    bm, bn = min(bm, m), min(bn, d)
    return pl.pallas_call(
        functools.partial(_qkv_kernel, n_norm_blocks=2 * d // bn),
        grid=(m // bm, n // bn),
        in_specs=[
            pl.BlockSpec((bm, d), lambda i, j: (i, 0)),
            pl.BlockSpec((1, d), lambda i, j: (0, 0)),
            pl.BlockSpec((d, bn), lambda i, j: (0, j)),
            pl.BlockSpec((1, bn), lambda i, j: (0, j)),
        ],
        out_specs=pl.BlockSpec((bm, bn), lambda i, j: (i, j)),
        out_shape=jax.ShapeDtypeStruct((m, n), BF16),
        scratch_shapes=[pltpu.VMEM((bm, d), BF16)],
        compiler_params=_params("arbitrary", "arbitrary"),
        name="qkv_proj",
    )(x, g, w, hg)


def _attn_kernel(q_ref, k_ref, v_ref, km_ref, vm_ref, bias_ref, o_ref, *, rb, n_rows):
    rblk = pl.program_id(2)
    km = km_ref[0, 0]
    vm = vm_ref[0, 0]

    def group(gi, carry):
        scores, probs = [], []
        for u in range(ATTN_ROW_GROUP):
            rr = gi * ATTN_ROW_GROUP + u
            r = rblk * rb + rr
            r0 = jnp.clip(r - WIN_ROWS // 2, 0, n_rows - WIN_ROWS)
            start = pl.multiple_of(r0 * GRID_W, GRID_W)
            qs = pl.multiple_of(rr * GRID_W, GRID_W)
            q = q_ref[pl.ds(qs, GRID_W), :]
            kw = k_ref[pl.ds(start, N_LOC), :]
            s = jnp.concatenate([_dot_nt(q, kw), _dot_nt(q, km)], axis=-1) + bias_ref[r - r0, 0]
            scores.append((qs, start, s))
        for qs, start, s in scores:
            p = jnp.exp(s - jnp.max(s, axis=-1, keepdims=True))
            probs.append((qs, start, p.astype(BF16), jnp.sum(p, axis=-1, keepdims=True)))
        for qs, start, pb, l in probs:
            vw = v_ref[pl.ds(start, N_LOC), :]
            o = _dot(pb[:, :N_LOC], vw) + _dot(pb[:, N_LOC:], vm)
            o_ref[pl.ds(qs, GRID_W), :] = (o / l).astype(BF16)
        return carry

    lax.fori_loop(0, rb // ATTN_ROW_GROUP, group, 0)


def _attention(qkv, km, vm, bias, *, batch, seq, rb):
    n, d3 = qkv.shape
    d = d3 // 3
    h = d // HEAD_DIM
    n_rows = seq // GRID_W
    rb = min(rb, n_rows)
    nrb = n_rows // rb
    kv_spec = lambda off: pl.BlockSpec((seq, HEAD_DIM), lambda b, hh, r: (b, off + hh))
    meta_spec = pl.BlockSpec((1, 1, LANES, HEAD_DIM), lambda b, hh, r: (b, hh, 0, 0))
    return pl.pallas_call(
        functools.partial(_attn_kernel, rb=rb, n_rows=n_rows),
        grid=(batch, h, nrb),
        in_specs=[
            pl.BlockSpec((rb * GRID_W, HEAD_DIM), lambda b, hh, r: (b * nrb + r, hh)),
            kv_spec(h),
            kv_spec(2 * h),
            meta_spec,
            meta_spec,
            pl.BlockSpec((WIN_ROWS, 1, GRID_W, N_LOC + LANES), lambda b, hh, r: (0, hh, 0, 0)),
        ],
        out_specs=pl.BlockSpec((rb * GRID_W, HEAD_DIM), lambda b, hh, r: (b * nrb + r, hh)),
        out_shape=jax.ShapeDtypeStruct((n, d), BF16),
        compiler_params=_params("arbitrary", "arbitrary", "arbitrary"),
        name="nbr_attention",
    )(qkv, qkv, qkv, km, vm, bias)


def _meta_attn_kernel(q_ref, k_ref, v_ref, mb_ref, o_ref):
    s = _dot_nt(q_ref[...], k_ref[...]) + mb_ref[0]
    p = jnp.exp(s - jnp.max(s, axis=-1, keepdims=True))
    l = jnp.sum(p, axis=-1, keepdims=True)
    o_ref[...] = (_dot(p.astype(BF16), v_ref[...]) / l).astype(BF16)


def _meta_attention(qkv_m, meta_bias, *, batch):
    d = qkv_m.shape[1] // 3
    h = d // HEAD_DIM
    spec = lambda off: pl.BlockSpec((N_META, HEAD_DIM), lambda b, hh: (b, off + hh))
    return pl.pallas_call(
        _meta_attn_kernel,
        grid=(batch, h),
        in_specs=[spec(0), spec(h), spec(2 * h),
                  pl.BlockSpec((1, 1, N_META), lambda b, hh: (hh, 0, 0))],
        out_specs=spec(0),
        out_shape=jax.ShapeDtypeStruct((batch * N_META, d), BF16),
        compiler_params=_params("arbitrary", "arbitrary"),
        name="meta_attention",
    )(qkv_m, qkv_m, qkv_m, meta_bias.reshape(h, 1, N_META))


def _attention_bias(rel_bias, meta_bias):
    h = rel_bias.shape[0]
    cq = np.arange(GRID_W)
    c0 = np.clip(cq - WIN_COLS // 2, 0, GRID_W - WIN_COLS)
    ck = np.arange(GRID_W)
    in_win = (ck[None, :] >= c0[:, None]) & (ck[None, :] < c0[:, None] + WIN_COLS)
    dc = ck[None, :] - cq[:, None] + WIN_COLS - 1
    by_col = jnp.full((h, 2 * WIN_ROWS - 1, GRID_W, GRID_W), MASK_VALUE, F32)
    for j in range(2 * WIN_COLS - 1):
        by_col = jnp.where((in_win & (dc == j))[None, None], rel_bias[:, :, j, None, None], by_col)
    loc = jnp.stack([by_col[:, WIN_ROWS - 1 - off:2 * WIN_ROWS - 1 - off] for off in range(WIN_ROWS)])
    loc = loc.transpose(0, 1, 3, 2, 4).reshape(WIN_ROWS, h, GRID_W, N_LOC)
    met = jnp.concatenate([meta_bias, jnp.full((h, LANES - N_META), MASK_VALUE, F32)], axis=-1)
    met = jnp.broadcast_to(met[None, :, None, :], (WIN_ROWS, h, GRID_W, LANES))
    return jnp.concatenate([loc, met], axis=-1).astype(F32)


def _proj_kernel(a_ref, w_ref, r_ref, o_ref):
    o_ref[...] = r_ref[...] + _dot(a_ref[...], w_ref[...])


def _proj_residual(a, w, r, *, bm):
    m, d = r.shape
    bm = min(bm, m)
    row = pl.BlockSpec((bm, d), lambda i: (i, 0))
    return pl.pallas_call(
        _proj_kernel,
        grid=(m // bm,),
        in_specs=[row, pl.BlockSpec((d, d), lambda i: (0, 0)), row],
        out_specs=row,
        out_shape=jax.ShapeDtypeStruct((m, d), F32),
        compiler_params=_params("arbitrary"),
        name="attn_out_proj",
    )(a, w, r)


def _conv_proj_kernel(bg_ref, z_ref, zp_ref, zn_ref, zm_ref, cw_ref, w_ref, r_ref, o_ref, *, blocks_per_seq):
    i = pl.program_id(0)
    z = z_ref[...].astype(F32)
    bm = z.shape[0]
    first = (i % blocks_per_seq) == 0
    last = (i % blocks_per_seq) == blocks_per_seq - 1
    prev_row = jnp.where(first, zm_ref[0], zp_ref[BF16_SUBLANES - 1:BF16_SUBLANES, :].astype(F32))
    next_row = jnp.where(last, 0.0, zn_ref[0:1, :].astype(F32))
    row = lax.broadcasted_iota(jnp.int32, z.shape, 0)
    z_prev = jnp.where(row == 0, prev_row, pltpu.roll(z, 1, axis=0))
    z_next = jnp.where(row == bm - 1, next_row, pltpu.roll(z, bm - 1, axis=0))
    y = z_prev * cw_ref[0:1, :] + z * cw_ref[1:2, :] + z_next * cw_ref[2:3, :]
    a = (bg_ref[...].astype(F32) * y).astype(BF16)
    o_ref[...] = r_ref[...] + _dot(a, w_ref[...])


def _conv_proj_residual(bg, z, z_meta_last, conv_w_t, w, r, *, seq, bm):
    m, d = r.shape
    bm = min(bm, seq)
    bps = seq // bm
    halo = bm // BF16_SUBLANES
    n_halo = m // BF16_SUBLANES
    row = pl.BlockSpec((bm, d), lambda i: (i, 0))
    return pl.pallas_call(
        functools.partial(_conv_proj_kernel, blocks_per_seq=bps),
        grid=(m // bm,),
        in_specs=[
            row, row,
            pl.BlockSpec((BF16_SUBLANES, d), lambda i: (jnp.maximum(i * halo - 1, 0), 0)),
            pl.BlockSpec((BF16_SUBLANES, d), lambda i: (jnp.minimum((i + 1) * halo, n_halo - 1), 0)),
            pl.BlockSpec((1, 1, d), lambda i: (i // bps, 0, 0)),
            pl.BlockSpec((3, d), lambda i: (0, 0)),
            pl.BlockSpec((d, d), lambda i: (0, 0)),
            row,
        ],
        out_specs=row,
        out_shape=jax.ShapeDtypeStruct((m, d), F32),
        compiler_params=_params("arbitrary"),
        name="conv_out_proj",
    )(bg, z, z, z, z_meta_last, conv_w_t, w, r)


def _win_kernel(h_ref, g_ref, wb_ref, wc_ref, wu_ref, bg_ref, z_ref, xn_ref):
    @pl.when(pl.program_id(1) == 0)
    def _():
        xn_ref[...] = _rms(h_ref[...], g_ref[...]).astype(BF16)

    xn = xn_ref[...]
    bg_ref[...] = _dot(xn, wb_ref[...]).astype(BF16)
    z_ref[...] = (_dot(xn, wc_ref[...]) * _dot(xn, wu_ref[...])).astype(BF16)


def _conv_in_proj(h, g, w_in, *, bm, bn):
    m, d = h.shape
    bm, bn = min(bm, m), min(bn, d)
    nb = d // bn
    wspec = lambda part: pl.BlockSpec((d, bn), lambda i, j: (0, part * nb + j))
    out = pl.BlockSpec((bm, bn), lambda i, j: (i, j))
    return pl.pallas_call(
        _win_kernel,
        grid=(m // bm, nb),
        in_specs=[pl.BlockSpec((bm, d), lambda i, j: (i, 0)), pl.BlockSpec((1, d), lambda i, j: (0, 0)),
                  wspec(0), wspec(1), wspec(2)],
        out_specs=[out, out],
        out_shape=[jax.ShapeDtypeStruct((m, d), BF16), jax.ShapeDtypeStruct((m, d), BF16)],
        scratch_shapes=[pltpu.VMEM((bm, d), BF16)],
        compiler_params=_params("arbitrary", "arbitrary"),
        name="conv_in_proj",
    )(h, g, w_in, w_in, w_in)


def _ffn_kernel(h_ref, g_ref, wg_ref, wu_ref, wd_ref, o_ref, xn_ref, acc_ref):
    f = pl.program_id(1)

    @pl.when(f == 0)
    def _():
        xn_ref[...] = _rms(h_ref[...], g_ref[...]).astype(BF16)
        acc_ref[...] = jnp.zeros_like(acc_ref)

    xn = xn_ref[...]
    a = (jax.nn.silu(_dot(xn, wg_ref[...])) * _dot(xn, wu_ref[...])).astype(BF16)
    acc_ref[...] += _dot(a, wd_ref[...])

    @pl.when(f == pl.num_programs(1) - 1)
    def _():
        o_ref[...] = h_ref[...] + acc_ref[...]


def _dense_ffn(h, g, wg, wu, wd, *, bm, bf):
    m, d = h.shape
    ff = wg.shape[1]
    bm, bf = min(bm, m), min(bf, ff)
    row = pl.BlockSpec((bm, d), lambda i, f: (i, 0))
    return pl.pallas_call(
        _ffn_kernel,
        grid=(m // bm, ff // bf),
        in_specs=[row, pl.BlockSpec((1, d), lambda i, f: (0, 0)),
                  pl.BlockSpec((d, bf), lambda i, f: (0, f)),
                  pl.BlockSpec((d, bf), lambda i, f: (0, f)),
                  pl.BlockSpec((bf, d), lambda i, f: (f, 0))],
        out_specs=row,
        out_shape=jax.ShapeDtypeStruct((m, d), F32),
        scratch_shapes=[pltpu.VMEM((bm, d), BF16), pltpu.VMEM((bm, d), F32)],
        compiler_params=_params("arbitrary", "arbitrary"),
        name="dense_swiglu",
    )(h, g, wg, wu, wd)


def _router_kernel(h_ref, g_ref, wr_ref, o_ref, *, n_experts):
    xn = _rms(h_ref[...], g_ref[...])
    logits = jnp.dot(xn, wr_ref[...], preferred_element_type=F32, precision=lax.Precision.HIGHEST)
    lane = lax.broadcasted_iota(jnp.int32, logits.shape, 1)
    neg = -jnp.inf
    l1 = jnp.where(lane < n_experts, logits, neg)
    m1 = jnp.max(l1, axis=-1, keepdims=True)
    i1 = jnp.min(jnp.where(l1 == m1, lane, LANES), axis=-1, keepdims=True)
    l2 = jnp.where(lane == i1, neg, l1)
    m2 = jnp.max(l2, axis=-1, keepdims=True)
    i2 = jnp.min(jnp.where(l2 == m2, lane, LANES), axis=-1, keepdims=True)
    t = jnp.exp(m2 - m1)
    g1 = 1.0 / (1.0 + t)
    g2 = t / (1.0 + t)
    o_ref[...] = jnp.where(lane == 0, i1.astype(F32),
                           jnp.where(lane == 1, i2.astype(F32),
                                     jnp.where(lane == 2, g1, jnp.where(lane == 3, g2, 0.0))))


def _router(h, g, wr_pad, *, n_experts, bm):
    m, d = h.shape
    bm = min(bm, m)
    return pl.pallas_call(
        functools.partial(_router_kernel, n_experts=n_experts),
        grid=(m // bm,),
        in_specs=[pl.BlockSpec((bm, d), lambda i: (i, 0)), pl.BlockSpec((1, d), lambda i: (0, 0)),
                  pl.BlockSpec((d, LANES), lambda i: (0, 0))],
        out_specs=pl.BlockSpec((bm, LANES), lambda i: (i, 0)),
        out_shape=jax.ShapeDtypeStruct((m, LANES), F32),
        compiler_params=_params("arbitrary"),
        name="moe_router",
    )(h, g, wr_pad)


def _expert_kernel(be_ref, nreal_ref, tok_ref, tok_next_ref, h_ref, g_ref, wg_ref, wu_ref, wd_ref, o_ref,
                   xg_ref, xn_ref, acc_ref, sem):
    i = pl.program_id(0)
    f = pl.program_id(1)
    bm = xn_ref.shape[0]
    slot = i % 2

    def row_copy(tok, s, r):
        return pltpu.make_async_copy(h_ref.at[pl.ds(tok, 1)], xg_ref.at[s, pl.ds(r, 1)], sem.at[s])

    def start_gather(idx_ref, s):
        def body(r, c):
            row_copy(idx_ref[0, 0, r], s, r).start()
            return c
        lax.fori_loop(0, bm, body, 0)

    def wait_gather(s):
        def body(r, c):
            row_copy(0, s, r).wait()
            return c
        lax.fori_loop(0, bm, body, 0)

    @pl.when(jnp.logical_and(i == 0, f == 0))
    def _():
        start_gather(tok_ref, 0)

    @pl.when(i < nreal_ref[0])
    def _():
        @pl.when(f == 0)
        def _():
            wait_gather(slot)
            xn_ref[...] = _rms(xg_ref[slot], g_ref[...]).astype(BF16)
            acc_ref[...] = jnp.zeros_like(acc_ref)

            @pl.when(i + 1 < nreal_ref[0])
            def _():
                start_gather(tok_next_ref, 1 - slot)

        xn = xn_ref[...]
        a = (jax.nn.silu(_dot(xn, wg_ref[0])) * _dot(xn, wu_ref[0])).astype(BF16)
        acc_ref[...] += _dot(a, wd_ref[0])

        @pl.when(f == pl.num_programs(1) - 1)
        def _():
            o_ref[...] = acc_ref[...]

    @pl.when(jnp.logical_and(i >= nreal_ref[0], f == pl.num_programs(1) - 1))
    def _():
        o_ref[...] = jnp.zeros_like(o_ref)


def _expert_ffn(block_expert, n_real, buf_tok, h, g, wg, wu, wd, *, bm, bf):
    d = h.shape[1]
    ff = wg.shape[2]
    bf = min(bf, ff)
    nblk, nf = buf_tok.shape[0] // bm, ff // bf
    tok = buf_tok.reshape(nblk, 1, bm)

    def blk(i, nr):
        return jnp.minimum(i, nr[0] - 1)

    def fblk(i, f, nr):
        return jnp.where(i < nr[0], f, nf - 1)

    return pl.pallas_call(
        _expert_kernel,
        grid_spec=pltpu.PrefetchScalarGridSpec(
            num_scalar_prefetch=2,
            grid=(nblk, nf),
            in_specs=[
                pl.BlockSpec((1, 1, bm), lambda i, f, be, nr: (i, 0, 0), memory_space=pltpu.SMEM),
                pl.BlockSpec((1, 1, bm), lambda i, f, be, nr: (jnp.minimum(i + 1, nblk - 1), 0, 0),
                             memory_space=pltpu.SMEM),
                pl.BlockSpec(memory_space=pl.ANY),
                pl.BlockSpec((1, d), lambda i, f, be, nr: (0, 0)),
                pl.BlockSpec((1, d, bf), lambda i, f, be, nr: (be[blk(i, nr)], 0, fblk(i, f, nr))),
                pl.BlockSpec((1, d, bf), lambda i, f, be, nr: (be[blk(i, nr)], 0, fblk(i, f, nr))),
                pl.BlockSpec((1, bf, d), lambda i, f, be, nr: (be[blk(i, nr)], fblk(i, f, nr), 0)),
            ],
            out_specs=pl.BlockSpec((bm, d), lambda i, f, be, nr: (i, 0)),
            scratch_shapes=[pltpu.VMEM((2, bm, d), F32), pltpu.VMEM((bm, d), BF16), pltpu.VMEM((bm, d), F32),
                            pltpu.SemaphoreType.DMA((2,))],
        ),
        out_shape=jax.ShapeDtypeStruct((nblk * bm, d), F32),
        compiler_params=_params("arbitrary", "arbitrary"),
        name="moe_experts",
    )(block_expert, n_real, tok, tok, h, g, wg, wu, wd)


def _combine_kernel(p0_ref, p1_ref, h_ref, route_ref, y_ref, o_ref, ybuf, sem, *, tm):
    def row_copy(src_row, k, r):
        return pltpu.make_async_copy(y_ref.at[pl.ds(src_row, 1)], ybuf.at[k, pl.ds(r, 1)], sem)

    def issue(r, c):
        row_copy(p0_ref[0, 0, r], 0, r).start()
        row_copy(p1_ref[0, 0, r], 1, r).start()
        return c

    def drain(r, c):
        row_copy(0, 0, r).wait()
        row_copy(0, 1, r).wait()
        return c

    lax.fori_loop(0, tm, issue, 0)
    lax.fori_loop(0, tm, drain, 0)
    g0 = route_ref[:, TOP_K:TOP_K + 1]
    g1 = route_ref[:, TOP_K + 1:TOP_K + 2]
    o_ref[...] = h_ref[...] + (ybuf[0] * g0 + ybuf[1] * g1)


def _combine(p0, p1, h, route, y, *, tm):
    m, d = h.shape
    tm = min(tm, m)
    nt = m // tm
    idx = pl.BlockSpec((1, 1, tm), lambda i: (i, 0, 0), memory_space=pltpu.SMEM)
    row = pl.BlockSpec((tm, d), lambda i: (i, 0))
    return pl.pallas_call(
        functools.partial(_combine_kernel, tm=tm),
        grid=(nt,),
        in_specs=[idx, idx, row, pl.BlockSpec((tm, LANES), lambda i: (i, 0)), pl.BlockSpec(memory_space=pl.ANY)],
        out_specs=row,
        out_shape=jax.ShapeDtypeStruct((m, d), F32),
        scratch_shapes=[pltpu.VMEM((2, tm, d), F32), pltpu.SemaphoreType.DMA(())],
        compiler_params=_params("arbitrary"),
        name="moe_combine",
    )(p0.reshape(nt, 1, tm), p1.reshape(nt, 1, tm), h, route, y)


def _route_plan(route, *, n_experts, bm):
    n = route.shape[0]
    flat_e = route[:, :TOP_K].astype(jnp.int32).reshape(-1)
    onehot = (flat_e[:, None] == jnp.arange(n_experts, dtype=jnp.int32)[None, :]).astype(jnp.int32)
    csum = jnp.cumsum(onehot, axis=0)
    rank = jnp.take_along_axis(csum, flat_e[:, None], axis=1)[:, 0] - 1
    counts = csum[-1]
    blocks = (counts + bm - 1) // bm
    blk_end = jnp.cumsum(blocks)
    blk_start = blk_end - blocks
    rows = blk_start[flat_e] * bm + rank
    nblk = -(-(n * TOP_K) // bm) + n_experts
    block_expert = jnp.minimum(jnp.searchsorted(blk_end, jnp.arange(nblk), side="right"),
                               n_experts - 1).astype(jnp.int32)
    buf_tok = jnp.zeros((nblk * bm,), jnp.int32).at[rows].set(jnp.arange(n * TOP_K, dtype=jnp.int32) // TOP_K)
    n_real = blk_end[-1:].astype(jnp.int32)
    return block_expert, n_real, buf_tok, rows[0::TOP_K], rows[1::TOP_K]


def kernel(x, meta_tokens, attn_norm, w_qkv, q_gain, k_gain, rel_bias, meta_bias, w_o, conv_norm, w_in, conv_w,
           w_out, dense_norm, w_gate, w_up, w_down, moe_norm, w_router, moe_gate, moe_up, moe_down):
    batch, seq, d = x.shape
    n = batch * seq
    h = d // HEAD_DIM
    n_experts = w_router.shape[-1]
    assert d % HEAD_DIM == 0 and seq % GRID_W == 0 and seq // GRID_W >= WIN_ROWS
    assert meta_tokens.shape[0] == N_META and rel_bias.shape[1:] == (h, 2 * WIN_ROWS - 1, 2 * WIN_COLS - 1)
    assert all(p.shape[0] == 1 for p in (attn_norm, conv_norm, dense_norm, moe_norm)), "two-layer trunk only"

    xt = x.reshape(n, d)
    mt = jnp.broadcast_to(meta_tokens[None].astype(x.dtype), (batch, N_META, d)).reshape(batch * N_META, d)
    nm = batch * N_META

    g_attn = attn_norm[0].reshape(1, d)
    wqkv = w_qkv[0].astype(BF16)
    head_gain = jnp.concatenate([jnp.tile(q_gain[0] * (HEAD_DIM ** -0.5), h), jnp.tile(k_gain[0], h),
                                 jnp.ones((d,), F32)]).reshape(1, 3 * d)
    qkv = _qkv(xt, g_attn, wqkv, head_gain, bm=1024, bn=1024)
    qkv_m = _qkv(mt, g_attn, wqkv, head_gain, bm=nm, bn=1024)

    def meta_heads(part):
        t = qkv_m[:, part * d:(part + 1) * d].reshape(batch, N_META, h, HEAD_DIM).transpose(0, 2, 1, 3)
        return jnp.pad(t, ((0, 0), (0, 0), (0, LANES - N_META), (0, 0)))

    bias = _attention_bias(rel_bias[0], meta_bias[0])
    o = _attention(qkv, meta_heads(1), meta_heads(2), bias, batch=batch, seq=seq, rb=16)
    o_m = _meta_attention(qkv_m, meta_bias[0], batch=batch)
    wo = w_o[0].astype(BF16)
    h1 = _proj_residual(o, wo, xt, bm=512)
    h1_m = _proj_residual(o_m, wo, mt, bm=nm)

    g_dense = dense_norm[0].reshape(1, d)
    wg, wu, wd = w_gate[0].astype(BF16), w_up[0].astype(BF16), w_down[0].astype(BF16)
    h2 = _dense_ffn(h1, g_dense, wg, wu, wd, bm=512, bf=512)
    h2_m = _dense_ffn(h1_m, g_dense, wg, wu, wd, bm=nm, bf=512)

    g_conv = conv_norm[0].reshape(1, d)
    win = w_in[0].astype(BF16)
    bg, z = _conv_in_proj(h2, g_conv, win, bm=1024, bn=512)
    _, z_m = _conv_in_proj(h2_m, g_conv, win, bm=nm, bn=512)
    z_meta_last = z_m.reshape(batch, N_META, d)[:, N_META - 1:, :].astype(F32)
    h3 = _conv_proj_residual(bg, z, z_meta_last, conv_w[0].T, w_out[0].astype(BF16), h2, seq=seq, bm=512)

    g_moe = moe_norm[0].reshape(1, d)
    wr_pad = jnp.pad(w_router[0], ((0, 0), (0, LANES - n_experts)))
    route = _router(h3, g_moe, wr_pad, n_experts=n_experts, bm=512)
    bm_e = 512
    block_expert, n_real, buf_tok, p0, p1 = _route_plan(route, n_experts=n_experts, bm=bm_e)
    ybuf = _expert_ffn(block_expert, n_real, buf_tok, h3, g_moe, moe_gate[0].astype(BF16),
                       moe_up[0].astype(BF16), moe_down[0].astype(BF16), bm=bm_e, bf=512)
    out = _combine(p0, p1, h3, route, ybuf, tm=256)
    return out.reshape(batch, seq, d)
```

```python
import functools

import numpy as np
import jax
import jax.numpy as jnp
from jax import lax
from jax.experimental import pallas as pl
from jax.experimental.pallas import tpu as pltpu

HEAD_DIM = 128
GRID_W = 64
WIN_ROWS = 8
WIN_COLS = 16
N_META = 16
TOP_K = 2
RMS_EPS = 1e-6
LANES = 128
BF16_SUBLANES = 16
VMEM_LIMIT_BYTES = 56 * 1024 * 1024
MASK_VALUE = -1e30
N_LOC = WIN_ROWS * GRID_W
ATTN_ROW_GROUP = 8
DMA_LOOP_UNROLL = 8
FFN_BLOCK_F = 1024

F32 = jnp.float32
BF16 = jnp.bfloat16


def _params(*sem):
    return pltpu.CompilerParams(dimension_semantics=sem, vmem_limit_bytes=VMEM_LIMIT_BYTES)


def _rms(xf, g):
    return xf * lax.rsqrt(jnp.mean(xf * xf, axis=-1, keepdims=True) + RMS_EPS) * g


def _dot(a, b):
    return jnp.dot(a, b, preferred_element_type=F32)


def _dot_nt(a, b):
    return lax.dot_general(a, b, (((1,), (1,)), ((), ())), preferred_element_type=F32)


def _qkv_kernel(x_ref, g_ref, w_ref, hg_ref, o_ref, xn_ref, *, n_norm_blocks):
    j = pl.program_id(1)

    @pl.when(j == 0)
    def _():
        xn_ref[...] = _rms(x_ref[...], g_ref[...]).astype(BF16)

    acc = _dot(xn_ref[...], w_ref[...])

    @pl.when(j < n_norm_blocks)
    def _():
        for c in range(acc.shape[1] // HEAD_DIM):
            sl = slice(c * HEAD_DIM, (c + 1) * HEAD_DIM)
            o_ref[:, sl] = _rms(acc[:, sl], hg_ref[:, sl]).astype(BF16)

    @pl.when(j >= n_norm_blocks)
    def _():
        o_ref[...] = acc.astype(BF16)


def _qkv(x, g, w, hg, *, bm, bn):
    m, d = x.shape
    n = w.shape[1]
    bm, bn = min(bm, m), min(bn, d)
    return pl.pallas_call(
        functools.partial(_qkv_kernel, n_norm_blocks=2 * d // bn),
        grid=(m // bm, n // bn),
        in_specs=[
            pl.BlockSpec((bm, d), lambda i, j: (i, 0)),
            pl.BlockSpec((1, d), lambda i, j: (0, 0)),
            pl.BlockSpec((d, bn), lambda i, j: (0, j)),
            pl.BlockSpec((1, bn), lambda i, j: (0, j)),
        ],
        out_specs=pl.BlockSpec((bm, bn), lambda i, j: (i, j)),
        out_shape=jax.ShapeDtypeStruct((m, n), BF16),
        scratch_shapes=[pltpu.VMEM((bm, d), BF16)],
        compiler_params=_params("arbitrary", "arbitrary"),
        name="qkv_proj",
    )(x, g, w, hg)


def _attn_kernel(q_ref, k_ref, v_ref, km_ref, vm_ref, bias_ref, o_ref, *, rb, n_rows):
    rblk = pl.program_id(2)
    km = km_ref[0, 0]
    vm = vm_ref[0, 0]

    def group(gi, carry):
        scores, probs = [], []
        for u in range(ATTN_ROW_GROUP):
            rr = gi * ATTN_ROW_GROUP + u
            r = rblk * rb + rr
            r0 = jnp.clip(r - WIN_ROWS // 2, 0, n_rows - WIN_ROWS)
            start = pl.multiple_of(r0 * GRID_W, GRID_W)
            qs = pl.multiple_of(rr * GRID_W, GRID_W)
            q = q_ref[pl.ds(qs, GRID_W), :]
            kw = k_ref[pl.ds(start, N_LOC), :]
            s = jnp.concatenate([_dot_nt(q, kw), _dot_nt(q, km)], axis=-1) + bias_ref[r - r0, 0]
            scores.append((qs, start, s))
        for qs, start, s in scores:
            p = jnp.exp(s - jnp.max(s, axis=-1, keepdims=True))
            probs.append((qs, start, p.astype(BF16), jnp.sum(p, axis=-1, keepdims=True)))
        for qs, start, pb, l in probs:
            vw = v_ref[pl.ds(start, N_LOC), :]
            o = _dot(pb[:, :N_LOC], vw) + _dot(pb[:, N_LOC:], vm)
            o_ref[pl.ds(qs, GRID_W), :] = (o / l).astype(BF16)
        return carry

    lax.fori_loop(0, rb // ATTN_ROW_GROUP, group, 0)


def _attention(qkv, km, vm, bias, *, batch, seq, rb):
    n, d3 = qkv.shape
    d = d3 // 3
    h = d // HEAD_DIM
    n_rows = seq // GRID_W
    rb = min(rb, n_rows)
    nrb = n_rows // rb
    kv_spec = lambda off: pl.BlockSpec((seq, HEAD_DIM), lambda b, hh, r: (b, off + hh))
    meta_spec = pl.BlockSpec((1, 1, LANES, HEAD_DIM), lambda b, hh, r: (b, hh, 0, 0))
    return pl.pallas_call(
        functools.partial(_attn_kernel, rb=rb, n_rows=n_rows),
        grid=(batch, h, nrb),
        in_specs=[
            pl.BlockSpec((rb * GRID_W, HEAD_DIM), lambda b, hh, r: (b * nrb + r, hh)),
            kv_spec(h),
            kv_spec(2 * h),
            meta_spec,
            meta_spec,
            pl.BlockSpec((WIN_ROWS, 1, GRID_W, N_LOC + LANES), lambda b, hh, r: (0, hh, 0, 0)),
        ],
        out_specs=pl.BlockSpec((rb * GRID_W, HEAD_DIM), lambda b, hh, r: (b * nrb + r, hh)),
        out_shape=jax.ShapeDtypeStruct((n, d), BF16),
        compiler_params=_params("arbitrary", "arbitrary", "arbitrary"),
        name="nbr_attention",
    )(qkv, qkv, qkv, km, vm, bias)


def _meta_attn_kernel(q_ref, k_ref, v_ref, mb_ref, o_ref):
    s = _dot_nt(q_ref[...], k_ref[...]) + mb_ref[0]
    p = jnp.exp(s - jnp.max(s, axis=-1, keepdims=True))
    l = jnp.sum(p, axis=-1, keepdims=True)
    o_ref[...] = (_dot(p.astype(BF16), v_ref[...]) / l).astype(BF16)


def _meta_attention(qkv_m, meta_bias, *, batch):
    d = qkv_m.shape[1] // 3
    h = d // HEAD_DIM
    spec = lambda off: pl.BlockSpec((N_META, HEAD_DIM), lambda b, hh: (b, off + hh))
    return pl.pallas_call(
        _meta_attn_kernel,
        grid=(batch, h),
        in_specs=[spec(0), spec(h), spec(2 * h),
                  pl.BlockSpec((1, 1, N_META), lambda b, hh: (hh, 0, 0))],
        out_specs=spec(0),
        out_shape=jax.ShapeDtypeStruct((batch * N_META, d), BF16),
        compiler_params=_params("arbitrary", "arbitrary"),
        name="meta_attention",
    )(qkv_m, qkv_m, qkv_m, meta_bias.reshape(h, 1, N_META))


def _attention_bias(rel_bias, meta_bias):
    h = rel_bias.shape[0]
    cq = np.arange(GRID_W)
    c0 = np.clip(cq - WIN_COLS // 2, 0, GRID_W - WIN_COLS)
    ck = np.arange(GRID_W)
    in_win = (ck[None, :] >= c0[:, None]) & (ck[None, :] < c0[:, None] + WIN_COLS)
    dc = ck[None, :] - cq[:, None] + WIN_COLS - 1
    by_col = jnp.full((h, 2 * WIN_ROWS - 1, GRID_W, GRID_W), MASK_VALUE, F32)
    for j in range(2 * WIN_COLS - 1):
        by_col = jnp.where((in_win & (dc == j))[None, None], rel_bias[:, :, j, None, None], by_col)
    loc = jnp.stack([by_col[:, WIN_ROWS - 1 - off:2 * WIN_ROWS - 1 - off] for off in range(WIN_ROWS)])
    loc = loc.transpose(0, 1, 3, 2, 4).reshape(WIN_ROWS, h, GRID_W, N_LOC)
    met = jnp.concatenate([meta_bias, jnp.full((h, LANES - N_META), MASK_VALUE, F32)], axis=-1)
    met = jnp.broadcast_to(met[None, :, None, :], (WIN_ROWS, h, GRID_W, LANES))
    return jnp.concatenate([loc, met], axis=-1).astype(F32)


def _proj_kernel(a_ref, w_ref, r_ref, o_ref):
    o_ref[...] = r_ref[...] + _dot(a_ref[...], w_ref[...])


def _proj_residual(a, w, r, *, bm):
    m, d = r.shape
    bm = min(bm, m)
    row = pl.BlockSpec((bm, d), lambda i: (i, 0))
    return pl.pallas_call(
        _proj_kernel,
        grid=(m // bm,),
        in_specs=[row, pl.BlockSpec((d, d), lambda i: (0, 0)), row],
        out_specs=row,
        out_shape=jax.ShapeDtypeStruct((m, d), F32),
        compiler_params=_params("arbitrary"),
        name="attn_out_proj",
    )(a, w, r)


def _conv_proj_kernel(bg_ref, z_ref, zp_ref, zn_ref, zm_ref, cw_ref, w_ref, r_ref, o_ref, *, blocks_per_seq):
    i = pl.program_id(0)
    z = z_ref[...].astype(F32)
    bm = z.shape[0]
    first = (i % blocks_per_seq) == 0
    last = (i % blocks_per_seq) == blocks_per_seq - 1
    prev_row = jnp.where(first, zm_ref[0], zp_ref[BF16_SUBLANES - 1:BF16_SUBLANES, :].astype(F32))
    next_row = jnp.where(last, 0.0, zn_ref[0:1, :].astype(F32))
    row = lax.broadcasted_iota(jnp.int32, z.shape, 0)
    z_prev = jnp.where(row == 0, prev_row, pltpu.roll(z, 1, axis=0))
    z_next = jnp.where(row == bm - 1, next_row, pltpu.roll(z, bm - 1, axis=0))
    y = z_prev * cw_ref[0:1, :] + z * cw_ref[1:2, :] + z_next * cw_ref[2:3, :]
    a = (bg_ref[...].astype(F32) * y).astype(BF16)
    o_ref[...] = r_ref[...] + _dot(a, w_ref[...])


def _conv_proj_residual(bg, z, z_meta_last, conv_w_t, w, r, *, seq, bm):
    m, d = r.shape
    bm = min(bm, seq)
    bps = seq // bm
    halo = bm // BF16_SUBLANES
    n_halo = m // BF16_SUBLANES
    row = pl.BlockSpec((bm, d), lambda i: (i, 0))
    return pl.pallas_call(
        functools.partial(_conv_proj_kernel, blocks_per_seq=bps),
        grid=(m // bm,),
        in_specs=[
            row, row,
            pl.BlockSpec((BF16_SUBLANES, d), lambda i: (jnp.maximum(i * halo - 1, 0), 0)),
            pl.BlockSpec((BF16_SUBLANES, d), lambda i: (jnp.minimum((i + 1) * halo, n_halo - 1), 0)),
            pl.BlockSpec((1, 1, d), lambda i: (i // bps, 0, 0)),
            pl.BlockSpec((3, d), lambda i: (0, 0)),
            pl.BlockSpec((d, d), lambda i: (0, 0)),
            row,
        ],
        out_specs=row,
        out_shape=jax.ShapeDtypeStruct((m, d), F32),
        compiler_params=_params("arbitrary"),
        name="conv_out_proj",
    )(bg, z, z, z, z_meta_last, conv_w_t, w, r)


def _win_kernel(h_ref, g_ref, wb_ref, wc_ref, wu_ref, bg_ref, z_ref, xn_ref):
    @pl.when(pl.program_id(1) == 0)
    def _():
        xn_ref[...] = _rms(h_ref[...], g_ref[...]).astype(BF16)

    xn = xn_ref[...]
    bg_ref[...] = _dot(xn, wb_ref[...]).astype(BF16)
    z_ref[...] = (_dot(xn, wc_ref[...]) * _dot(xn, wu_ref[...])).astype(BF16)


def _conv_in_proj(h, g, w_in, *, bm, bn):
    m, d = h.shape
    bm, bn = min(bm, m), min(bn, d)
    nb = d // bn
    wspec = lambda part: pl.BlockSpec((d, bn), lambda i, j: (0, part * nb + j))
    out = pl.BlockSpec((bm, bn), lambda i, j: (i, j))
    return pl.pallas_call(
        _win_kernel,
        grid=(m // bm, nb),
        in_specs=[pl.BlockSpec((bm, d), lambda i, j: (i, 0)), pl.BlockSpec((1, d), lambda i, j: (0, 0)),
                  wspec(0), wspec(1), wspec(2)],
        out_specs=[out, out],
        out_shape=[jax.ShapeDtypeStruct((m, d), BF16), jax.ShapeDtypeStruct((m, d), BF16)],
        scratch_shapes=[pltpu.VMEM((bm, d), BF16)],
        compiler_params=_params("arbitrary", "arbitrary"),
        name="conv_in_proj",
    )(h, g, w_in, w_in, w_in)


def _ffn_kernel(h_ref, g_ref, wg_ref, wu_ref, wd_ref, o_ref, xn_ref):
    f = pl.program_id(1)

    @pl.when(f == 0)
    def _():
        xn_ref[...] = _rms(h_ref[...], g_ref[...]).astype(BF16)
        o_ref[...] = h_ref[...]

    xn = xn_ref[...]
    a = (jax.nn.silu(_dot(xn, wg_ref[...])) * _dot(xn, wu_ref[...])).astype(BF16)
    o_ref[...] += _dot(a, wd_ref[...])


def _dense_ffn(h, g, wg, wu, wd, *, bm, bf):
    m, d = h.shape
    ff = wg.shape[1]
    bm, bf = min(bm, m), min(bf, ff)
    row = pl.BlockSpec((bm, d), lambda i, f: (i, 0))
    return pl.pallas_call(
        _ffn_kernel,
        grid=(m // bm, ff // bf),
        in_specs=[row, pl.BlockSpec((1, d), lambda i, f: (0, 0)),
                  pl.BlockSpec((d, bf), lambda i, f: (0, f)),
                  pl.BlockSpec((d, bf), lambda i, f: (0, f)),
                  pl.BlockSpec((bf, d), lambda i, f: (f, 0))],
        out_specs=row,
        out_shape=jax.ShapeDtypeStruct((m, d), F32),
        scratch_shapes=[pltpu.VMEM((bm, d), BF16)],
        compiler_params=_params("arbitrary", "arbitrary"),
        name="dense_swiglu",
    )(h, g, wg, wu, wd)


def _router_kernel(h_ref, g_ref, wr_ref, o_ref, *, n_experts):
    xn = _rms(h_ref[...], g_ref[...])
    logits = jnp.dot(xn, wr_ref[...], preferred_element_type=F32, precision=lax.Precision.HIGHEST)
    lane = lax.broadcasted_iota(jnp.int32, logits.shape, 1)
    neg = -jnp.inf
    l1 = jnp.where(lane < n_experts, logits, neg)
    m1 = jnp.max(l1, axis=-1, keepdims=True)
    i1 = jnp.min(jnp.where(l1 == m1, lane, LANES), axis=-1, keepdims=True)
    l2 = jnp.where(lane == i1, neg, l1)
    m2 = jnp.max(l2, axis=-1, keepdims=True)
    i2 = jnp.min(jnp.where(l2 == m2, lane, LANES), axis=-1, keepdims=True)
    t = jnp.exp(m2 - m1)
    g1 = 1.0 / (1.0 + t)
    g2 = t / (1.0 + t)
    o_ref[...] = jnp.where(lane == 0, i1.astype(F32),
                           jnp.where(lane == 1, i2.astype(F32),
                                     jnp.where(lane == 2, g1, jnp.where(lane == 3, g2, 0.0))))


def _router(h, g, wr_pad, *, n_experts, bm):
    m, d = h.shape
    bm = min(bm, m)
    return pl.pallas_call(
        functools.partial(_router_kernel, n_experts=n_experts),
        grid=(m // bm,),
        in_specs=[pl.BlockSpec((bm, d), lambda i: (i, 0)), pl.BlockSpec((1, d), lambda i: (0, 0)),
                  pl.BlockSpec((d, LANES), lambda i: (0, 0))],
        out_specs=pl.BlockSpec((bm, LANES), lambda i: (i, 0)),
        out_shape=jax.ShapeDtypeStruct((m, LANES), F32),
        compiler_params=_params("arbitrary"),
        name="moe_router",
    )(h, g, wr_pad)


def _expert_kernel(be_ref, nreal_ref, tok_ref, tok_next_ref, h_ref, g_ref, wg_ref, wu_ref, wd_ref, o_ref,
                   xg_ref, xn_ref, sem):
    i = pl.program_id(0)
    f = pl.program_id(1)
    bm = xn_ref.shape[0]

    def row_copy(tok, r):
        return pltpu.make_async_copy(h_ref.at[pl.ds(tok, 1)], xg_ref.at[pl.ds(r, 1)], sem)

    def start_gather(idx_ref):
        def body(r, c):
            row_copy(idx_ref[0, 0, r], r).start()
            return c
        lax.fori_loop(0, bm, body, 0, unroll=DMA_LOOP_UNROLL)

    def wait_gather():
        def body(r, c):
            row_copy(0, r).wait()
            return c
        lax.fori_loop(0, bm, body, 0, unroll=DMA_LOOP_UNROLL)

    @pl.when(jnp.logical_and(i == 0, f == 0))
    def _():
        start_gather(tok_ref)

    @pl.when(i < nreal_ref[0])
    def _():
        @pl.when(f == 0)
        def _():
            wait_gather()
            xn_ref[...] = _rms(xg_ref[...], g_ref[...]).astype(BF16)
            o_ref[...] = jnp.zeros_like(o_ref)

            @pl.when(i + 1 < nreal_ref[0])
            def _():
                start_gather(tok_next_ref)

        xn = xn_ref[...]
        a = (jax.nn.silu(_dot(xn, wg_ref[0])) * _dot(xn, wu_ref[0])).astype(BF16)
        o_ref[...] += _dot(a, wd_ref[0])

    @pl.when(jnp.logical_and(i >= nreal_ref[0], f == pl.num_programs(1) - 1))
    def _():
        o_ref[...] = jnp.zeros_like(o_ref)


def _expert_ffn(block_expert, n_real, buf_tok, h, g, wg, wu, wd, *, bm, bf):
    d = h.shape[1]
    ff = wg.shape[2]
    bf = min(bf, ff)
    nblk, nf = buf_tok.shape[0] // bm, ff // bf
    tok = buf_tok.reshape(nblk, 1, bm)

    def blk(i, nr):
        return jnp.minimum(i, nr[0] - 1)

    def fblk(i, f, nr):
        return jnp.where(i < nr[0], f, nf - 1)

    return pl.pallas_call(
        _expert_kernel,
        grid_spec=pltpu.PrefetchScalarGridSpec(
            num_scalar_prefetch=2,
            grid=(nblk, nf),
            in_specs=[
                pl.BlockSpec((1, 1, bm), lambda i, f, be, nr: (i, 0, 0), memory_space=pltpu.SMEM),
                pl.BlockSpec((1, 1, bm), lambda i, f, be, nr: (jnp.minimum(i + 1, nblk - 1), 0, 0),
                             memory_space=pltpu.SMEM),
                pl.BlockSpec(memory_space=pl.ANY),
                pl.BlockSpec((1, d), lambda i, f, be, nr: (0, 0)),
                pl.BlockSpec((1, d, bf), lambda i, f, be, nr: (be[blk(i, nr)], 0, fblk(i, f, nr))),
                pl.BlockSpec((1, d, bf), lambda i, f, be, nr: (be[blk(i, nr)], 0, fblk(i, f, nr))),
                pl.BlockSpec((1, bf, d), lambda i, f, be, nr: (be[blk(i, nr)], fblk(i, f, nr), 0)),
            ],
            out_specs=pl.BlockSpec((bm, d), lambda i, f, be, nr: (i, 0)),
            scratch_shapes=[pltpu.VMEM((bm, d), F32), pltpu.VMEM((bm, d), BF16), pltpu.SemaphoreType.DMA(())],
        ),
        out_shape=jax.ShapeDtypeStruct((nblk * bm, d), F32),
        compiler_params=_params("arbitrary", "arbitrary"),
        name="moe_experts",
    )(block_expert, n_real, tok, tok, h, g, wg, wu, wd)


def _combine_kernel(p0_ref, p1_ref, h_ref, route_ref, y_ref, o_ref, ybuf, sem, *, tm):
    def row_copy(src_row, k, r):
        return pltpu.make_async_copy(y_ref.at[pl.ds(src_row, 1)], ybuf.at[k, pl.ds(r, 1)], sem)

    def issue(r, c):
        row_copy(p0_ref[0, 0, r], 0, r).start()
        row_copy(p1_ref[0, 0, r], 1, r).start()
        return c

    def drain(r, c):
        row_copy(0, 0, r).wait()
        row_copy(0, 1, r).wait()
        return c

    lax.fori_loop(0, tm, issue, 0, unroll=DMA_LOOP_UNROLL)
    lax.fori_loop(0, tm, drain, 0, unroll=DMA_LOOP_UNROLL)
    g0 = route_ref[:, TOP_K:TOP_K + 1]
    g1 = route_ref[:, TOP_K + 1:TOP_K + 2]
    o_ref[...] = h_ref[...] + (ybuf[0] * g0 + ybuf[1] * g1)


def _combine(p0, p1, h, route, y, *, tm):
    m, d = h.shape
    tm = min(tm, m)
    nt = m // tm
    idx = pl.BlockSpec((1, 1, tm), lambda i: (i, 0, 0), memory_space=pltpu.SMEM)
    row = pl.BlockSpec((tm, d), lambda i: (i, 0))
    return pl.pallas_call(
        functools.partial(_combine_kernel, tm=tm),
        grid=(nt,),
        in_specs=[idx, idx, row, pl.BlockSpec((tm, LANES), lambda i: (i, 0)), pl.BlockSpec(memory_space=pl.ANY)],
        out_specs=row,
        out_shape=jax.ShapeDtypeStruct((m, d), F32),
        scratch_shapes=[pltpu.VMEM((2, tm, d), F32), pltpu.SemaphoreType.DMA(())],
        compiler_params=_params("arbitrary"),
        name="moe_combine",
    )(p0.reshape(nt, 1, tm), p1.reshape(nt, 1, tm), h, route, y)


def _route_plan(route, *, n_experts, bm):
    n = route.shape[0]
    flat_e = route[:, :TOP_K].astype(jnp.int32).reshape(-1)
    onehot = (flat_e[:, None] == jnp.arange(n_experts, dtype=jnp.int32)[None, :]).astype(jnp.int32)
    csum = jnp.cumsum(onehot, axis=0)
    rank = jnp.take_along_axis(csum, flat_e[:, None], axis=1)[:, 0] - 1
    counts = csum[-1]
    blocks = (counts + bm - 1) // bm
    blk_end = jnp.cumsum(blocks)
    blk_start = blk_end - blocks
    rows = blk_start[flat_e] * bm + rank
    nblk = -(-(n * TOP_K) // bm) + n_experts
    block_expert = jnp.minimum(jnp.searchsorted(blk_end, jnp.arange(nblk), side="right"),
                               n_experts - 1).astype(jnp.int32)
    buf_tok = jnp.zeros((nblk * bm,), jnp.int32).at[rows].set(jnp.arange(n * TOP_K, dtype=jnp.int32) // TOP_K)
    n_real = blk_end[-1:].astype(jnp.int32)
    return block_expert, n_real, buf_tok, rows[0::TOP_K], rows[1::TOP_K]


def kernel(x, meta_tokens, attn_norm, w_qkv, q_gain, k_gain, rel_bias, meta_bias, w_o, conv_norm, w_in, conv_w,
           w_out, dense_norm, w_gate, w_up, w_down, moe_norm, w_router, moe_gate, moe_up, moe_down):
    batch, seq, d = x.shape
    n = batch * seq
    h = d // HEAD_DIM
    n_experts = w_router.shape[-1]
    assert d % HEAD_DIM == 0 and seq % GRID_W == 0 and seq // GRID_W >= WIN_ROWS
    assert meta_tokens.shape[0] == N_META and rel_bias.shape[1:] == (h, 2 * WIN_ROWS - 1, 2 * WIN_COLS - 1)
    assert all(p.shape[0] == 1 for p in (attn_norm, conv_norm, dense_norm, moe_norm)), "two-layer trunk only"

    xt = x.reshape(n, d)
    mt = jnp.broadcast_to(meta_tokens[None].astype(x.dtype), (batch, N_META, d)).reshape(batch * N_META, d)
    nm = batch * N_META

    g_attn = attn_norm[0].reshape(1, d)
    wqkv = w_qkv[0].astype(BF16)
    head_gain = jnp.concatenate([jnp.tile(q_gain[0] * (HEAD_DIM ** -0.5), h), jnp.tile(k_gain[0], h),
                                 jnp.ones((d,), F32)]).reshape(1, 3 * d)
    qkv = _qkv(xt, g_attn, wqkv, head_gain, bm=1024, bn=1024)
    qkv_m = _qkv(mt, g_attn, wqkv, head_gain, bm=nm, bn=1024)

    def meta_heads(part):
        t = qkv_m[:, part * d:(part + 1) * d].reshape(batch, N_META, h, HEAD_DIM).transpose(0, 2, 1, 3)
        return jnp.pad(t, ((0, 0), (0, 0), (0, LANES - N_META), (0, 0)))

    bias = _attention_bias(rel_bias[0], meta_bias[0])
    o = _attention(qkv, meta_heads(1), meta_heads(2), bias, batch=batch, seq=seq, rb=16)
    o_m = _meta_attention(qkv_m, meta_bias[0], batch=batch)
    wo = w_o[0].astype(BF16)
    h1 = _proj_residual(o, wo, xt, bm=512)
    h1_m = _proj_residual(o_m, wo, mt, bm=nm)

    g_dense = dense_norm[0].reshape(1, d)
    wg, wu, wd = w_gate[0].astype(BF16), w_up[0].astype(BF16), w_down[0].astype(BF16)
    h2 = _dense_ffn(h1, g_dense, wg, wu, wd, bm=512, bf=FFN_BLOCK_F)
    h2_m = _dense_ffn(h1_m, g_dense, wg, wu, wd, bm=nm, bf=FFN_BLOCK_F)

    g_conv = conv_norm[0].reshape(1, d)
    win = w_in[0].astype(BF16)
    bg, z = _conv_in_proj(h2, g_conv, win, bm=1024, bn=512)
    _, z_m = _conv_in_proj(h2_m, g_conv, win, bm=nm, bn=512)
    z_meta_last = z_m.reshape(batch, N_META, d)[:, N_META - 1:, :].astype(F32)
    h3 = _conv_proj_residual(bg, z, z_meta_last, conv_w[0].T, w_out[0].astype(BF16), h2, seq=seq, bm=512)

    g_moe = moe_norm[0].reshape(1, d)
    wr_pad = jnp.pad(w_router[0], ((0, 0), (0, LANES - n_experts)))
    route = _router(h3, g_moe, wr_pad, n_experts=n_experts, bm=512)
    bm_e = 512
    block_expert, n_real, buf_tok, p0, p1 = _route_plan(route, n_experts=n_experts, bm=bm_e)
    ybuf = _expert_ffn(block_expert, n_real, buf_tok, h3, g_moe, moe_gate[0].astype(BF16),
                       moe_up[0].astype(BF16), moe_down[0].astype(BF16), bm=bm_e, bf=FFN_BLOCK_F)
    out = _combine(p0, p1, h3, route, ybuf, tm=256)
    return out.reshape(batch, seq, d)
```

```python
import functools

import numpy as np
import jax
import jax.numpy as jnp
from jax import lax
from jax.experimental import pallas as pl
from jax.experimental.pallas import tpu as pltpu

HEAD_DIM = 128
GRID_W = 64
WIN_ROWS = 8
WIN_COLS = 16
N_META = 16
TOP_K = 2
RMS_EPS = 1e-6
LANES = 128
MXU_WIDTH = 256
F32_SUBLANES = 8
BF16_SUBLANES = 16
VMEM_LIMIT_BYTES = 56 * 1024 * 1024
MASK_VALUE = -1e30
N_LOC = WIN_ROWS * GRID_W
ATTN_ROW_GROUP = 16
DMA_LOOP_UNROLL = 8
FFN_BLOCK_F = 1024

F32 = jnp.float32
BF16 = jnp.bfloat16


def _params(*sem):
    return pltpu.CompilerParams(dimension_semantics=sem, vmem_limit_bytes=VMEM_LIMIT_BYTES)


def _rms(xf, g):
    return xf * lax.rsqrt(jnp.mean(xf * xf, axis=-1, keepdims=True) + RMS_EPS) * g


def _dot(a, b):
    return jnp.dot(a, b, preferred_element_type=F32)


def _dot_nt(a, b):
    return lax.dot_general(a, b, (((1,), (1,)), ((), ())), preferred_element_type=F32)


def _qkv_kernel(x_ref, g_ref, w_ref, hg_ref, o_ref, xn_ref, *, n_norm_blocks):
    j = pl.program_id(1)

    @pl.when(j == 0)
    def _():
        xn_ref[...] = _rms(x_ref[...], g_ref[...]).astype(BF16)

    normed = j < n_norm_blocks
    xn = xn_ref[...]
    for c in range(o_ref.shape[1] // MXU_WIDTH):
        acc = _dot(xn, w_ref[:, c * MXU_WIDTH:(c + 1) * MXU_WIDTH])
        for hh in range(MXU_WIDTH // HEAD_DIM):
            sl = slice(c * MXU_WIDTH + hh * HEAD_DIM, c * MXU_WIDTH + (hh + 1) * HEAD_DIM)
            t = acc[:, hh * HEAD_DIM:(hh + 1) * HEAD_DIM]
            o_ref[:, sl] = jnp.where(normed, _rms(t, hg_ref[:, sl]), t).astype(BF16)


def _qkv(x, g, w, hg, *, bm, bn):
    m, d = x.shape
    n = w.shape[1]
    bm, bn = min(bm, m), min(bn, d)
    return pl.pallas_call(
        functools.partial(_qkv_kernel, n_norm_blocks=2 * d // bn),
        grid=(m // bm, n // bn),
        in_specs=[
            pl.BlockSpec((bm, d), lambda i, j: (i, 0)),
            pl.BlockSpec((1, d), lambda i, j: (0, 0)),
            pl.BlockSpec((d, bn), lambda i, j: (0, j)),
            pl.BlockSpec((1, bn), lambda i, j: (0, j)),
        ],
        out_specs=pl.BlockSpec((bm, bn), lambda i, j: (i, j)),
        out_shape=jax.ShapeDtypeStruct((m, n), BF16),
        scratch_shapes=[pltpu.VMEM((bm, d), BF16)],
        compiler_params=_params("arbitrary", "arbitrary"),
        name="qkv_proj",
    )(x, g, w, hg)


def _attn_kernel(q_ref, k_ref, v_ref, km_ref, vm_ref, bias_ref, o_ref, *, rb, n_rows):
    rblk = pl.program_id(2)
    km = km_ref[0, 0]
    vm = vm_ref[0, 0]

    def group(gi, carry):
        scores, probs = [], []
        for u in range(ATTN_ROW_GROUP):
            rr = gi * ATTN_ROW_GROUP + u
            r = rblk * rb + rr
            r0 = jnp.clip(r - WIN_ROWS // 2, 0, n_rows - WIN_ROWS)
            start = pl.multiple_of(r0 * GRID_W, GRID_W)
            qs = pl.multiple_of(rr * GRID_W, GRID_W)
            q = q_ref[pl.ds(qs, GRID_W), :]
            kw = k_ref[pl.ds(start, N_LOC), :]
            s = jnp.concatenate([_dot_nt(q, kw), _dot_nt(q, km)], axis=-1) + bias_ref[r - r0, 0]
            scores.append((qs, start, s))
        for qs, start, s in scores:
            p = jnp.exp(s - jnp.max(s, axis=-1, keepdims=True))
            probs.append((qs, start, p.astype(BF16), jnp.sum(p, axis=-1, keepdims=True)))
        for qs, start, pb, l in probs:
            vw = v_ref[pl.ds(start, N_LOC), :]
            o = _dot(pb[:, :N_LOC], vw) + _dot(pb[:, N_LOC:], vm)
            o_ref[pl.ds(qs, GRID_W), :] = (o / l).astype(BF16)
        return carry

    lax.fori_loop(0, rb // ATTN_ROW_GROUP, group, 0)


def _attention(qkv, km, vm, bias, *, batch, seq, rb):
    n, d3 = qkv.shape
    d = d3 // 3
    h = d // HEAD_DIM
    n_rows = seq // GRID_W
    rb = min(rb, n_rows)
    nrb = n_rows // rb
    kv_spec = lambda off: pl.BlockSpec((seq, HEAD_DIM), lambda b, hh, r: (b, off + hh))
    meta_spec = pl.BlockSpec((1, 1, LANES, HEAD_DIM), lambda b, hh, r: (b, hh, 0, 0))
    return pl.pallas_call(
        functools.partial(_attn_kernel, rb=rb, n_rows=n_rows),
        grid=(batch, h, nrb),
        in_specs=[
            pl.BlockSpec((rb * GRID_W, HEAD_DIM), lambda b, hh, r: (b * nrb + r, hh)),
            kv_spec(h),
            kv_spec(2 * h),
            meta_spec,
            meta_spec,
            pl.BlockSpec((WIN_ROWS, 1, GRID_W, N_LOC + LANES), lambda b, hh, r: (0, hh, 0, 0)),
        ],
        out_specs=pl.BlockSpec((rb * GRID_W, HEAD_DIM), lambda b, hh, r: (b * nrb + r, hh)),
        out_shape=jax.ShapeDtypeStruct((n, d), BF16),
        compiler_params=_params("arbitrary", "arbitrary", "arbitrary"),
        name="nbr_attention",
    )(qkv, qkv, qkv, km, vm, bias)


def _meta_attn_kernel(q_ref, k_ref, v_ref, mb_ref, o_ref):
    s = _dot_nt(q_ref[...], k_ref[...]) + mb_ref[0]
    p = jnp.exp(s - jnp.max(s, axis=-1, keepdims=True))
    l = jnp.sum(p, axis=-1, keepdims=True)
    o_ref[...] = (_dot(p.astype(BF16), v_ref[...]) / l).astype(BF16)


def _meta_attention(qkv_m, meta_bias, *, batch):
    d = qkv_m.shape[1] // 3
    h = d // HEAD_DIM
    spec = lambda off: pl.BlockSpec((N_META, HEAD_DIM), lambda b, hh: (b, off + hh))
    return pl.pallas_call(
        _meta_attn_kernel,
        grid=(batch, h),
        in_specs=[spec(0), spec(h), spec(2 * h),
                  pl.BlockSpec((1, 1, N_META), lambda b, hh: (hh, 0, 0))],
        out_specs=spec(0),
        out_shape=jax.ShapeDtypeStruct((batch * N_META, d), BF16),
        compiler_params=_params("arbitrary", "arbitrary"),
        name="meta_attention",
    )(qkv_m, qkv_m, qkv_m, meta_bias.reshape(h, 1, N_META))


def _attention_bias(rel_bias, meta_bias):
    h = rel_bias.shape[0]
    cq = np.arange(GRID_W)
    c0 = np.clip(cq - WIN_COLS // 2, 0, GRID_W - WIN_COLS)
    ck = np.arange(GRID_W)
    in_win = (ck[None, :] >= c0[:, None]) & (ck[None, :] < c0[:, None] + WIN_COLS)
    dc = ck[None, :] - cq[:, None] + WIN_COLS - 1
    by_col = jnp.full((h, 2 * WIN_ROWS - 1, GRID_W, GRID_W), MASK_VALUE, F32)
    for j in range(2 * WIN_COLS - 1):
        by_col = jnp.where((in_win & (dc == j))[None, None], rel_bias[:, :, j, None, None], by_col)
    loc = jnp.stack([by_col[:, WIN_ROWS - 1 - off:2 * WIN_ROWS - 1 - off] for off in range(WIN_ROWS)])
    loc = loc.transpose(0, 1, 3, 2, 4).reshape(WIN_ROWS, h, GRID_W, N_LOC)
    met = jnp.concatenate([meta_bias, jnp.full((h, LANES - N_META), MASK_VALUE, F32)], axis=-1)
    met = jnp.broadcast_to(met[None, :, None, :], (WIN_ROWS, h, GRID_W, LANES))
    return jnp.concatenate([loc, met], axis=-1).astype(F32)


def _proj_kernel(a_ref, w_ref, r_ref, o_ref):
    o_ref[...] = r_ref[...] + _dot(a_ref[...], w_ref[...])


def _proj_residual(a, w, r, *, bm):
    m, d = r.shape
    bm = min(bm, m)
    row = pl.BlockSpec((bm, d), lambda i: (i, 0))
    return pl.pallas_call(
        _proj_kernel,
        grid=(m // bm,),
        in_specs=[row, pl.BlockSpec((d, d), lambda i: (0, 0)), row],
        out_specs=row,
        out_shape=jax.ShapeDtypeStruct((m, d), F32),
        compiler_params=_params("arbitrary"),
        name="attn_out_proj",
    )(a, w, r)


def _conv_proj_kernel(bg_ref, z_ref, zp_ref, zn_ref, zm_ref, cw_ref, w_ref, r_ref, o_ref, *, blocks_per_seq):
    i = pl.program_id(0)
    z = z_ref[...].astype(F32)
    bm = z.shape[0]
    first = (i % blocks_per_seq) == 0
    last = (i % blocks_per_seq) == blocks_per_seq - 1
    prev_row = jnp.where(first, zm_ref[0], zp_ref[BF16_SUBLANES - 1:BF16_SUBLANES, :].astype(F32))
    next_row = jnp.where(last, 0.0, zn_ref[0:1, :].astype(F32))
    row = lax.broadcasted_iota(jnp.int32, z.shape, 0)
    z_prev = jnp.where(row == 0, prev_row, pltpu.roll(z, 1, axis=0))
    z_next = jnp.where(row == bm - 1, next_row, pltpu.roll(z, bm - 1, axis=0))
    y = z_prev * cw_ref[0:1, :] + z * cw_ref[1:2, :] + z_next * cw_ref[2:3, :]
    a = (bg_ref[...].astype(F32) * y).astype(BF16)
    o_ref[...] = r_ref[...] + _dot(a, w_ref[...])


def _conv_proj_residual(bg, z, z_meta_last, conv_w_t, w, r, *, seq, bm):
    m, d = r.shape
    bm = min(bm, seq)
    bps = seq // bm
    halo = bm // BF16_SUBLANES
    n_halo = m // BF16_SUBLANES
    row = pl.BlockSpec((bm, d), lambda i: (i, 0))
    return pl.pallas_call(
        functools.partial(_conv_proj_kernel, blocks_per_seq=bps),
        grid=(m // bm,),
        in_specs=[
            row, row,
            pl.BlockSpec((BF16_SUBLANES, d), lambda i: (jnp.maximum(i * halo - 1, 0), 0)),
            pl.BlockSpec((BF16_SUBLANES, d), lambda i: (jnp.minimum((i + 1) * halo, n_halo - 1), 0)),
            pl.BlockSpec((1, 1, d), lambda i: (i // bps, 0, 0)),
            pl.BlockSpec((3, d), lambda i: (0, 0)),
            pl.BlockSpec((d, d), lambda i: (0, 0)),
            row,
        ],
        out_specs=row,
        out_shape=jax.ShapeDtypeStruct((m, d), F32),
        compiler_params=_params("arbitrary"),
        name="conv_out_proj",
    )(bg, z, z, z, z_meta_last, conv_w_t, w, r)


def _win_kernel(h_ref, g_ref, wb_ref, wc_ref, wu_ref, bg_ref, z_ref, xn_ref):
    @pl.when(pl.program_id(1) == 0)
    def _():
        xn_ref[...] = _rms(h_ref[...], g_ref[...]).astype(BF16)

    xn = xn_ref[...]
    bg_ref[...] = _dot(xn, wb_ref[...]).astype(BF16)
    z_ref[...] = (_dot(xn, wc_ref[...]) * _dot(xn, wu_ref[...])).astype(BF16)


def _conv_in_proj(h, g, w_in, *, bm, bn):
    m, d = h.shape
    bm, bn = min(bm, m), min(bn, d)
    nb = d // bn
    wspec = lambda part: pl.BlockSpec((d, bn), lambda i, j: (0, part * nb + j))
    out = pl.BlockSpec((bm, bn), lambda i, j: (i, j))
    return pl.pallas_call(
        _win_kernel,
        grid=(m // bm, nb),
        in_specs=[pl.BlockSpec((bm, d), lambda i, j: (i, 0)), pl.BlockSpec((1, d), lambda i, j: (0, 0)),
                  wspec(0), wspec(1), wspec(2)],
        out_specs=[out, out],
        out_shape=[jax.ShapeDtypeStruct((m, d), BF16), jax.ShapeDtypeStruct((m, d), BF16)],
        scratch_shapes=[pltpu.VMEM((bm, d), BF16)],
        compiler_params=_params("arbitrary", "arbitrary"),
        name="conv_in_proj",
    )(h, g, w_in, w_in, w_in)


def _ffn_kernel(h_ref, g_ref, wg_ref, wu_ref, wd_ref, o_ref, xn_ref):
    f = pl.program_id(1)

    @pl.when(f == 0)
    def _():
        xn_ref[...] = _rms(h_ref[...], g_ref[...]).astype(BF16)
        o_ref[...] = h_ref[...]

    xn = xn_ref[...]
    a = (jax.nn.silu(_dot(xn, wg_ref[...])) * _dot(xn, wu_ref[...])).astype(BF16)
    o_ref[...] += _dot(a, wd_ref[...])


def _dense_ffn(h, g, wg, wu, wd, *, bm, bf):
    m, d = h.shape
    ff = wg.shape[1]
    bm, bf = min(bm, m), min(bf, ff)
    row = pl.BlockSpec((bm, d), lambda i, f: (i, 0))
    return pl.pallas_call(
        _ffn_kernel,
        grid=(m // bm, ff // bf),
        in_specs=[row, pl.BlockSpec((1, d), lambda i, f: (0, 0)),
                  pl.BlockSpec((d, bf), lambda i, f: (0, f)),
                  pl.BlockSpec((d, bf), lambda i, f: (0, f)),
                  pl.BlockSpec((bf, d), lambda i, f: (f, 0))],
        out_specs=row,
        out_shape=jax.ShapeDtypeStruct((m, d), F32),
        scratch_shapes=[pltpu.VMEM((bm, d), BF16)],
        compiler_params=_params("arbitrary", "arbitrary"),
        name="dense_swiglu",
    )(h, g, wg, wu, wd)


def _router_kernel(h_ref, g_ref, wr_ref, o_ref, *, n_experts):
    xn = _rms(h_ref[...], g_ref[...])
    logits = jnp.dot(xn, wr_ref[...], preferred_element_type=F32, precision=lax.Precision.HIGHEST)
    lane = lax.broadcasted_iota(jnp.int32, logits.shape, 1)
    neg = -jnp.inf
    l1 = jnp.where(lane < n_experts, logits, neg)
    m1 = jnp.max(l1, axis=-1, keepdims=True)
    i1 = jnp.min(jnp.where(l1 == m1, lane, LANES), axis=-1, keepdims=True)
    l2 = jnp.where(lane == i1, neg, l1)
    m2 = jnp.max(l2, axis=-1, keepdims=True)
    i2 = jnp.min(jnp.where(l2 == m2, lane, LANES), axis=-1, keepdims=True)
    t = jnp.exp(m2 - m1)
    g1 = 1.0 / (1.0 + t)
    g2 = t / (1.0 + t)
    o_ref[...] = jnp.where(lane == 0, i1.astype(F32),
                           jnp.where(lane == 1, i2.astype(F32),
                                     jnp.where(lane == 2, g1, jnp.where(lane == 3, g2, 0.0))))


def _router(h, g, wr_pad, *, n_experts, bm):
    m, d = h.shape
    bm = min(bm, m)
    return pl.pallas_call(
        functools.partial(_router_kernel, n_experts=n_experts),
        grid=(m // bm,),
        in_specs=[pl.BlockSpec((bm, d), lambda i: (i, 0)), pl.BlockSpec((1, d), lambda i: (0, 0)),
                  pl.BlockSpec((d, LANES), lambda i: (0, 0))],
        out_specs=pl.BlockSpec((bm, LANES), lambda i: (i, 0)),
        out_shape=jax.ShapeDtypeStruct((m, LANES), F32),
        compiler_params=_params("arbitrary"),
        name="moe_router",
    )(h, g, wr_pad)


def _expert_kernel(be_ref, nreal_ref, tok_ref, tok_next_ref, h_ref, g_ref, wg_ref, wu_ref, wd_ref, o_ref,
                   xg_ref, xn_ref, sem, *, rows_per_step):
    i = pl.program_id(0)
    f = pl.program_id(1)
    nblk, nf = pl.num_programs(0), pl.num_programs(1)
    n_real = nreal_ref[0]
    bm = xn_ref.shape[0]
    n_gather = tok_ref.shape[2]

    def row_copy(tok, r):
        return pltpu.make_async_copy(h_ref.at[pl.ds(tok, 1)], xg_ref.at[pl.ds(r, 1)], sem)

    def wait_gather():
        def body(r, c):
            row_copy(0, r).wait()
            return c
        lax.fori_loop(0, n_gather, body, 0, unroll=DMA_LOOP_UNROLL)

    @pl.when(jnp.logical_and(i == 0, f == 0))
    def _():
        def body(r, c):
            row_copy(tok_ref[0, 0, r], r).start()
            return c
        lax.fori_loop(0, n_gather, body, 0, unroll=DMA_LOOP_UNROLL)

    @pl.when(jnp.logical_and(f == 0, i <= n_real))
    def _():
        wait_gather()

    @pl.when(i < n_real)
    def _():
        @pl.when(f == 0)
        def _():
            xn_ref[...] = _rms(xg_ref[:bm, :], g_ref[...]).astype(BF16)
            o_ref[...] = jnp.zeros_like(o_ref)

        for u in range(rows_per_step):
            r = f * rows_per_step + u
            row_copy(tok_next_ref[0, 0, r], r).start()

        xn = xn_ref[...]
        a = (jax.nn.silu(_dot(xn, wg_ref[0])) * _dot(xn, wu_ref[0])).astype(BF16)
        o_ref[...] += _dot(a, wd_ref[0])

        @pl.when(jnp.logical_and(i == nblk - 1, f == nf - 1))
        def _():
            wait_gather()

    @pl.when(jnp.logical_and(i >= n_real, f == nf - 1))
    def _():
        o_ref[...] = jnp.zeros_like(o_ref)


def _expert_ffn(block_expert, n_real, buf_tok, h, g, wg, wu, wd, *, bm, bf):
    d = h.shape[1]
    ff = wg.shape[2]
    bf = min(bf, ff)
    nblk, nf = buf_tok.shape[0] // bm, ff // bf
    rows_per_step = pl.cdiv(bm, nf)
    n_gather = rows_per_step * nf
    tok = jnp.pad(buf_tok.reshape(nblk, 1, bm), ((0, 0), (0, 0), (0, n_gather - bm)))

    def blk(i, nr):
        return jnp.minimum(i, nr[0] - 1)

    def fblk(i, f, nr):
        return jnp.where(i < nr[0], f, nf - 1)

    return pl.pallas_call(
        functools.partial(_expert_kernel, rows_per_step=rows_per_step),
        grid_spec=pltpu.PrefetchScalarGridSpec(
            num_scalar_prefetch=2,
            grid=(nblk, nf),
            in_specs=[
                pl.BlockSpec((1, 1, n_gather), lambda i, f, be, nr: (i, 0, 0), memory_space=pltpu.SMEM),
                pl.BlockSpec((1, 1, n_gather), lambda i, f, be, nr: (jnp.minimum(i + 1, nblk - 1), 0, 0),
                             memory_space=pltpu.SMEM),
                pl.BlockSpec(memory_space=pl.ANY),
                pl.BlockSpec((1, d), lambda i, f, be, nr: (0, 0)),
                pl.BlockSpec((1, d, bf), lambda i, f, be, nr: (be[blk(i, nr)], 0, fblk(i, f, nr))),
                pl.BlockSpec((1, d, bf), lambda i, f, be, nr: (be[blk(i, nr)], 0, fblk(i, f, nr))),
                pl.BlockSpec((1, bf, d), lambda i, f, be, nr: (be[blk(i, nr)], fblk(i, f, nr), 0)),
            ],
            out_specs=pl.BlockSpec((bm, d), lambda i, f, be, nr: (i, 0)),
            scratch_shapes=[pltpu.VMEM((pl.cdiv(n_gather, F32_SUBLANES) * F32_SUBLANES, d), F32), pltpu.VMEM((bm, d), BF16),
                            pltpu.SemaphoreType.DMA(())],
        ),
        out_shape=jax.ShapeDtypeStruct((nblk * bm, d), F32),
        compiler_params=_params("arbitrary", "arbitrary"),
        name="moe_experts",
    )(block_expert, n_real, tok, tok, h, g, wg, wu, wd)


def _combine_kernel(p0_ref, p1_ref, p0_next_ref, p1_next_ref, h_ref, route_ref, y_ref, o_ref, ybuf, sem, *, tm):
    i = pl.program_id(0)
    slot = i % 2

    def row_copy(src_row, s, k, r):
        return pltpu.make_async_copy(y_ref.at[pl.ds(src_row, 1)], ybuf.at[s, k, pl.ds(r, 1)], sem.at[s])

    def start_gather(a_ref, b_ref, s):
        def body(r, c):
            row_copy(a_ref[0, 0, r], s, 0, r).start()
            row_copy(b_ref[0, 0, r], s, 1, r).start()
            return c
        lax.fori_loop(0, tm, body, 0, unroll=DMA_LOOP_UNROLL)

    def wait_gather(s):
        def body(r, c):
            row_copy(0, s, 0, r).wait()
            row_copy(0, s, 1, r).wait()
            return c
        lax.fori_loop(0, tm, body, 0, unroll=DMA_LOOP_UNROLL)

    @pl.when(i == 0)
    def _():
        start_gather(p0_ref, p1_ref, 0)

    @pl.when(i + 1 < pl.num_programs(0))
    def _():
        start_gather(p0_next_ref, p1_next_ref, 1 - slot)

    wait_gather(slot)
    g0 = route_ref[:, TOP_K:TOP_K + 1]
    g1 = route_ref[:, TOP_K + 1:TOP_K + 2]
    o_ref[...] = h_ref[...] + (ybuf[slot, 0] * g0 + ybuf[slot, 1] * g1)


def _combine(p0, p1, h, route, y, *, tm):
    m, d = h.shape
    tm = min(tm, m)
    nt = m // tm
    idx = pl.BlockSpec((1, 1, tm), lambda i: (i, 0, 0), memory_space=pltpu.SMEM)
    idx_next = pl.BlockSpec((1, 1, tm), lambda i: (jnp.minimum(i + 1, nt - 1), 0, 0), memory_space=pltpu.SMEM)
    row = pl.BlockSpec((tm, d), lambda i: (i, 0))
    p0, p1 = p0.reshape(nt, 1, tm), p1.reshape(nt, 1, tm)
    return pl.pallas_call(
        functools.partial(_combine_kernel, tm=tm),
        grid=(nt,),
        in_specs=[idx, idx, idx_next, idx_next, row, pl.BlockSpec((tm, LANES), lambda i: (i, 0)),
                  pl.BlockSpec(memory_space=pl.ANY)],
        out_specs=row,
        out_shape=jax.ShapeDtypeStruct((m, d), F32),
        scratch_shapes=[pltpu.VMEM((2, 2, tm, d), F32), pltpu.SemaphoreType.DMA((2,))],
        compiler_params=_params("arbitrary"),
        name="moe_combine",
    )(p0, p1, p0, p1, h, route, y)


def _route_plan(route, *, n_experts, bm):
    n = route.shape[0]
    flat_e = route[:, :TOP_K].astype(jnp.int32).reshape(-1)
    onehot = (flat_e[:, None] == jnp.arange(n_experts, dtype=jnp.int32)[None, :]).astype(jnp.int32)
    csum = jnp.cumsum(onehot, axis=0)
    rank = jnp.take_along_axis(csum, flat_e[:, None], axis=1)[:, 0] - 1
    counts = csum[-1]
    blocks = (counts + bm - 1) // bm
    blk_end = jnp.cumsum(blocks)
    blk_start = blk_end - blocks
    rows = blk_start[flat_e] * bm + rank
    nblk = -(-(n * TOP_K) // bm) + n_experts
    block_expert = jnp.minimum(jnp.searchsorted(blk_end, jnp.arange(nblk), side="right"),
                               n_experts - 1).astype(jnp.int32)
    buf_tok = jnp.zeros((nblk * bm,), jnp.int32).at[rows].set(jnp.arange(n * TOP_K, dtype=jnp.int32) // TOP_K)
    n_real = blk_end[-1:].astype(jnp.int32)
    return block_expert, n_real, buf_tok, rows[0::TOP_K], rows[1::TOP_K]


def kernel(x, meta_tokens, attn_norm, w_qkv, q_gain, k_gain, rel_bias, meta_bias, w_o, conv_norm, w_in, conv_w,
           w_out, dense_norm, w_gate, w_up, w_down, moe_norm, w_router, moe_gate, moe_up, moe_down):
    batch, seq, d = x.shape
    n = batch * seq
    h = d // HEAD_DIM
    n_experts = w_router.shape[-1]
    assert d % HEAD_DIM == 0 and seq % GRID_W == 0 and seq // GRID_W >= WIN_ROWS
    assert meta_tokens.shape[0] == N_META and rel_bias.shape[1:] == (h, 2 * WIN_ROWS - 1, 2 * WIN_COLS - 1)
    assert all(p.shape[0] == 1 for p in (attn_norm, conv_norm, dense_norm, moe_norm)), "two-layer trunk only"

    xt = x.reshape(n, d)
    mt = jnp.broadcast_to(meta_tokens[None].astype(x.dtype), (batch, N_META, d)).reshape(batch * N_META, d)
    nm = batch * N_META

    g_attn = attn_norm[0].reshape(1, d)
    wqkv = w_qkv[0].astype(BF16)
    head_gain = jnp.concatenate([jnp.tile(q_gain[0] * (HEAD_DIM ** -0.5), h), jnp.tile(k_gain[0], h),
                                 jnp.ones((d,), F32)]).reshape(1, 3 * d)
    qkv = _qkv(xt, g_attn, wqkv, head_gain, bm=1024, bn=1024)
    qkv_m = _qkv(mt, g_attn, wqkv, head_gain, bm=nm, bn=1024)

    def meta_heads(part):
        t = qkv_m[:, part * d:(part + 1) * d].reshape(batch, N_META, h, HEAD_DIM).transpose(0, 2, 1, 3)
        return jnp.pad(t, ((0, 0), (0, 0), (0, LANES - N_META), (0, 0)))

    bias = _attention_bias(rel_bias[0], meta_bias[0])
    o = _attention(qkv, meta_heads(1), meta_heads(2), bias, batch=batch, seq=seq, rb=16)
    o_m = _meta_attention(qkv_m, meta_bias[0], batch=batch)
    wo = w_o[0].astype(BF16)
    h1 = _proj_residual(o, wo, xt, bm=512)
    h1_m = _proj_residual(o_m, wo, mt, bm=nm)

    g_dense = dense_norm[0].reshape(1, d)
    wg, wu, wd = w_gate[0].astype(BF16), w_up[0].astype(BF16), w_down[0].astype(BF16)
    h2 = _dense_ffn(h1, g_dense, wg, wu, wd, bm=512, bf=FFN_BLOCK_F)
    h2_m = _dense_ffn(h1_m, g_dense, wg, wu, wd, bm=nm, bf=FFN_BLOCK_F)

    g_conv = conv_norm[0].reshape(1, d)
    win = w_in[0].astype(BF16)
    bg, z = _conv_in_proj(h2, g_conv, win, bm=1024, bn=512)
    _, z_m = _conv_in_proj(h2_m, g_conv, win, bm=nm, bn=512)
    z_meta_last = z_m.reshape(batch, N_META, d)[:, N_META - 1:, :].astype(F32)
    h3 = _conv_proj_residual(bg, z, z_meta_last, conv_w[0].T, w_out[0].astype(BF16), h2, seq=seq, bm=512)

    g_moe = moe_norm[0].reshape(1, d)
    wr_pad = jnp.pad(w_router[0], ((0, 0), (0, LANES - n_experts)))
    route = _router(h3, g_moe, wr_pad, n_experts=n_experts, bm=512)
    bm_e = 512
    block_expert, n_real, buf_tok, p0, p1 = _route_plan(route, n_experts=n_experts, bm=bm_e)
    ybuf = _expert_ffn(block_expert, n_real, buf_tok, h3, g_moe, moe_gate[0].astype(BF16),
                       moe_up[0].astype(BF16), moe_down[0].astype(BF16), bm=bm_e, bf=FFN_BLOCK_F)
    out = _combine(p0, p1, h3, route, ybuf, tm=256)
    return out.reshape(batch, seq, d)
```

```python
import functools

import numpy as np
import jax
import jax.numpy as jnp
from jax import lax
from jax.experimental import pallas as pl
from jax.experimental.pallas import tpu as pltpu

HEAD_DIM = 128
GRID_W = 64
WIN_ROWS = 8
WIN_COLS = 16
N_META = 16
TOP_K = 2
RMS_EPS = 1e-6
LANES = 128
MXU_WIDTH = 256
F32_SUBLANES = 8
BF16_SUBLANES = 16
VMEM_LIMIT_BYTES = 56 * 1024 * 1024
MASK_VALUE = -1e30
N_LOC = WIN_ROWS * GRID_W
ATTN_ROW_GROUP = 16
DMA_LOOP_UNROLL = 8
FFN_BLOCK_F = 1024

F32 = jnp.float32
BF16 = jnp.bfloat16


def _params(*sem):
    return pltpu.CompilerParams(dimension_semantics=sem, vmem_limit_bytes=VMEM_LIMIT_BYTES)


def _rms(xf, g):
    return xf * lax.rsqrt(jnp.mean(xf * xf, axis=-1, keepdims=True) + RMS_EPS) * g


def _dot(a, b):
    return jnp.dot(a, b, preferred_element_type=F32)


def _dot_nt(a, b):
    return lax.dot_general(a, b, (((1,), (1,)), ((), ())), preferred_element_type=F32)


def _with_cast_rider(body, n_in, n_out):
    def wrapped(*refs):
        rider_src, rider_dst = refs[n_in], refs[n_in + 1 + n_out]
        rider_dst[...] = rider_src[...].astype(rider_dst.dtype)
        body(*refs[:n_in], *refs[n_in + 1:n_in + 1 + n_out], *refs[n_in + 2 + n_out:])
    return wrapped


def _hosted_call(body, *, rider, rider_blocks, rider_index, in_specs, out_specs, out_shape, args, **kwargs):
    if rider is not None:
        rows, cols = rider.shape
        nr, nc = rider_blocks
        assert rows % (nr * BF16_SUBLANES) == 0 and cols % (nc * LANES) == 0, (rider.shape, rider_blocks)
        spec = pl.BlockSpec((rows // nr, cols // nc), rider_index)
        body = _with_cast_rider(body, len(in_specs), len(out_specs))
        in_specs, out_specs = in_specs + [spec], out_specs + [spec]
        out_shape = out_shape + [jax.ShapeDtypeStruct(rider.shape, BF16)]
        args = args + [rider]
    return pl.pallas_call(body, in_specs=in_specs, out_specs=out_specs, out_shape=out_shape, **kwargs)(*args)


def _qkv_kernel(x_ref, g_ref, w_ref, hg_ref, o_ref, xn_ref, *, n_norm_blocks):
    j = pl.program_id(1)

    @pl.when(j == 0)
    def _():
        xn_ref[...] = _rms(x_ref[...], g_ref[...]).astype(BF16)

    normed = j < n_norm_blocks
    xn = xn_ref[...]
    for c in range(o_ref.shape[1] // MXU_WIDTH):
        acc = _dot(xn, w_ref[:, c * MXU_WIDTH:(c + 1) * MXU_WIDTH])
        for hh in range(MXU_WIDTH // HEAD_DIM):
            sl = slice(c * MXU_WIDTH + hh * HEAD_DIM, c * MXU_WIDTH + (hh + 1) * HEAD_DIM)
            t = acc[:, hh * HEAD_DIM:(hh + 1) * HEAD_DIM]
            o_ref[:, sl] = jnp.where(normed, _rms(t, hg_ref[:, sl]), t).astype(BF16)


def _qkv(x, g, w, hg, *, bm, bn):
    m, d = x.shape
    n = w.shape[1]
    bm, bn = min(bm, m), min(bn, d)
    return pl.pallas_call(
        functools.partial(_qkv_kernel, n_norm_blocks=2 * d // bn),
        grid=(m // bm, n // bn),
        in_specs=[
            pl.BlockSpec((bm, d), lambda i, j: (i, 0)),
            pl.BlockSpec((1, d), lambda i, j: (0, 0)),
            pl.BlockSpec((d, bn), lambda i, j: (0, j)),
            pl.BlockSpec((1, bn), lambda i, j: (0, j)),
        ],
        out_specs=pl.BlockSpec((bm, bn), lambda i, j: (i, j)),
        out_shape=jax.ShapeDtypeStruct((m, n), BF16),
        scratch_shapes=[pltpu.VMEM((bm, d), BF16)],
        compiler_params=_params("arbitrary", "arbitrary"),
        name="qkv_proj",
    )(x, g, w, hg)


def _attn_kernel(q_ref, k_ref, v_ref, km_ref, vm_ref, bias_ref, o_ref, *, rb, n_rows):
    rblk = pl.program_id(2)
    km = km_ref[0, 0]
    vm = vm_ref[0, 0]

    def group(gi, carry):
        scores, probs = [], []
        for u in range(ATTN_ROW_GROUP):
            rr = gi * ATTN_ROW_GROUP + u
            r = rblk * rb + rr
            r0 = jnp.clip(r - WIN_ROWS // 2, 0, n_rows - WIN_ROWS)
            start = pl.multiple_of(r0 * GRID_W, GRID_W)
            qs = pl.multiple_of(rr * GRID_W, GRID_W)
            q = q_ref[pl.ds(qs, GRID_W), :]
            kw = k_ref[pl.ds(start, N_LOC), :]
            s = jnp.concatenate([_dot_nt(q, kw), _dot_nt(q, km)], axis=-1) + bias_ref[r - r0, 0]
            scores.append((qs, start, s))
        for qs, start, s in scores:
            p = jnp.exp(s - jnp.max(s, axis=-1, keepdims=True))
            probs.append((qs, start, p.astype(BF16), jnp.sum(p, axis=-1, keepdims=True)))
        for qs, start, pb, l in probs:
            vw = v_ref[pl.ds(start, N_LOC), :]
            o = _dot(pb[:, :N_LOC], vw) + _dot(pb[:, N_LOC:], vm)
            o_ref[pl.ds(qs, GRID_W), :] = (o / l).astype(BF16)
        return carry

    lax.fori_loop(0, rb // ATTN_ROW_GROUP, group, 0)


def _attention(qkv, km, vm, bias, *, batch, seq, rb, rider=None):
    n, d3 = qkv.shape
    d = d3 // 3
    h = d // HEAD_DIM
    n_rows = seq // GRID_W
    rb = min(rb, n_rows)
    nrb = n_rows // rb
    kv_spec = lambda off: pl.BlockSpec((seq, HEAD_DIM), lambda b, hh, r: (b, off + hh))
    meta_spec = pl.BlockSpec((1, 1, LANES, HEAD_DIM), lambda b, hh, r: (b, hh, 0, 0))
    return _hosted_call(
        functools.partial(_attn_kernel, rb=rb, n_rows=n_rows),
        rider=rider, rider_blocks=(batch * h * nrb, 1), rider_index=lambda b, hh, r: ((b * h + hh) * nrb + r, 0),
        grid=(batch, h, nrb),
        in_specs=[
            pl.BlockSpec((rb * GRID_W, HEAD_DIM), lambda b, hh, r: (b * nrb + r, hh)),
            kv_spec(h),
            kv_spec(2 * h),
            meta_spec,
            meta_spec,
            pl.BlockSpec((WIN_ROWS, 1, GRID_W, N_LOC + LANES), lambda b, hh, r: (0, hh, 0, 0)),
        ],
        out_specs=[pl.BlockSpec((rb * GRID_W, HEAD_DIM), lambda b, hh, r: (b * nrb + r, hh))],
        out_shape=[jax.ShapeDtypeStruct((n, d), BF16)],
        args=[qkv, qkv, qkv, km, vm, bias],
        compiler_params=_params("arbitrary", "arbitrary", "arbitrary"),
        name="nbr_attention",
    )


def _meta_attn_kernel(q_ref, k_ref, v_ref, mb_ref, o_ref):
    s = _dot_nt(q_ref[...], k_ref[...]) + mb_ref[0]
    p = jnp.exp(s - jnp.max(s, axis=-1, keepdims=True))
    l = jnp.sum(p, axis=-1, keepdims=True)
    o_ref[...] = (_dot(p.astype(BF16), v_ref[...]) / l).astype(BF16)


def _meta_attention(qkv_m, meta_bias, *, batch):
    d = qkv_m.shape[1] // 3
    h = d // HEAD_DIM
    spec = lambda off: pl.BlockSpec((N_META, HEAD_DIM), lambda b, hh: (b, off + hh))
    return pl.pallas_call(
        _meta_attn_kernel,
        grid=(batch, h),
        in_specs=[spec(0), spec(h), spec(2 * h),
                  pl.BlockSpec((1, 1, N_META), lambda b, hh: (hh, 0, 0))],
        out_specs=spec(0),
        out_shape=jax.ShapeDtypeStruct((batch * N_META, d), BF16),
        compiler_params=_params("arbitrary", "arbitrary"),
        name="meta_attention",
    )(qkv_m, qkv_m, qkv_m, meta_bias.reshape(h, 1, N_META))


def _attention_bias(rel_bias, meta_bias):
    h = rel_bias.shape[0]
    cq = np.arange(GRID_W)
    c0 = np.clip(cq - WIN_COLS // 2, 0, GRID_W - WIN_COLS)
    ck = np.arange(GRID_W)
    in_win = (ck[None, :] >= c0[:, None]) & (ck[None, :] < c0[:, None] + WIN_COLS)
    dc = ck[None, :] - cq[:, None] + WIN_COLS - 1
    by_col = jnp.full((h, 2 * WIN_ROWS - 1, GRID_W, GRID_W), MASK_VALUE, F32)
    for j in range(2 * WIN_COLS - 1):
        by_col = jnp.where((in_win & (dc == j))[None, None], rel_bias[:, :, j, None, None], by_col)
    loc = jnp.stack([by_col[:, WIN_ROWS - 1 - off:2 * WIN_ROWS - 1 - off] for off in range(WIN_ROWS)])
    loc = loc.transpose(0, 1, 3, 2, 4).reshape(WIN_ROWS, h, GRID_W, N_LOC)
    met = jnp.concatenate([meta_bias, jnp.full((h, LANES - N_META), MASK_VALUE, F32)], axis=-1)
    met = jnp.broadcast_to(met[None, :, None, :], (WIN_ROWS, h, GRID_W, LANES))
    return jnp.concatenate([loc, met], axis=-1).astype(F32)


def _proj_kernel(a_ref, w_ref, r_ref, o_ref):
    o_ref[...] = r_ref[...] + _dot(a_ref[...], w_ref[...])


def _proj_residual(a, w, r, *, bm):
    m, d = r.shape
    bm = min(bm, m)
    row = pl.BlockSpec((bm, d), lambda i: (i, 0))
    return pl.pallas_call(
        _proj_kernel,
        grid=(m // bm,),
        in_specs=[row, pl.BlockSpec((d, d), lambda i: (0, 0)), row],
        out_specs=row,
        out_shape=jax.ShapeDtypeStruct((m, d), F32),
        compiler_params=_params("arbitrary"),
        name="attn_out_proj",
    )(a, w, r)


def _conv_proj_kernel(bg_ref, z_ref, zp_ref, zn_ref, zm_ref, cw_ref, w_ref, r_ref, o_ref, *, blocks_per_seq):
    i = pl.program_id(0)
    z = z_ref[...].astype(F32)
    bm = z.shape[0]
    first = (i % blocks_per_seq) == 0
    last = (i % blocks_per_seq) == blocks_per_seq - 1
    prev_row = jnp.where(first, zm_ref[0], zp_ref[BF16_SUBLANES - 1:BF16_SUBLANES, :].astype(F32))
    next_row = jnp.where(last, 0.0, zn_ref[0:1, :].astype(F32))
    row = lax.broadcasted_iota(jnp.int32, z.shape, 0)
    z_prev = jnp.where(row == 0, prev_row, pltpu.roll(z, 1, axis=0))
    z_next = jnp.where(row == bm - 1, next_row, pltpu.roll(z, bm - 1, axis=0))
    y = z_prev * cw_ref[0:1, :] + z * cw_ref[1:2, :] + z_next * cw_ref[2:3, :]
    a = (bg_ref[...].astype(F32) * y).astype(BF16)
    o_ref[...] = r_ref[...] + _dot(a, w_ref[...])


def _conv_proj_residual(bg, z, z_meta_last, conv_w_t, w, r, *, seq, bm):
    m, d = r.shape
    bm = min(bm, seq)
    bps = seq // bm
    halo = bm // BF16_SUBLANES
    n_halo = m // BF16_SUBLANES
    row = pl.BlockSpec((bm, d), lambda i: (i, 0))
    return pl.pallas_call(
        functools.partial(_conv_proj_kernel, blocks_per_seq=bps),
        grid=(m // bm,),
        in_specs=[
            row, row,
            pl.BlockSpec((BF16_SUBLANES, d), lambda i: (jnp.maximum(i * halo - 1, 0), 0)),
            pl.BlockSpec((BF16_SUBLANES, d), lambda i: (jnp.minimum((i + 1) * halo, n_halo - 1), 0)),
            pl.BlockSpec((1, 1, d), lambda i: (i // bps, 0, 0)),
            pl.BlockSpec((3, d), lambda i: (0, 0)),
            pl.BlockSpec((d, d), lambda i: (0, 0)),
            row,
        ],
        out_specs=row,
        out_shape=jax.ShapeDtypeStruct((m, d), F32),
        compiler_params=_params("arbitrary"),
        name="conv_out_proj",
    )(bg, z, z, z, z_meta_last, conv_w_t, w, r)


def _win_kernel(h_ref, g_ref, wb_ref, wc_ref, wu_ref, bg_ref, z_ref, xn_ref):
    @pl.when(pl.program_id(1) == 0)
    def _():
        xn_ref[...] = _rms(h_ref[...], g_ref[...]).astype(BF16)

    xn = xn_ref[...]
    bg_ref[...] = _dot(xn, wb_ref[...]).astype(BF16)
    z_ref[...] = (_dot(xn, wc_ref[...]) * _dot(xn, wu_ref[...])).astype(BF16)


def _conv_in_proj(h, g, w_in, *, bm, bn, rider=None):
    m, d = h.shape
    bm, bn = min(bm, m), min(bn, d)
    nb = d // bn
    wspec = lambda part: pl.BlockSpec((d, bn), lambda i, j: (0, part * nb + j))
    out = pl.BlockSpec((bm, bn), lambda i, j: (i, j))
    return _hosted_call(
        _win_kernel,
        rider=rider, rider_blocks=(m // bm, nb), rider_index=lambda i, j: (i, j),
        grid=(m // bm, nb),
        in_specs=[pl.BlockSpec((bm, d), lambda i, j: (i, 0)), pl.BlockSpec((1, d), lambda i, j: (0, 0)),
                  wspec(0), wspec(1), wspec(2)],
        out_specs=[out, out],
        out_shape=[jax.ShapeDtypeStruct((m, d), BF16), jax.ShapeDtypeStruct((m, d), BF16)],
        args=[h, g, w_in, w_in, w_in],
        scratch_shapes=[pltpu.VMEM((bm, d), BF16)],
        compiler_params=_params("arbitrary", "arbitrary"),
        name="conv_in_proj",
    )


def _ffn_kernel(h_ref, g_ref, wg_ref, wu_ref, wd_ref, o_ref, xn_ref):
    f = pl.program_id(1)

    @pl.when(f == 0)
    def _():
        xn_ref[...] = _rms(h_ref[...], g_ref[...]).astype(BF16)
        o_ref[...] = h_ref[...]

    xn = xn_ref[...]
    a = (jax.nn.silu(_dot(xn, wg_ref[...])) * _dot(xn, wu_ref[...])).astype(BF16)
    o_ref[...] += _dot(a, wd_ref[...])


def _dense_ffn(h, g, wg, wu, wd, *, bm, bf, rider=None):
    m, d = h.shape
    ff = wg.shape[1]
    bm, bf = min(bm, m), min(bf, ff)
    row = pl.BlockSpec((bm, d), lambda i, f: (i, 0))
    return _hosted_call(
        _ffn_kernel,
        rider=rider, rider_blocks=(m // bm, ff // bf), rider_index=lambda i, f: (i, f),
        grid=(m // bm, ff // bf),
        in_specs=[row, pl.BlockSpec((1, d), lambda i, f: (0, 0)),
                  pl.BlockSpec((d, bf), lambda i, f: (0, f)),
                  pl.BlockSpec((d, bf), lambda i, f: (0, f)),
                  pl.BlockSpec((bf, d), lambda i, f: (f, 0))],
        out_specs=[row],
        out_shape=[jax.ShapeDtypeStruct((m, d), F32)],
        args=[h, g, wg, wu, wd],
        scratch_shapes=[pltpu.VMEM((bm, d), BF16)],
        compiler_params=_params("arbitrary", "arbitrary"),
        name="dense_swiglu",
    )


def _router_kernel(h_ref, g_ref, wr_ref, o_ref, *, n_experts):
    xn = _rms(h_ref[...], g_ref[...])
    logits = jnp.dot(xn, wr_ref[...], preferred_element_type=F32, precision=lax.Precision.HIGHEST)
    lane = lax.broadcasted_iota(jnp.int32, logits.shape, 1)
    neg = -jnp.inf
    l1 = jnp.where(lane < n_experts, logits, neg)
    m1 = jnp.max(l1, axis=-1, keepdims=True)
    i1 = jnp.min(jnp.where(l1 == m1, lane, LANES), axis=-1, keepdims=True)
    l2 = jnp.where(lane == i1, neg, l1)
    m2 = jnp.max(l2, axis=-1, keepdims=True)
    i2 = jnp.min(jnp.where(l2 == m2, lane, LANES), axis=-1, keepdims=True)
    t = jnp.exp(m2 - m1)
    g1 = 1.0 / (1.0 + t)
    g2 = t / (1.0 + t)
    o_ref[...] = jnp.where(lane == 0, i1.astype(F32),
                           jnp.where(lane == 1, i2.astype(F32),
                                     jnp.where(lane == 2, g1, jnp.where(lane == 3, g2, 0.0))))


def _router(h, g, wr_pad, *, n_experts, bm):
    m, d = h.shape
    bm = min(bm, m)
    return pl.pallas_call(
        functools.partial(_router_kernel, n_experts=n_experts),
        grid=(m // bm,),
        in_specs=[pl.BlockSpec((bm, d), lambda i: (i, 0)), pl.BlockSpec((1, d), lambda i: (0, 0)),
                  pl.BlockSpec((d, LANES), lambda i: (0, 0))],
        out_specs=pl.BlockSpec((bm, LANES), lambda i: (i, 0)),
        out_shape=jax.ShapeDtypeStruct((m, LANES), F32),
        compiler_params=_params("arbitrary"),
        name="moe_router",
    )(h, g, wr_pad)


def _expert_kernel(be_ref, nreal_ref, tok_ref, tok_next_ref, h_ref, g_ref, wg_ref, wu_ref, wd_ref, o_ref,
                   xg_ref, xn_ref, sem, *, rows_per_step):
    i = pl.program_id(0)
    f = pl.program_id(1)
    nblk, nf = pl.num_programs(0), pl.num_programs(1)
    n_real = nreal_ref[0]
    bm = xn_ref.shape[0]
    n_gather = tok_ref.shape[2]

    def row_copy(tok, r):
        return pltpu.make_async_copy(h_ref.at[pl.ds(tok, 1)], xg_ref.at[pl.ds(r, 1)], sem)

    def wait_gather():
        def body(r, c):
            row_copy(0, r).wait()
            return c
        lax.fori_loop(0, n_gather, body, 0, unroll=DMA_LOOP_UNROLL)

    @pl.when(jnp.logical_and(i == 0, f == 0))
    def _():
        def body(r, c):
            row_copy(tok_ref[0, 0, r], r).start()
            return c
        lax.fori_loop(0, n_gather, body, 0, unroll=DMA_LOOP_UNROLL)

    @pl.when(jnp.logical_and(f == 0, i <= n_real))
    def _():
        wait_gather()

    @pl.when(i < n_real)
    def _():
        @pl.when(f == 0)
        def _():
            xn_ref[...] = _rms(xg_ref[:bm, :], g_ref[...]).astype(BF16)
            o_ref[...] = jnp.zeros_like(o_ref)

        for u in range(rows_per_step):
            r = f * rows_per_step + u
            row_copy(tok_next_ref[0, 0, r], r).start()

        xn = xn_ref[...]
        a = (jax.nn.silu(_dot(xn, wg_ref[0])) * _dot(xn, wu_ref[0])).astype(BF16)
        o_ref[...] += _dot(a, wd_ref[0])

        @pl.when(jnp.logical_and(i == nblk - 1, f == nf - 1))
        def _():
            wait_gather()

    @pl.when(jnp.logical_and(i >= n_real, f == nf - 1))
    def _():
        o_ref[...] = jnp.zeros_like(o_ref)


def _expert_ffn(block_expert, n_real, buf_tok, h, g, wg, wu, wd, *, bm, bf):
    d = h.shape[1]
    ff = wg.shape[2]
    bf = min(bf, ff)
    nblk, nf = buf_tok.shape[0] // bm, ff // bf
    rows_per_step = pl.cdiv(bm, nf)
    n_gather = rows_per_step * nf
    tok = jnp.pad(buf_tok.reshape(nblk, 1, bm), ((0, 0), (0, 0), (0, n_gather - bm)))

    def blk(i, nr):
        return jnp.minimum(i, nr[0] - 1)

    def fblk(i, f, nr):
        return jnp.where(i < nr[0], f, nf - 1)

    return pl.pallas_call(
        functools.partial(_expert_kernel, rows_per_step=rows_per_step),
        grid_spec=pltpu.PrefetchScalarGridSpec(
            num_scalar_prefetch=2,
            grid=(nblk, nf),
            in_specs=[
                pl.BlockSpec((1, 1, n_gather), lambda i, f, be, nr: (i, 0, 0), memory_space=pltpu.SMEM),
                pl.BlockSpec((1, 1, n_gather), lambda i, f, be, nr: (jnp.minimum(i + 1, nblk - 1), 0, 0),
                             memory_space=pltpu.SMEM),
                pl.BlockSpec(memory_space=pl.ANY),
                pl.BlockSpec((1, d), lambda i, f, be, nr: (0, 0)),
                pl.BlockSpec((1, d, bf), lambda i, f, be, nr: (be[blk(i, nr)], 0, fblk(i, f, nr))),
                pl.BlockSpec((1, d, bf), lambda i, f, be, nr: (be[blk(i, nr)], 0, fblk(i, f, nr))),
                pl.BlockSpec((1, bf, d), lambda i, f, be, nr: (be[blk(i, nr)], fblk(i, f, nr), 0)),
            ],
            out_specs=pl.BlockSpec((bm, d), lambda i, f, be, nr: (i, 0)),
            scratch_shapes=[pltpu.VMEM((pl.cdiv(n_gather, F32_SUBLANES) * F32_SUBLANES, d), F32), pltpu.VMEM((bm, d), BF16),
                            pltpu.SemaphoreType.DMA(())],
        ),
        out_shape=jax.ShapeDtypeStruct((nblk * bm, d), F32),
        compiler_params=_params("arbitrary", "arbitrary"),
        name="moe_experts",
    )(block_expert, n_real, tok, tok, h, g, wg, wu, wd)


def _combine_kernel(p0_ref, p1_ref, p0_next_ref, p1_next_ref, h_ref, route_ref, y_ref, o_ref, ybuf, sem, *, tm):
    i = pl.program_id(0)
    slot = i % 2

    def row_copy(src_row, s, k, r):
        return pltpu.make_async_copy(y_ref.at[pl.ds(src_row, 1)], ybuf.at[s, k, pl.ds(r, 1)], sem.at[s])

    def start_gather(a_ref, b_ref, s):
        def body(r, c):
            row_copy(a_ref[0, 0, r], s, 0, r).start()
            row_copy(b_ref[0, 0, r], s, 1, r).start()
            return c
        lax.fori_loop(0, tm, body, 0, unroll=DMA_LOOP_UNROLL)

    def wait_gather(s):
        def body(r, c):
            row_copy(0, s, 0, r).wait()
            row_copy(0, s, 1, r).wait()
            return c
        lax.fori_loop(0, tm, body, 0, unroll=DMA_LOOP_UNROLL)

    @pl.when(i == 0)
    def _():
        start_gather(p0_ref, p1_ref, 0)

    @pl.when(i + 1 < pl.num_programs(0))
    def _():
        start_gather(p0_next_ref, p1_next_ref, 1 - slot)

    wait_gather(slot)
    g0 = route_ref[:, TOP_K:TOP_K + 1]
    g1 = route_ref[:, TOP_K + 1:TOP_K + 2]
    o_ref[...] = h_ref[...] + (ybuf[slot, 0] * g0 + ybuf[slot, 1] * g1)


def _combine(p0, p1, h, route, y, *, tm):
    m, d = h.shape
    tm = min(tm, m)
    nt = m // tm
    idx = pl.BlockSpec((1, 1, tm), lambda i: (i, 0, 0), memory_space=pltpu.SMEM)
    idx_next = pl.BlockSpec((1, 1, tm), lambda i: (jnp.minimum(i + 1, nt - 1), 0, 0), memory_space=pltpu.SMEM)
    row = pl.BlockSpec((tm, d), lambda i: (i, 0))
    p0, p1 = p0.reshape(nt, 1, tm), p1.reshape(nt, 1, tm)
    return pl.pallas_call(
        functools.partial(_combine_kernel, tm=tm),
        grid=(nt,),
        in_specs=[idx, idx, idx_next, idx_next, row, pl.BlockSpec((tm, LANES), lambda i: (i, 0)),
                  pl.BlockSpec(memory_space=pl.ANY)],
        out_specs=row,
        out_shape=jax.ShapeDtypeStruct((m, d), F32),
        scratch_shapes=[pltpu.VMEM((2, 2, tm, d), F32), pltpu.SemaphoreType.DMA((2,))],
        compiler_params=_params("arbitrary"),
        name="moe_combine",
    )(p0, p1, p0, p1, h, route, y)


def _route_plan(route, *, n_experts, bm):
    n = route.shape[0]
    flat_e = route[:, :TOP_K].astype(jnp.int32).reshape(-1)
    onehot = (flat_e[:, None] == jnp.arange(n_experts, dtype=jnp.int32)[None, :]).astype(jnp.int32)
    csum = jnp.cumsum(onehot, axis=0)
    rank = jnp.take_along_axis(csum, flat_e[:, None], axis=1)[:, 0] - 1
    counts = csum[-1]
    blocks = (counts + bm - 1) // bm
    blk_end = jnp.cumsum(blocks)
    blk_start = blk_end - blocks
    rows = blk_start[flat_e] * bm + rank
    nblk = -(-(n * TOP_K) // bm) + n_experts
    block_expert = jnp.minimum(jnp.searchsorted(blk_end, jnp.arange(nblk), side="right"),
                               n_experts - 1).astype(jnp.int32)
    buf_tok = jnp.zeros((nblk * bm,), jnp.int32).at[rows].set(jnp.arange(n * TOP_K, dtype=jnp.int32) // TOP_K)
    n_real = blk_end[-1:].astype(jnp.int32)
    return block_expert, n_real, buf_tok, rows[0::TOP_K], rows[1::TOP_K]


def kernel(x, meta_tokens, attn_norm, w_qkv, q_gain, k_gain, rel_bias, meta_bias, w_o, conv_norm, w_in, conv_w,
           w_out, dense_norm, w_gate, w_up, w_down, moe_norm, w_router, moe_gate, moe_up, moe_down):
    batch, seq, d = x.shape
    n = batch * seq
    h = d // HEAD_DIM
    n_experts = w_router.shape[-1]
    assert d % HEAD_DIM == 0 and seq % GRID_W == 0 and seq // GRID_W >= WIN_ROWS
    assert meta_tokens.shape[0] == N_META and rel_bias.shape[1:] == (h, 2 * WIN_ROWS - 1, 2 * WIN_COLS - 1)
    assert all(p.shape[0] == 1 for p in (attn_norm, conv_norm, dense_norm, moe_norm)), "two-layer trunk only"

    xt = x.reshape(n, d)
    mt = jnp.broadcast_to(meta_tokens[None].astype(x.dtype), (batch, N_META, d)).reshape(batch * N_META, d)
    nm = batch * N_META

    g_attn = attn_norm[0].reshape(1, d)
    wqkv = w_qkv[0].astype(BF16)
    head_gain = jnp.concatenate([jnp.tile(q_gain[0] * (HEAD_DIM ** -0.5), h), jnp.tile(k_gain[0], h),
                                 jnp.ones((d,), F32)]).reshape(1, 3 * d)
    qkv = _qkv(xt, g_attn, wqkv, head_gain, bm=1024, bn=1024)
    qkv_m = _qkv(mt, g_attn, wqkv, head_gain, bm=nm, bn=1024)

    def meta_heads(part):
        t = qkv_m[:, part * d:(part + 1) * d].reshape(batch, N_META, h, HEAD_DIM).transpose(0, 2, 1, 3)
        return jnp.pad(t, ((0, 0), (0, 0), (0, LANES - N_META), (0, 0)))

    bias = _attention_bias(rel_bias[0], meta_bias[0])
    n_e, _, d_e = moe_gate.shape[1:]
    o, moe_gate_bf = _attention(qkv, meta_heads(1), meta_heads(2), bias, batch=batch, seq=seq, rb=16,
                                rider=moe_gate[0].reshape(n_e * d, d_e))
    o_m = _meta_attention(qkv_m, meta_bias[0], batch=batch)
    wo = w_o[0].astype(BF16)
    h1 = _proj_residual(o, wo, xt, bm=512)
    h1_m = _proj_residual(o_m, wo, mt, bm=nm)

    g_dense = dense_norm[0].reshape(1, d)
    wg, wu, wd = w_gate[0].astype(BF16), w_up[0].astype(BF16), w_down[0].astype(BF16)
    h2, moe_up_bf = _dense_ffn(h1, g_dense, wg, wu, wd, bm=512, bf=FFN_BLOCK_F, rider=moe_up[0].reshape(n_e * d, d_e))
    (h2_m,) = _dense_ffn(h1_m, g_dense, wg, wu, wd, bm=nm, bf=FFN_BLOCK_F)

    g_conv = conv_norm[0].reshape(1, d)
    win = w_in[0].astype(BF16)
    bg, z, moe_down_bf = _conv_in_proj(h2, g_conv, win, bm=1024, bn=512, rider=moe_down[0].reshape(n_e * d_e, d))
    _, z_m = _conv_in_proj(h2_m, g_conv, win, bm=nm, bn=512)
    z_meta_last = z_m.reshape(batch, N_META, d)[:, N_META - 1:, :].astype(F32)
    h3 = _conv_proj_residual(bg, z, z_meta_last, conv_w[0].T, w_out[0].astype(BF16), h2, seq=seq, bm=512)

    g_moe = moe_norm[0].reshape(1, d)
    wr_pad = jnp.pad(w_router[0], ((0, 0), (0, LANES - n_experts)))
    route = _router(h3, g_moe, wr_pad, n_experts=n_experts, bm=512)
    bm_e = 512
    block_expert, n_real, buf_tok, p0, p1 = _route_plan(route, n_experts=n_experts, bm=bm_e)
    ybuf = _expert_ffn(block_expert, n_real, buf_tok, h3, g_moe, moe_gate_bf.reshape(n_e, d, d_e),
                       moe_up_bf.reshape(n_e, d, d_e), moe_down_bf.reshape(n_e, d_e, d), bm=bm_e, bf=FFN_BLOCK_F)
    out = _combine(p0, p1, h3, route, ybuf, tm=256)
    return out.reshape(batch, seq, d)
```

```python
import functools

import numpy as np
import jax
import jax.numpy as jnp
from jax import lax
from jax.experimental import pallas as pl
from jax.experimental.pallas import tpu as pltpu

HEAD_DIM = 128
GRID_W = 64
WIN_ROWS = 8
WIN_COLS = 16
N_META = 16
TOP_K = 2
RMS_EPS = 1e-6
LANES = 128
MXU_WIDTH = 256
F32_SUBLANES = 8
BF16_SUBLANES = 16
VMEM_LIMIT_BYTES = 56 * 1024 * 1024
MASK_VALUE = -1e30
N_LOC = WIN_ROWS * GRID_W
ATTN_ROW_GROUP = 16
DMA_LOOP_UNROLL = 8
FFN_BLOCK_F = 1024
CONV_ROW_CHUNKS = 2

F32 = jnp.float32
BF16 = jnp.bfloat16


def _params(*sem):
    return pltpu.CompilerParams(dimension_semantics=sem, vmem_limit_bytes=VMEM_LIMIT_BYTES)


def _rms(xf, g):
    return xf * lax.rsqrt(jnp.mean(xf * xf, axis=-1, keepdims=True) + RMS_EPS) * g


def _dot(a, b):
    return jnp.dot(a, b, preferred_element_type=F32)


def _dot_nt(a, b):
    return lax.dot_general(a, b, (((1,), (1,)), ((), ())), preferred_element_type=F32)


def _with_cast_rider(body, n_in, n_out):
    def wrapped(*refs):
        rider_src, rider_dst = refs[n_in], refs[n_in + 1 + n_out]
        rider_dst[...] = rider_src[...].astype(rider_dst.dtype)
        body(*refs[:n_in], *refs[n_in + 1:n_in + 1 + n_out], *refs[n_in + 2 + n_out:])
    return wrapped


def _hosted_call(body, *, rider, rider_blocks, rider_index, in_specs, out_specs, out_shape, args, **kwargs):
    if rider is not None:
        rows, cols = rider.shape
        nr, nc = rider_blocks
        assert rows % (nr * BF16_SUBLANES) == 0 and cols % (nc * LANES) == 0, (rider.shape, rider_blocks)
        spec = pl.BlockSpec((rows // nr, cols // nc), rider_index)
        body = _with_cast_rider(body, len(in_specs), len(out_specs))
        in_specs, out_specs = in_specs + [spec], out_specs + [spec]
        out_shape = out_shape + [jax.ShapeDtypeStruct(rider.shape, BF16)]
        args = args + [rider]
    return pl.pallas_call(body, in_specs=in_specs, out_specs=out_specs, out_shape=out_shape, **kwargs)(*args)


def _qkv_kernel(x_ref, g_ref, w_ref, hg_ref, o_ref, xn_ref, *, n_norm_blocks):
    j = pl.program_id(1)

    @pl.when(j == 0)
    def _():
        xn_ref[...] = _rms(x_ref[...], g_ref[...]).astype(BF16)

    normed = j < n_norm_blocks
    xn = xn_ref[...]
    for c in range(o_ref.shape[1] // MXU_WIDTH):
        acc = _dot(xn, w_ref[:, c * MXU_WIDTH:(c + 1) * MXU_WIDTH])
        for hh in range(MXU_WIDTH // HEAD_DIM):
            sl = slice(c * MXU_WIDTH + hh * HEAD_DIM, c * MXU_WIDTH + (hh + 1) * HEAD_DIM)
            t = acc[:, hh * HEAD_DIM:(hh + 1) * HEAD_DIM]
            o_ref[:, sl] = jnp.where(normed, _rms(t, hg_ref[:, sl]), t).astype(BF16)


def _qkv(x, g, w, hg, *, bm, bn):
    m, d = x.shape
    n = w.shape[1]
    bm, bn = min(bm, m), min(bn, d)
    return pl.pallas_call(
        functools.partial(_qkv_kernel, n_norm_blocks=2 * d // bn),
        grid=(m // bm, n // bn),
        in_specs=[
            pl.BlockSpec((bm, d), lambda i, j: (i, 0)),
            pl.BlockSpec((1, d), lambda i, j: (0, 0)),
            pl.BlockSpec((d, bn), lambda i, j: (0, j)),
            pl.BlockSpec((1, bn), lambda i, j: (0, j)),
        ],
        out_specs=pl.BlockSpec((bm, bn), lambda i, j: (i, j)),
        out_shape=jax.ShapeDtypeStruct((m, n), BF16),
        scratch_shapes=[pltpu.VMEM((bm, d), BF16)],
        compiler_params=_params("arbitrary", "arbitrary"),
        name="qkv_proj",
    )(x, g, w, hg)


def _attn_kernel(q_ref, k_ref, v_ref, km_ref, vm_ref, bias_ref, o_ref, *, rb, n_rows):
    rblk = pl.program_id(2)
    km = km_ref[0, 0]
    vm = vm_ref[0, 0]

    def group(gi, carry):
        scores, probs = [], []
        for u in range(ATTN_ROW_GROUP):
            rr = gi * ATTN_ROW_GROUP + u
            r = rblk * rb + rr
            r0 = jnp.clip(r - WIN_ROWS // 2, 0, n_rows - WIN_ROWS)
            start = pl.multiple_of(r0 * GRID_W, GRID_W)
            qs = pl.multiple_of(rr * GRID_W, GRID_W)
            q = q_ref[pl.ds(qs, GRID_W), :]
            kw = k_ref[pl.ds(start, N_LOC), :]
            s = jnp.concatenate([_dot_nt(q, kw), _dot_nt(q, km)], axis=-1) + bias_ref[r - r0, 0]
            scores.append((qs, start, s))
        for qs, start, s in scores:
            p = jnp.exp(s - jnp.max(s, axis=-1, keepdims=True))
            probs.append((qs, start, p.astype(BF16), jnp.sum(p, axis=-1, keepdims=True)))
        for qs, start, pb, l in probs:
            vw = v_ref[pl.ds(start, N_LOC), :]
            o = _dot(pb[:, :N_LOC], vw) + _dot(pb[:, N_LOC:], vm)
            o_ref[pl.ds(qs, GRID_W), :] = (o / l).astype(BF16)
        return carry

    lax.fori_loop(0, rb // ATTN_ROW_GROUP, group, 0)


def _attention(qkv, km, vm, bias, *, batch, seq, rb, rider=None):
    n, d3 = qkv.shape
    d = d3 // 3
    h = d // HEAD_DIM
    n_rows = seq // GRID_W
    rb = min(rb, n_rows)
    nrb = n_rows // rb
    kv_spec = lambda off: pl.BlockSpec((seq, HEAD_DIM), lambda b, hh, r: (b, off + hh))
    meta_spec = pl.BlockSpec((1, 1, LANES, HEAD_DIM), lambda b, hh, r: (b, hh, 0, 0))
    return _hosted_call(
        functools.partial(_attn_kernel, rb=rb, n_rows=n_rows),
        rider=rider, rider_blocks=(batch * h * nrb, 1), rider_index=lambda b, hh, r: ((b * h + hh) * nrb + r, 0),
        grid=(batch, h, nrb),
        in_specs=[
            pl.BlockSpec((rb * GRID_W, HEAD_DIM), lambda b, hh, r: (b * nrb + r, hh)),
            kv_spec(h),
            kv_spec(2 * h),
            meta_spec,
            meta_spec,
            pl.BlockSpec((WIN_ROWS, 1, GRID_W, N_LOC + LANES), lambda b, hh, r: (0, hh, 0, 0)),
        ],
        out_specs=[pl.BlockSpec((rb * GRID_W, HEAD_DIM), lambda b, hh, r: (b * nrb + r, hh))],
        out_shape=[jax.ShapeDtypeStruct((n, d), BF16)],
        args=[qkv, qkv, qkv, km, vm, bias],
        compiler_params=_params("arbitrary", "arbitrary", "arbitrary"),
        name="nbr_attention",
    )


def _meta_attn_kernel(q_ref, k_ref, v_ref, mb_ref, o_ref):
    s = _dot_nt(q_ref[...], k_ref[...]) + mb_ref[0]
    p = jnp.exp(s - jnp.max(s, axis=-1, keepdims=True))
    l = jnp.sum(p, axis=-1, keepdims=True)
    o_ref[...] = (_dot(p.astype(BF16), v_ref[...]) / l).astype(BF16)


def _meta_attention(qkv_m, meta_bias, *, batch):
    d = qkv_m.shape[1] // 3
    h = d // HEAD_DIM
    spec = lambda off: pl.BlockSpec((N_META, HEAD_DIM), lambda b, hh: (b, off + hh))
    return pl.pallas_call(
        _meta_attn_kernel,
        grid=(batch, h),
        in_specs=[spec(0), spec(h), spec(2 * h),
                  pl.BlockSpec((1, 1, N_META), lambda b, hh: (hh, 0, 0))],
        out_specs=spec(0),
        out_shape=jax.ShapeDtypeStruct((batch * N_META, d), BF16),
        compiler_params=_params("arbitrary", "arbitrary"),
        name="meta_attention",
    )(qkv_m, qkv_m, qkv_m, meta_bias.reshape(h, 1, N_META))


def _attention_bias(rel_bias, meta_bias):
    h = rel_bias.shape[0]
    cq = np.arange(GRID_W)
    c0 = np.clip(cq - WIN_COLS // 2, 0, GRID_W - WIN_COLS)
    ck = np.arange(GRID_W)
    in_win = (ck[None, :] >= c0[:, None]) & (ck[None, :] < c0[:, None] + WIN_COLS)
    dc = ck[None, :] - cq[:, None] + WIN_COLS - 1
    by_col = jnp.full((h, 2 * WIN_ROWS - 1, GRID_W, GRID_W), MASK_VALUE, F32)
    for j in range(2 * WIN_COLS - 1):
        by_col = jnp.where((in_win & (dc == j))[None, None], rel_bias[:, :, j, None, None], by_col)
    loc = jnp.stack([by_col[:, WIN_ROWS - 1 - off:2 * WIN_ROWS - 1 - off] for off in range(WIN_ROWS)])
    loc = loc.transpose(0, 1, 3, 2, 4).reshape(WIN_ROWS, h, GRID_W, N_LOC)
    met = jnp.concatenate([meta_bias, jnp.full((h, LANES - N_META), MASK_VALUE, F32)], axis=-1)
    met = jnp.broadcast_to(met[None, :, None, :], (WIN_ROWS, h, GRID_W, LANES))
    return jnp.concatenate([loc, met], axis=-1).astype(F32)


def _proj_kernel(a_ref, w_ref, r_ref, o_ref):
    o_ref[...] = r_ref[...] + _dot(a_ref[...], w_ref[...])


def _proj_residual(a, w, r, *, bm):
    m, d = r.shape
    bm = min(bm, m)
    row = pl.BlockSpec((bm, d), lambda i: (i, 0))
    return pl.pallas_call(
        _proj_kernel,
        grid=(m // bm,),
        in_specs=[row, pl.BlockSpec((d, d), lambda i: (0, 0)), row],
        out_specs=row,
        out_shape=jax.ShapeDtypeStruct((m, d), F32),
        compiler_params=_params("arbitrary"),
        name="attn_out_proj",
    )(a, w, r)


def _conv_proj_kernel(bg_ref, z_ref, zp_ref, zn_ref, zm_ref, cw_ref, w_ref, r_ref, gr_ref, wr_hi_ref, wr_lo_ref,
                      o_ref, route_ref, *, blocks_per_seq, n_experts):
    i = pl.program_id(0)
    z = z_ref[...].astype(F32)
    bm = z.shape[0]
    first = (i % blocks_per_seq) == 0
    last = (i % blocks_per_seq) == blocks_per_seq - 1
    prev_row = jnp.where(first, zm_ref[0], zp_ref[BF16_SUBLANES - 1:BF16_SUBLANES, :].astype(F32))
    next_row = jnp.where(last, 0.0, zn_ref[0:1, :].astype(F32))
    row = lax.broadcasted_iota(jnp.int32, z.shape, 0)
    z_prev = jnp.where(row == 0, prev_row, pltpu.roll(z, 1, axis=0))
    z_next = jnp.where(row == bm - 1, next_row, pltpu.roll(z, bm - 1, axis=0))
    y = z_prev * cw_ref[0:1, :] + z * cw_ref[1:2, :] + z_next * cw_ref[2:3, :]
    a = (bg_ref[...].astype(F32) * y).astype(BF16)
    for s in range(0, bm, bm // CONV_ROW_CHUNKS):
        rows = slice(s, s + bm // CONV_ROW_CHUNKS)
        h = r_ref[rows, :] + _dot(a[rows, :], w_ref[...])
        o_ref[rows, :] = h
        route_ref[rows, :] = _route(h, gr_ref[...], wr_hi_ref[...], wr_lo_ref[...], n_experts)


def _conv_proj_residual(bg, z, z_meta_last, conv_w_t, w, r, g_route, wr_hi, wr_lo, *, seq, bm, n_experts):
    m, d = r.shape
    bm = min(bm, seq)
    bps = seq // bm
    halo = bm // BF16_SUBLANES
    n_halo = m // BF16_SUBLANES
    row = pl.BlockSpec((bm, d), lambda i: (i, 0))
    const = lambda shape: pl.BlockSpec(shape, lambda i: (0,) * len(shape), pipeline_mode=pl.Buffered(1))
    return pl.pallas_call(
        functools.partial(_conv_proj_kernel, blocks_per_seq=bps, n_experts=n_experts),
        grid=(m // bm,),
        in_specs=[
            row, row,
            pl.BlockSpec((BF16_SUBLANES, d), lambda i: (jnp.maximum(i * halo - 1, 0), 0)),
            pl.BlockSpec((BF16_SUBLANES, d), lambda i: (jnp.minimum((i + 1) * halo, n_halo - 1), 0)),
            pl.BlockSpec((1, 1, d), lambda i: (i // bps, 0, 0)),
            const((3, d)),
            const((d, d)),
            row,
            const((1, d)),
            const((d, LANES)),
            const((d, LANES)),
        ],
        out_specs=[row, pl.BlockSpec((bm, LANES), lambda i: (i, 0))],
        out_shape=[jax.ShapeDtypeStruct((m, d), F32), jax.ShapeDtypeStruct((m, LANES), F32)],
        compiler_params=_params("arbitrary"),
        name="conv_out_proj",
    )(bg, z, z, z, z_meta_last, conv_w_t, w, r, g_route, wr_hi, wr_lo)


def _win_kernel(h_ref, g_ref, wb_ref, wc_ref, wu_ref, bg_ref, z_ref, xn_ref):
    @pl.when(pl.program_id(1) == 0)
    def _():
        xn_ref[...] = _rms(h_ref[...], g_ref[...]).astype(BF16)

    xn = xn_ref[...]
    bg_ref[...] = _dot(xn, wb_ref[...]).astype(BF16)
    z_ref[...] = (_dot(xn, wc_ref[...]) * _dot(xn, wu_ref[...])).astype(BF16)


def _conv_in_proj(h, g, w_in, *, bm, bn, rider=None):
    m, d = h.shape
    bm, bn = min(bm, m), min(bn, d)
    nb = d // bn
    wspec = lambda part: pl.BlockSpec((d, bn), lambda i, j: (0, part * nb + j))
    out = pl.BlockSpec((bm, bn), lambda i, j: (i, j))
    return _hosted_call(
        _win_kernel,
        rider=rider, rider_blocks=(m // bm, nb), rider_index=lambda i, j: (i, j),
        grid=(m // bm, nb),
        in_specs=[pl.BlockSpec((bm, d), lambda i, j: (i, 0)), pl.BlockSpec((1, d), lambda i, j: (0, 0)),
                  wspec(0), wspec(1), wspec(2)],
        out_specs=[out, out],
        out_shape=[jax.ShapeDtypeStruct((m, d), BF16), jax.ShapeDtypeStruct((m, d), BF16)],
        args=[h, g, w_in, w_in, w_in],
        scratch_shapes=[pltpu.VMEM((bm, d), BF16)],
        compiler_params=_params("arbitrary", "arbitrary"),
        name="conv_in_proj",
    )


def _ffn_kernel(h_ref, g_ref, wg_ref, wu_ref, wd_ref, o_ref, xn_ref):
    f = pl.program_id(1)

    @pl.when(f == 0)
    def _():
        xn_ref[...] = _rms(h_ref[...], g_ref[...]).astype(BF16)
        o_ref[...] = h_ref[...]

    xn = xn_ref[...]
    a = (jax.nn.silu(_dot(xn, wg_ref[...])) * _dot(xn, wu_ref[...])).astype(BF16)
    o_ref[...] += _dot(a, wd_ref[...])


def _dense_ffn(h, g, wg, wu, wd, *, bm, bf, rider=None):
    m, d = h.shape
    ff = wg.shape[1]
    bm, bf = min(bm, m), min(bf, ff)
    row = pl.BlockSpec((bm, d), lambda i, f: (i, 0))
    return _hosted_call(
        _ffn_kernel,
        rider=rider, rider_blocks=(m // bm, ff // bf), rider_index=lambda i, f: (i, f),
        grid=(m // bm, ff // bf),
        in_specs=[row, pl.BlockSpec((1, d), lambda i, f: (0, 0)),
                  pl.BlockSpec((d, bf), lambda i, f: (0, f)),
                  pl.BlockSpec((d, bf), lambda i, f: (0, f)),
                  pl.BlockSpec((bf, d), lambda i, f: (f, 0))],
        out_specs=[row],
        out_shape=[jax.ShapeDtypeStruct((m, d), F32)],
        args=[h, g, wg, wu, wd],
        scratch_shapes=[pltpu.VMEM((bm, d), BF16)],
        compiler_params=_params("arbitrary", "arbitrary"),
        name="dense_swiglu",
    )


def _route(h, g, wr_hi, wr_lo, n_experts):
    xn = _rms(h, g)
    x_hi = xn.astype(BF16)
    x_lo = (xn - x_hi.astype(F32)).astype(BF16)
    logits = _dot(x_hi, wr_hi) + (_dot(x_lo, wr_hi) + _dot(x_hi, wr_lo))
    lane = lax.broadcasted_iota(jnp.int32, logits.shape, 1)
    neg = -jnp.inf
    l1 = jnp.where(lane < n_experts, logits, neg)
    m1 = jnp.max(l1, axis=-1, keepdims=True)
    i1 = jnp.min(jnp.where(l1 == m1, lane, LANES), axis=-1, keepdims=True)
    l2 = jnp.where(lane == i1, neg, l1)
    m2 = jnp.max(l2, axis=-1, keepdims=True)
    i2 = jnp.min(jnp.where(l2 == m2, lane, LANES), axis=-1, keepdims=True)
    t = jnp.exp(m2 - m1)
    g1 = 1.0 / (1.0 + t)
    g2 = t / (1.0 + t)
    return jnp.where(lane == 0, i1.astype(F32),
                     jnp.where(lane == 1, i2.astype(F32),
                               jnp.where(lane == 2, g1, jnp.where(lane == 3, g2, 0.0))))


def _expert_kernel(be_ref, nreal_ref, tok_ref, tok_next_ref, h_ref, g_ref, wg_ref, wu_ref, wd_ref, o_ref,
                   xg_ref, xn_ref, sem, *, rows_per_step):
    i = pl.program_id(0)
    f = pl.program_id(1)
    nblk, nf = pl.num_programs(0), pl.num_programs(1)
    n_real = nreal_ref[0]
    bm = xn_ref.shape[0]
    n_gather = tok_ref.shape[2]

    def row_copy(tok, r):
        return pltpu.make_async_copy(h_ref.at[pl.ds(tok, 1)], xg_ref.at[pl.ds(r, 1)], sem)

    def wait_gather():
        def body(r, c):
            row_copy(0, r).wait()
            return c
        lax.fori_loop(0, n_gather, body, 0, unroll=DMA_LOOP_UNROLL)

    @pl.when(jnp.logical_and(i == 0, f == 0))
    def _():
        def body(r, c):
            row_copy(tok_ref[0, 0, r], r).start()
            return c
        lax.fori_loop(0, n_gather, body, 0, unroll=DMA_LOOP_UNROLL)

    @pl.when(jnp.logical_and(f == 0, i <= n_real))
    def _():
        wait_gather()

    @pl.when(i < n_real)
    def _():
        @pl.when(f == 0)
        def _():
            xn_ref[...] = _rms(xg_ref[:bm, :], g_ref[...]).astype(BF16)
            o_ref[...] = jnp.zeros_like(o_ref)

        for u in range(rows_per_step):
            r = f * rows_per_step + u
            row_copy(tok_next_ref[0, 0, r], r).start()

        xn = xn_ref[...]
        a = (jax.nn.silu(_dot(xn, wg_ref[0])) * _dot(xn, wu_ref[0])).astype(BF16)
        o_ref[...] += _dot(a, wd_ref[0])

        @pl.when(jnp.logical_and(i == nblk - 1, f == nf - 1))
        def _():
            wait_gather()

    @pl.when(jnp.logical_and(i >= n_real, f == nf - 1))
    def _():
        o_ref[...] = jnp.zeros_like(o_ref)


def _expert_ffn(block_expert, n_real, buf_tok, h, g, wg, wu, wd, *, bm, bf):
    d = h.shape[1]
    ff = wg.shape[2]
    bf = min(bf, ff)
    nblk, nf = buf_tok.shape[0] // bm, ff // bf
    rows_per_step = pl.cdiv(bm, nf)
    n_gather = rows_per_step * nf
    tok = jnp.pad(buf_tok.reshape(nblk, 1, bm), ((0, 0), (0, 0), (0, n_gather - bm)))

    def blk(i, nr):
        return jnp.minimum(i, nr[0] - 1)

    def fblk(i, f, nr):
        return jnp.where(i < nr[0], f, nf - 1)

    return pl.pallas_call(
        functools.partial(_expert_kernel, rows_per_step=rows_per_step),
        grid_spec=pltpu.PrefetchScalarGridSpec(
            num_scalar_prefetch=2,
            grid=(nblk, nf),
            in_specs=[
                pl.BlockSpec((1, 1, n_gather), lambda i, f, be, nr: (i, 0, 0), memory_space=pltpu.SMEM),
                pl.BlockSpec((1, 1, n_gather), lambda i, f, be, nr: (jnp.minimum(i + 1, nblk - 1), 0, 0),
                             memory_space=pltpu.SMEM),
                pl.BlockSpec(memory_space=pl.ANY),
                pl.BlockSpec((1, d), lambda i, f, be, nr: (0, 0)),
                pl.BlockSpec((1, d, bf), lambda i, f, be, nr: (be[blk(i, nr)], 0, fblk(i, f, nr))),
                pl.BlockSpec((1, d, bf), lambda i, f, be, nr: (be[blk(i, nr)], 0, fblk(i, f, nr))),
                pl.BlockSpec((1, bf, d), lambda i, f, be, nr: (be[blk(i, nr)], fblk(i, f, nr), 0)),
            ],
            out_specs=pl.BlockSpec((bm, d), lambda i, f, be, nr: (i, 0)),
            scratch_shapes=[pltpu.VMEM((pl.cdiv(n_gather, F32_SUBLANES) * F32_SUBLANES, d), F32), pltpu.VMEM((bm, d), BF16),
                            pltpu.SemaphoreType.DMA(())],
        ),
        out_shape=jax.ShapeDtypeStruct((nblk * bm, d), F32),
        compiler_params=_params("arbitrary", "arbitrary"),
        name="moe_experts",
    )(block_expert, n_real, tok, tok, h, g, wg, wu, wd)


def _combine_kernel(p0_ref, p1_ref, p0_next_ref, p1_next_ref, h_ref, route_ref, y_ref, o_ref, ybuf, sem, *, tm):
    i = pl.program_id(0)
    slot = i % 2

    def row_copy(src_row, s, k, r):
        return pltpu.make_async_copy(y_ref.at[pl.ds(src_row, 1)], ybuf.at[s, k, pl.ds(r, 1)], sem.at[s])

    def start_gather(a_ref, b_ref, s):
        def body(r, c):
            row_copy(a_ref[0, 0, r], s, 0, r).start()
            row_copy(b_ref[0, 0, r], s, 1, r).start()
            return c
        lax.fori_loop(0, tm, body, 0, unroll=DMA_LOOP_UNROLL)

    def wait_gather(s):
        def body(r, c):
            row_copy(0, s, 0, r).wait()
            row_copy(0, s, 1, r).wait()
            return c
        lax.fori_loop(0, tm, body, 0, unroll=DMA_LOOP_UNROLL)

    @pl.when(i == 0)
    def _():
        start_gather(p0_ref, p1_ref, 0)

    @pl.when(i + 1 < pl.num_programs(0))
    def _():
        start_gather(p0_next_ref, p1_next_ref, 1 - slot)

    wait_gather(slot)
    g0 = route_ref[:, TOP_K:TOP_K + 1]
    g1 = route_ref[:, TOP_K + 1:TOP_K + 2]
    o_ref[...] = h_ref[...] + (ybuf[slot, 0] * g0 + ybuf[slot, 1] * g1)


def _combine(p0, p1, h, route, y, *, tm):
    m, d = h.shape
    tm = min(tm, m)
    nt = m // tm
    idx = pl.BlockSpec((1, 1, tm), lambda i: (i, 0, 0), memory_space=pltpu.SMEM)
    idx_next = pl.BlockSpec((1, 1, tm), lambda i: (jnp.minimum(i + 1, nt - 1), 0, 0), memory_space=pltpu.SMEM)
    row = pl.BlockSpec((tm, d), lambda i: (i, 0))
    p0, p1 = p0.reshape(nt, 1, tm), p1.reshape(nt, 1, tm)
    return pl.pallas_call(
        functools.partial(_combine_kernel, tm=tm),
        grid=(nt,),
        in_specs=[idx, idx, idx_next, idx_next, row, pl.BlockSpec((tm, LANES), lambda i: (i, 0)),
                  pl.BlockSpec(memory_space=pl.ANY)],
        out_specs=row,
        out_shape=jax.ShapeDtypeStruct((m, d), F32),
        scratch_shapes=[pltpu.VMEM((2, 2, tm, d), F32), pltpu.SemaphoreType.DMA((2,))],
        compiler_params=_params("arbitrary"),
        name="moe_combine",
    )(p0, p1, p0, p1, h, route, y)


def _route_plan(route, *, n_experts, bm):
    n = route.shape[0]
    flat_e = route[:, :TOP_K].astype(jnp.int32).reshape(-1)
    onehot = (flat_e[:, None] == jnp.arange(n_experts, dtype=jnp.int32)[None, :]).astype(jnp.int32)
    csum = jnp.cumsum(onehot, axis=0)
    rank = jnp.take_along_axis(csum, flat_e[:, None], axis=1)[:, 0] - 1
    counts = csum[-1]
    blocks = (counts + bm - 1) // bm
    blk_end = jnp.cumsum(blocks)
    blk_start = blk_end - blocks
    rows = blk_start[flat_e] * bm + rank
    nblk = -(-(n * TOP_K) // bm) + n_experts
    block_expert = jnp.minimum(jnp.searchsorted(blk_end, jnp.arange(nblk), side="right"),
                               n_experts - 1).astype(jnp.int32)
    buf_tok = jnp.zeros((nblk * bm,), jnp.int32).at[rows].set(jnp.arange(n * TOP_K, dtype=jnp.int32) // TOP_K)
    n_real = blk_end[-1:].astype(jnp.int32)
    return block_expert, n_real, buf_tok, rows[0::TOP_K], rows[1::TOP_K]


def kernel(x, meta_tokens, attn_norm, w_qkv, q_gain, k_gain, rel_bias, meta_bias, w_o, conv_norm, w_in, conv_w,
           w_out, dense_norm, w_gate, w_up, w_down, moe_norm, w_router, moe_gate, moe_up, moe_down):
    batch, seq, d = x.shape
    n = batch * seq
    h = d // HEAD_DIM
    n_experts = w_router.shape[-1]
    assert d % HEAD_DIM == 0 and seq % GRID_W == 0 and seq // GRID_W >= WIN_ROWS
    assert meta_tokens.shape[0] == N_META and rel_bias.shape[1:] == (h, 2 * WIN_ROWS - 1, 2 * WIN_COLS - 1)
    assert all(p.shape[0] == 1 for p in (attn_norm, conv_norm, dense_norm, moe_norm)), "two-layer trunk only"

    xt = x.reshape(n, d)
    mt = jnp.broadcast_to(meta_tokens[None].astype(x.dtype), (batch, N_META, d)).reshape(batch * N_META, d)
    nm = batch * N_META

    g_attn = attn_norm[0].reshape(1, d)
    wqkv = w_qkv[0].astype(BF16)
    head_gain = jnp.concatenate([jnp.tile(q_gain[0] * (HEAD_DIM ** -0.5), h), jnp.tile(k_gain[0], h),
                                 jnp.ones((d,), F32)]).reshape(1, 3 * d)
    qkv = _qkv(xt, g_attn, wqkv, head_gain, bm=1024, bn=2048)
    qkv_m = _qkv(mt, g_attn, wqkv, head_gain, bm=nm, bn=2048)

    def meta_heads(part):
        t = qkv_m[:, part * d:(part + 1) * d].reshape(batch, N_META, h, HEAD_DIM).transpose(0, 2, 1, 3)
        return jnp.pad(t, ((0, 0), (0, 0), (0, LANES - N_META), (0, 0)))

    bias = _attention_bias(rel_bias[0], meta_bias[0])
    n_e, _, d_e = moe_gate.shape[1:]
    o, moe_gate_bf = _attention(qkv, meta_heads(1), meta_heads(2), bias, batch=batch, seq=seq, rb=16,
                                rider=moe_gate[0].reshape(n_e * d, d_e))
    o_m = _meta_attention(qkv_m, meta_bias[0], batch=batch)
    wo = w_o[0].astype(BF16)
    h1 = _proj_residual(o, wo, xt, bm=512)
    h1_m = _proj_residual(o_m, wo, mt, bm=nm)

    g_dense = dense_norm[0].reshape(1, d)
    wg, wu, wd = w_gate[0].astype(BF16), w_up[0].astype(BF16), w_down[0].astype(BF16)
    h2, moe_up_bf = _dense_ffn(h1, g_dense, wg, wu, wd, bm=512, bf=FFN_BLOCK_F, rider=moe_up[0].reshape(n_e * d, d_e))
    (h2_m,) = _dense_ffn(h1_m, g_dense, wg, wu, wd, bm=nm, bf=FFN_BLOCK_F)

    g_conv = conv_norm[0].reshape(1, d)
    win = w_in[0].astype(BF16)
    bg, z, moe_down_bf = _conv_in_proj(h2, g_conv, win, bm=1024, bn=512, rider=moe_down[0].reshape(n_e * d_e, d))
    _, z_m = _conv_in_proj(h2_m, g_conv, win, bm=nm, bn=512)
    z_meta_last = z_m.reshape(batch, N_META, d)[:, N_META - 1:, :].astype(F32)
    g_moe = moe_norm[0].reshape(1, d)
    wr_pad = jnp.pad(w_router[0], ((0, 0), (0, LANES - n_experts)))
    wr_hi = wr_pad.astype(BF16)
    wr_lo = (wr_pad - wr_hi.astype(F32)).astype(BF16)
    h3, route = _conv_proj_residual(bg, z, z_meta_last, conv_w[0].T, w_out[0].astype(BF16), h2, g_moe, wr_hi, wr_lo,
                                    seq=seq, bm=512, n_experts=n_experts)

    bm_e = 512
    block_expert, n_real, buf_tok, p0, p1 = _route_plan(route, n_experts=n_experts, bm=bm_e)
    ybuf = _expert_ffn(block_expert, n_real, buf_tok, h3, g_moe, moe_gate_bf.reshape(n_e, d, d_e),
                       moe_up_bf.reshape(n_e, d, d_e), moe_down_bf.reshape(n_e, d_e, d), bm=bm_e, bf=FFN_BLOCK_F)
    out = _combine(p0, p1, h3, route, ybuf, tm=256)
    return out.reshape(batch, seq, d)
```

```python
import functools

import numpy as np
import jax
import jax.numpy as jnp
from jax import lax
from jax.experimental import pallas as pl
from jax.experimental.pallas import tpu as pltpu

HEAD_DIM = 128
GRID_W = 64
WIN_ROWS = 8
WIN_COLS = 16
N_META = 16
TOP_K = 2
RMS_EPS = 1e-6
LANES = 128
MXU_WIDTH = 256
F32_SUBLANES = 8
BF16_SUBLANES = 16
VMEM_LIMIT_BYTES = 56 * 1024 * 1024
MASK_VALUE = -1e30
N_LOC = WIN_ROWS * GRID_W
ATTN_ROW_GROUP = 16
DMA_LOOP_UNROLL = 8
CONV_ROW_CHUNKS = 2

QKV_BLOCK = (1024, 2048)
ATTN_ROWS_PER_STEP = 32
PROJ_BLOCK_M = 512
FFN_BLOCK_M = 512
FFN_BLOCK_F = 1024
CONV_IN_BLOCK = (1024, 512)
COMBINE_BLOCK_M = 256

F32 = jnp.float32
BF16 = jnp.bfloat16


def _params(*sem):
    return pltpu.CompilerParams(dimension_semantics=sem, vmem_limit_bytes=VMEM_LIMIT_BYTES)


def _rms(xf, g):
    return xf * lax.rsqrt(jnp.mean(xf * xf, axis=-1, keepdims=True) + RMS_EPS) * g


def _dot(a, b):
    return jnp.dot(a, b, preferred_element_type=F32)


def _dot_nt(a, b):
    return lax.dot_general(a, b, (((1,), (1,)), ((), ())), preferred_element_type=F32)


def _with_cast_rider(body, n_in, n_out):
    def wrapped(*refs):
        rider_src, rider_dst = refs[n_in], refs[n_in + 1 + n_out]
        rider_dst[...] = rider_src[...].astype(rider_dst.dtype)
        body(*refs[:n_in], *refs[n_in + 1:n_in + 1 + n_out], *refs[n_in + 2 + n_out:])
    return wrapped


def _hosted_call(body, *, rider, rider_blocks, rider_index, in_specs, out_specs, out_shape, args, **kwargs):
    if rider is not None:
        rows, cols = rider.shape
        nr, nc = rider_blocks
        assert rows % (nr * BF16_SUBLANES) == 0 and cols % (nc * LANES) == 0, (rider.shape, rider_blocks)
        spec = pl.BlockSpec((rows // nr, cols // nc), rider_index)
        body = _with_cast_rider(body, len(in_specs), len(out_specs))
        in_specs, out_specs = in_specs + [spec], out_specs + [spec]
        out_shape = out_shape + [jax.ShapeDtypeStruct(rider.shape, BF16)]
        args = args + [rider]
    return pl.pallas_call(body, in_specs=in_specs, out_specs=out_specs, out_shape=out_shape, **kwargs)(*args)


def _qkv_kernel(x_ref, g_ref, w_ref, hg_ref, o_ref, xn_ref, *, n_norm_blocks):
    j = pl.program_id(1)

    @pl.when(j == 0)
    def _():
        xn_ref[...] = _rms(x_ref[...], g_ref[...]).astype(BF16)

    normed = j < n_norm_blocks
    xn = xn_ref[...]
    for c in range(o_ref.shape[1] // MXU_WIDTH):
        acc = _dot(xn, w_ref[:, c * MXU_WIDTH:(c + 1) * MXU_WIDTH])
        for hh in range(MXU_WIDTH // HEAD_DIM):
            sl = slice(c * MXU_WIDTH + hh * HEAD_DIM, c * MXU_WIDTH + (hh + 1) * HEAD_DIM)
            t = acc[:, hh * HEAD_DIM:(hh + 1) * HEAD_DIM]
            o_ref[:, sl] = jnp.where(normed, _rms(t, hg_ref[:, sl]), t).astype(BF16)


def _qkv(x, g, w, hg, *, bm, bn):
    m, d = x.shape
    n = w.shape[1]
    bm, bn = min(bm, m), min(bn, d)
    return pl.pallas_call(
        functools.partial(_qkv_kernel, n_norm_blocks=2 * d // bn),
        grid=(m // bm, n // bn),
        in_specs=[
            pl.BlockSpec((bm, d), lambda i, j: (i, 0)),
            pl.BlockSpec((1, d), lambda i, j: (0, 0)),
            pl.BlockSpec((d, bn), lambda i, j: (0, j)),
            pl.BlockSpec((1, bn), lambda i, j: (0, j)),
        ],
        out_specs=pl.BlockSpec((bm, bn), lambda i, j: (i, j)),
        out_shape=jax.ShapeDtypeStruct((m, n), BF16),
        scratch_shapes=[pltpu.VMEM((bm, d), BF16)],
        compiler_params=_params("arbitrary", "arbitrary"),
        name="qkv_proj",
    )(x, g, w, hg)


def _attn_kernel(q_ref, k_ref, v_ref, km_ref, vm_ref, bias_ref, o_ref, *, rb, n_rows):
    rblk = pl.program_id(2)
    km = km_ref[0, 0]
    vm = vm_ref[0, 0]

    def group(gi, carry):
        scores, probs = [], []
        for u in range(ATTN_ROW_GROUP):
            rr = gi * ATTN_ROW_GROUP + u
            r = rblk * rb + rr
            r0 = jnp.clip(r - WIN_ROWS // 2, 0, n_rows - WIN_ROWS)
            start = pl.multiple_of(r0 * GRID_W, GRID_W)
            qs = pl.multiple_of(rr * GRID_W, GRID_W)
            q = q_ref[pl.ds(qs, GRID_W), :]
            kw = k_ref[pl.ds(start, N_LOC), :]
            s = jnp.concatenate([_dot_nt(q, kw), _dot_nt(q, km)], axis=-1) + bias_ref[r - r0, 0]
            scores.append((qs, start, s))
        for qs, start, s in scores:
            p = jnp.exp(s - jnp.max(s, axis=-1, keepdims=True))
            probs.append((qs, start, p.astype(BF16), jnp.sum(p, axis=-1, keepdims=True)))
        for qs, start, pb, l in probs:
            vw = v_ref[pl.ds(start, N_LOC), :]
            o = _dot(pb[:, :N_LOC], vw) + _dot(pb[:, N_LOC:], vm)
            o_ref[pl.ds(qs, GRID_W), :] = (o / l).astype(BF16)
        return carry

    lax.fori_loop(0, rb // ATTN_ROW_GROUP, group, 0)


def _attention(qkv, km, vm, bias, *, batch, seq, rb, rider=None):
    n, d3 = qkv.shape
    d = d3 // 3
    h = d // HEAD_DIM
    n_rows = seq // GRID_W
    rb = min(rb, n_rows)
    nrb = n_rows // rb
    kv_spec = lambda off: pl.BlockSpec((seq, HEAD_DIM), lambda b, hh, r: (b, off + hh))
    meta_spec = pl.BlockSpec((1, 1, LANES, HEAD_DIM), lambda b, hh, r: (b, hh, 0, 0))
    return _hosted_call(
        functools.partial(_attn_kernel, rb=rb, n_rows=n_rows),
        rider=rider, rider_blocks=(batch * h * nrb, 1), rider_index=lambda b, hh, r: ((b * h + hh) * nrb + r, 0),
        grid=(batch, h, nrb),
        in_specs=[
            pl.BlockSpec((rb * GRID_W, HEAD_DIM), lambda b, hh, r: (b * nrb + r, hh)),
            kv_spec(h),
            kv_spec(2 * h),
            meta_spec,
            meta_spec,
            pl.BlockSpec((WIN_ROWS, 1, GRID_W, N_LOC + LANES), lambda b, hh, r: (0, hh, 0, 0)),
        ],
        out_specs=[pl.BlockSpec((rb * GRID_W, HEAD_DIM), lambda b, hh, r: (b * nrb + r, hh))],
        out_shape=[jax.ShapeDtypeStruct((n, d), BF16)],
        args=[qkv, qkv, qkv, km, vm, bias],
        compiler_params=_params("arbitrary", "arbitrary", "arbitrary"),
        name="nbr_attention",
    )


def _meta_attn_kernel(q_ref, k_ref, v_ref, mb_ref, o_ref):
    s = _dot_nt(q_ref[...], k_ref[...]) + mb_ref[0]
    p = jnp.exp(s - jnp.max(s, axis=-1, keepdims=True))
    l = jnp.sum(p, axis=-1, keepdims=True)
    o_ref[...] = (_dot(p.astype(BF16), v_ref[...]) / l).astype(BF16)


def _meta_attention(qkv_m, meta_bias, *, batch):
    d = qkv_m.shape[1] // 3
    h = d // HEAD_DIM
    spec = lambda off: pl.BlockSpec((N_META, HEAD_DIM), lambda b, hh: (b, off + hh))
    return pl.pallas_call(
        _meta_attn_kernel,
        grid=(batch, h),
        in_specs=[spec(0), spec(h), spec(2 * h),
                  pl.BlockSpec((1, 1, N_META), lambda b, hh: (hh, 0, 0))],
        out_specs=spec(0),
        out_shape=jax.ShapeDtypeStruct((batch * N_META, d), BF16),
        compiler_params=_params("arbitrary", "arbitrary"),
        name="meta_attention",
    )(qkv_m, qkv_m, qkv_m, meta_bias.reshape(h, 1, N_META))


def _attention_bias(rel_bias, meta_bias):
    h = rel_bias.shape[0]
    cq = np.arange(GRID_W)
    c0 = np.clip(cq - WIN_COLS // 2, 0, GRID_W - WIN_COLS)
    ck = np.arange(GRID_W)
    in_win = (ck[None, :] >= c0[:, None]) & (ck[None, :] < c0[:, None] + WIN_COLS)
    dc = ck[None, :] - cq[:, None] + WIN_COLS - 1
    by_col = jnp.full((h, 2 * WIN_ROWS - 1, GRID_W, GRID_W), MASK_VALUE, F32)
    for j in range(2 * WIN_COLS - 1):
        by_col = jnp.where((in_win & (dc == j))[None, None], rel_bias[:, :, j, None, None], by_col)
    loc = jnp.stack([by_col[:, WIN_ROWS - 1 - off:2 * WIN_ROWS - 1 - off] for off in range(WIN_ROWS)])
    loc = loc.transpose(0, 1, 3, 2, 4).reshape(WIN_ROWS, h, GRID_W, N_LOC)
    met = jnp.concatenate([meta_bias, jnp.full((h, LANES - N_META), MASK_VALUE, F32)], axis=-1)
    met = jnp.broadcast_to(met[None, :, None, :], (WIN_ROWS, h, GRID_W, LANES))
    return jnp.concatenate([loc, met], axis=-1).astype(F32)


def _proj_kernel(a_ref, w_ref, r_ref, o_ref):
    o_ref[...] = r_ref[...] + _dot(a_ref[...], w_ref[...])


def _proj_residual(a, w, r, *, bm):
    m, d = r.shape
    bm = min(bm, m)
    row = pl.BlockSpec((bm, d), lambda i: (i, 0))
    return pl.pallas_call(
        _proj_kernel,
        grid=(m // bm,),
        in_specs=[row, pl.BlockSpec((d, d), lambda i: (0, 0)), row],
        out_specs=row,
        out_shape=jax.ShapeDtypeStruct((m, d), F32),
        compiler_params=_params("arbitrary"),
        name="attn_out_proj",
    )(a, w, r)


def _conv_proj_kernel(bg_ref, z_ref, zp_ref, zn_ref, zm_ref, cw_ref, w_ref, r_ref, gr_ref, wr_hi_ref, wr_lo_ref,
                      o_ref, route_ref, *, blocks_per_seq, n_experts):
    i = pl.program_id(0)
    z = z_ref[...].astype(F32)
    bm = z.shape[0]
    first = (i % blocks_per_seq) == 0
    last = (i % blocks_per_seq) == blocks_per_seq - 1
    prev_row = jnp.where(first, zm_ref[0], zp_ref[BF16_SUBLANES - 1:BF16_SUBLANES, :].astype(F32))
    next_row = jnp.where(last, 0.0, zn_ref[0:1, :].astype(F32))
    row = lax.broadcasted_iota(jnp.int32, z.shape, 0)
    z_prev = jnp.where(row == 0, prev_row, pltpu.roll(z, 1, axis=0))
    z_next = jnp.where(row == bm - 1, next_row, pltpu.roll(z, bm - 1, axis=0))
    y = z_prev * cw_ref[0:1, :] + z * cw_ref[1:2, :] + z_next * cw_ref[2:3, :]
    a = (bg_ref[...].astype(F32) * y).astype(BF16)
    for s in range(0, bm, bm // CONV_ROW_CHUNKS):
        rows = slice(s, s + bm // CONV_ROW_CHUNKS)
        h = r_ref[rows, :] + _dot(a[rows, :], w_ref[...])
        o_ref[rows, :] = h
        route_ref[rows, :] = _route(h, gr_ref[...], wr_hi_ref[...], wr_lo_ref[...], n_experts)


def _conv_proj_residual(bg, z, z_meta_last, conv_w_t, w, r, g_route, wr_hi, wr_lo, *, seq, bm, n_experts):
    m, d = r.shape
    bm = min(bm, seq)
    bps = seq // bm
    halo = bm // BF16_SUBLANES
    n_halo = m // BF16_SUBLANES
    row = pl.BlockSpec((bm, d), lambda i: (i, 0))
    const = lambda shape: pl.BlockSpec(shape, lambda i: (0,) * len(shape), pipeline_mode=pl.Buffered(1))
    return pl.pallas_call(
        functools.partial(_conv_proj_kernel, blocks_per_seq=bps, n_experts=n_experts),
        grid=(m // bm,),
        in_specs=[
            row, row,
            pl.BlockSpec((BF16_SUBLANES, d), lambda i: (jnp.maximum(i * halo - 1, 0), 0)),
            pl.BlockSpec((BF16_SUBLANES, d), lambda i: (jnp.minimum((i + 1) * halo, n_halo - 1), 0)),
            pl.BlockSpec((1, 1, d), lambda i: (i // bps, 0, 0)),
            const((3, d)),
            const((d, d)),
            row,
            const((1, d)),
            const((d, LANES)),
            const((d, LANES)),
        ],
        out_specs=[row, pl.BlockSpec((bm, LANES), lambda i: (i, 0))],
        out_shape=[jax.ShapeDtypeStruct((m, d), F32), jax.ShapeDtypeStruct((m, LANES), F32)],
        compiler_params=_params("arbitrary"),
        name="conv_out_proj",
    )(bg, z, z, z, z_meta_last, conv_w_t, w, r, g_route, wr_hi, wr_lo)


def _win_kernel(h_ref, g_ref, wb_ref, wc_ref, wu_ref, bg_ref, z_ref, xn_ref):
    @pl.when(pl.program_id(1) == 0)
    def _():
        xn_ref[...] = _rms(h_ref[...], g_ref[...]).astype(BF16)

    xn = xn_ref[...]
    bg_ref[...] = _dot(xn, wb_ref[...]).astype(BF16)
    z_ref[...] = (_dot(xn, wc_ref[...]) * _dot(xn, wu_ref[...])).astype(BF16)


def _conv_in_proj(h, g, w_in, *, bm, bn, rider=None):
    m, d = h.shape
    bm, bn = min(bm, m), min(bn, d)
    nb = d // bn
    wspec = lambda part: pl.BlockSpec((d, bn), lambda i, j: (0, part * nb + j))
    out = pl.BlockSpec((bm, bn), lambda i, j: (i, j))
    return _hosted_call(
        _win_kernel,
        rider=rider, rider_blocks=(m // bm, nb), rider_index=lambda i, j: (i, j),
        grid=(m // bm, nb),
        in_specs=[pl.BlockSpec((bm, d), lambda i, j: (i, 0)), pl.BlockSpec((1, d), lambda i, j: (0, 0)),
                  wspec(0), wspec(1), wspec(2)],
        out_specs=[out, out],
        out_shape=[jax.ShapeDtypeStruct((m, d), BF16), jax.ShapeDtypeStruct((m, d), BF16)],
        args=[h, g, w_in, w_in, w_in],
        scratch_shapes=[pltpu.VMEM((bm, d), BF16)],
        compiler_params=_params("arbitrary", "arbitrary"),
        name="conv_in_proj",
    )


def _ffn_kernel(h_ref, g_ref, wg_ref, wu_ref, wd_ref, o_ref, xn_ref):
    f = pl.program_id(1)

    @pl.when(f == 0)
    def _():
        xn_ref[...] = _rms(h_ref[...], g_ref[...]).astype(BF16)
        o_ref[...] = h_ref[...]

    xn = xn_ref[...]
    a = (jax.nn.silu(_dot(xn, wg_ref[...])) * _dot(xn, wu_ref[...])).astype(BF16)
    o_ref[...] += _dot(a, wd_ref[...])


def _dense_ffn(h, g, wg, wu, wd, *, bm, bf, rider=None):
    m, d = h.shape
    ff = wg.shape[1]
    bm, bf = min(bm, m), min(bf, ff)
    row = pl.BlockSpec((bm, d), lambda i, f: (i, 0))
    return _hosted_call(
        _ffn_kernel,
        rider=rider, rider_blocks=(m // bm, ff // bf), rider_index=lambda i, f: (i, f),
        grid=(m // bm, ff // bf),
        in_specs=[row, pl.BlockSpec((1, d), lambda i, f: (0, 0)),
                  pl.BlockSpec((d, bf), lambda i, f: (0, f)),
                  pl.BlockSpec((d, bf), lambda i, f: (0, f)),
                  pl.BlockSpec((bf, d), lambda i, f: (f, 0))],
        out_specs=[row],
        out_shape=[jax.ShapeDtypeStruct((m, d), F32)],
        args=[h, g, wg, wu, wd],
        scratch_shapes=[pltpu.VMEM((bm, d), BF16)],
        compiler_params=_params("arbitrary", "arbitrary"),
        name="dense_swiglu",
    )


def _route(h, g, wr_hi, wr_lo, n_experts):
    xn = _rms(h, g)
    x_hi = xn.astype(BF16)
    x_lo = (xn - x_hi.astype(F32)).astype(BF16)
    logits = _dot(x_hi, wr_hi) + (_dot(x_lo, wr_hi) + _dot(x_hi, wr_lo))
    lane = lax.broadcasted_iota(jnp.int32, logits.shape, 1)
    neg = -jnp.inf
    l1 = jnp.where(lane < n_experts, logits, neg)
    m1 = jnp.max(l1, axis=-1, keepdims=True)
    i1 = jnp.min(jnp.where(l1 == m1, lane, LANES), axis=-1, keepdims=True)
    l2 = jnp.where(lane == i1, neg, l1)
    m2 = jnp.max(l2, axis=-1, keepdims=True)
    i2 = jnp.min(jnp.where(l2 == m2, lane, LANES), axis=-1, keepdims=True)
    t = jnp.exp(m2 - m1)
    g1 = 1.0 / (1.0 + t)
    g2 = t / (1.0 + t)
    return jnp.where(lane == 0, i1.astype(F32),
                     jnp.where(lane == 1, i2.astype(F32),
                               jnp.where(lane == 2, g1, jnp.where(lane == 3, g2, 0.0))))


def _expert_kernel(be_ref, nreal_ref, tok_ref, tok_next_ref, h_ref, g_ref, wg_ref, wu_ref, wd_ref, o_ref,
                   xg_ref, xn_ref, sem, *, rows_per_step):
    i = pl.program_id(0)
    f = pl.program_id(1)
    nblk, nf = pl.num_programs(0), pl.num_programs(1)
    n_real = nreal_ref[0]
    bm = xn_ref.shape[0]
    n_gather = tok_ref.shape[2]

    def row_copy(tok, r):
        return pltpu.make_async_copy(h_ref.at[pl.ds(tok, 1)], xg_ref.at[pl.ds(r, 1)], sem)

    def wait_gather():
        def body(r, c):
            row_copy(0, r).wait()
            return c
        lax.fori_loop(0, n_gather, body, 0, unroll=DMA_LOOP_UNROLL)

    @pl.when(jnp.logical_and(i == 0, f == 0))
    def _():
        def body(r, c):
            row_copy(tok_ref[0, 0, r], r).start()
            return c
        lax.fori_loop(0, n_gather, body, 0, unroll=DMA_LOOP_UNROLL)

    @pl.when(jnp.logical_and(f == 0, i <= n_real))
    def _():
        wait_gather()

    @pl.when(i < n_real)
    def _():
        @pl.when(f == 0)
        def _():
            xn_ref[...] = _rms(xg_ref[:bm, :], g_ref[...]).astype(BF16)
            o_ref[...] = jnp.zeros_like(o_ref)

        for u in range(rows_per_step):
            r = f * rows_per_step + u
            row_copy(tok_next_ref[0, 0, r], r).start()

        xn = xn_ref[...]
        a = (jax.nn.silu(_dot(xn, wg_ref[0])) * _dot(xn, wu_ref[0])).astype(BF16)
        o_ref[...] += _dot(a, wd_ref[0])

        @pl.when(jnp.logical_and(i == nblk - 1, f == nf - 1))
        def _():
            wait_gather()

    @pl.when(jnp.logical_and(i >= n_real, f == nf - 1))
    def _():
        o_ref[...] = jnp.zeros_like(o_ref)


def _expert_ffn(block_expert, n_real, buf_tok, h, g, wg, wu, wd, *, bm, bf):
    d = h.shape[1]
    ff = wg.shape[2]
    bf = min(bf, ff)
    nblk, nf = buf_tok.shape[0] // bm, ff // bf
    rows_per_step = pl.cdiv(bm, nf)
    n_gather = rows_per_step * nf
    tok = jnp.pad(buf_tok.reshape(nblk, 1, bm), ((0, 0), (0, 0), (0, n_gather - bm)))

    def blk(i, nr):
        return jnp.minimum(i, nr[0] - 1)

    def fblk(i, f, nr):
        return jnp.where(i < nr[0], f, nf - 1)

    return pl.pallas_call(
        functools.partial(_expert_kernel, rows_per_step=rows_per_step),
        grid_spec=pltpu.PrefetchScalarGridSpec(
            num_scalar_prefetch=2,
            grid=(nblk, nf),
            in_specs=[
                pl.BlockSpec((1, 1, n_gather), lambda i, f, be, nr: (i, 0, 0), memory_space=pltpu.SMEM),
                pl.BlockSpec((1, 1, n_gather), lambda i, f, be, nr: (jnp.minimum(i + 1, nblk - 1), 0, 0),
                             memory_space=pltpu.SMEM),
                pl.BlockSpec(memory_space=pl.ANY),
                pl.BlockSpec((1, d), lambda i, f, be, nr: (0, 0)),
                pl.BlockSpec((1, d, bf), lambda i, f, be, nr: (be[blk(i, nr)], 0, fblk(i, f, nr))),
                pl.BlockSpec((1, d, bf), lambda i, f, be, nr: (be[blk(i, nr)], 0, fblk(i, f, nr))),
                pl.BlockSpec((1, bf, d), lambda i, f, be, nr: (be[blk(i, nr)], fblk(i, f, nr), 0)),
            ],
            out_specs=pl.BlockSpec((bm, d), lambda i, f, be, nr: (i, 0)),
            scratch_shapes=[pltpu.VMEM((pl.cdiv(n_gather, F32_SUBLANES) * F32_SUBLANES, d), F32), pltpu.VMEM((bm, d), BF16),
                            pltpu.SemaphoreType.DMA(())],
        ),
        out_shape=jax.ShapeDtypeStruct((nblk * bm, d), F32),
        compiler_params=_params("arbitrary", "arbitrary"),
        name="moe_experts",
    )(block_expert, n_real, tok, tok, h, g, wg, wu, wd)


def _combine_kernel(p0_ref, p1_ref, p0_next_ref, p1_next_ref, h_ref, route_ref, y_ref, o_ref, ybuf, sem, *, tm):
    i = pl.program_id(0)
    slot = i % 2

    def row_copy(src_row, s, k, r):
        return pltpu.make_async_copy(y_ref.at[pl.ds(src_row, 1)], ybuf.at[s, k, pl.ds(r, 1)], sem.at[s])

    def start_gather(a_ref, b_ref, s):
        def body(r, c):
            row_copy(a_ref[0, 0, r], s, 0, r).start()
            row_copy(b_ref[0, 0, r], s, 1, r).start()
            return c
        lax.fori_loop(0, tm, body, 0, unroll=DMA_LOOP_UNROLL)

    def wait_gather(s):
        def body(r, c):
            row_copy(0, s, 0, r).wait()
            row_copy(0, s, 1, r).wait()
            return c
        lax.fori_loop(0, tm, body, 0, unroll=DMA_LOOP_UNROLL)

    @pl.when(i == 0)
    def _():
        start_gather(p0_ref, p1_ref, 0)

    @pl.when(i + 1 < pl.num_programs(0))
    def _():
        start_gather(p0_next_ref, p1_next_ref, 1 - slot)

    wait_gather(slot)
    g0 = route_ref[:, TOP_K:TOP_K + 1]
    g1 = route_ref[:, TOP_K + 1:TOP_K + 2]
    o_ref[...] = h_ref[...] + (ybuf[slot, 0] * g0 + ybuf[slot, 1] * g1)


def _combine(p0, p1, h, route, y, *, tm):
    m, d = h.shape
    tm = min(tm, m)
    nt = m // tm
    idx = pl.BlockSpec((1, 1, tm), lambda i: (i, 0, 0), memory_space=pltpu.SMEM)
    idx_next = pl.BlockSpec((1, 1, tm), lambda i: (jnp.minimum(i + 1, nt - 1), 0, 0), memory_space=pltpu.SMEM)
    row = pl.BlockSpec((tm, d), lambda i: (i, 0))
    p0, p1 = p0.reshape(nt, 1, tm), p1.reshape(nt, 1, tm)
    return pl.pallas_call(
        functools.partial(_combine_kernel, tm=tm),
        grid=(nt,),
        in_specs=[idx, idx, idx_next, idx_next, row, pl.BlockSpec((tm, LANES), lambda i: (i, 0)),
                  pl.BlockSpec(memory_space=pl.ANY)],
        out_specs=row,
        out_shape=jax.ShapeDtypeStruct((m, d), F32),
        scratch_shapes=[pltpu.VMEM((2, 2, tm, d), F32), pltpu.SemaphoreType.DMA((2,))],
        compiler_params=_params("arbitrary"),
        name="moe_combine",
    )(p0, p1, p0, p1, h, route, y)


def _route_plan(route, *, n_experts, bm):
    n = route.shape[0]
    flat_e = route[:, :TOP_K].astype(jnp.int32).reshape(-1)
    onehot = (flat_e[:, None] == jnp.arange(n_experts, dtype=jnp.int32)[None, :]).astype(jnp.int32)
    csum = jnp.cumsum(onehot, axis=0)
    rank = jnp.take_along_axis(csum, flat_e[:, None], axis=1)[:, 0] - 1
    counts = csum[-1]
    blocks = (counts + bm - 1) // bm
    blk_end = jnp.cumsum(blocks)
    blk_start = blk_end - blocks
    rows = blk_start[flat_e] * bm + rank
    nblk = -(-(n * TOP_K) // bm) + n_experts
    block_expert = jnp.minimum(jnp.searchsorted(blk_end, jnp.arange(nblk), side="right"),
                               n_experts - 1).astype(jnp.int32)
    buf_tok = jnp.zeros((nblk * bm,), jnp.int32).at[rows].set(jnp.arange(n * TOP_K, dtype=jnp.int32) // TOP_K)
    n_real = blk_end[-1:].astype(jnp.int32)
    return block_expert, n_real, buf_tok, rows[0::TOP_K], rows[1::TOP_K]


def kernel(x, meta_tokens, attn_norm, w_qkv, q_gain, k_gain, rel_bias, meta_bias, w_o, conv_norm, w_in, conv_w,
           w_out, dense_norm, w_gate, w_up, w_down, moe_norm, w_router, moe_gate, moe_up, moe_down):
    batch, seq, d = x.shape
    n = batch * seq
    h = d // HEAD_DIM
    n_experts = w_router.shape[-1]
    assert d % HEAD_DIM == 0 and seq % GRID_W == 0 and seq // GRID_W >= WIN_ROWS
    assert meta_tokens.shape[0] == N_META and rel_bias.shape[1:] == (h, 2 * WIN_ROWS - 1, 2 * WIN_COLS - 1)
    assert all(p.shape[0] == 1 for p in (attn_norm, conv_norm, dense_norm, moe_norm)), "two-layer trunk only"

    xt = x.reshape(n, d)
    mt = jnp.broadcast_to(meta_tokens[None].astype(x.dtype), (batch, N_META, d)).reshape(batch * N_META, d)
    nm = batch * N_META

    g_attn = attn_norm[0].reshape(1, d)
    wqkv = w_qkv[0].astype(BF16)
    head_gain = jnp.concatenate([jnp.tile(q_gain[0] * (HEAD_DIM ** -0.5), h), jnp.tile(k_gain[0], h),
                                 jnp.ones((d,), F32)]).reshape(1, 3 * d)
    qkv = _qkv(xt, g_attn, wqkv, head_gain, bm=QKV_BLOCK[0], bn=QKV_BLOCK[1])
    qkv_m = _qkv(mt, g_attn, wqkv, head_gain, bm=nm, bn=QKV_BLOCK[1])

    def meta_heads(part):
        t = qkv_m[:, part * d:(part + 1) * d].reshape(batch, N_META, h, HEAD_DIM).transpose(0, 2, 1, 3)
        return jnp.pad(t, ((0, 0), (0, 0), (0, LANES - N_META), (0, 0)))

    bias = _attention_bias(rel_bias[0], meta_bias[0])
    n_e, _, d_e = moe_gate.shape[1:]
    o, moe_gate_bf = _attention(qkv, meta_heads(1), meta_heads(2), bias, batch=batch, seq=seq, rb=ATTN_ROWS_PER_STEP,
                                rider=moe_gate[0].reshape(n_e * d, d_e))
    o_m = _meta_attention(qkv_m, meta_bias[0], batch=batch)
    wo = w_o[0].astype(BF16)
    h1 = _proj_residual(o, wo, xt, bm=PROJ_BLOCK_M)
    h1_m = _proj_residual(o_m, wo, mt, bm=nm)

    g_dense = dense_norm[0].reshape(1, d)
    wg, wu, wd = w_gate[0].astype(BF16), w_up[0].astype(BF16), w_down[0].astype(BF16)
    h2, moe_up_bf = _dense_ffn(h1, g_dense, wg, wu, wd, bm=FFN_BLOCK_M, bf=FFN_BLOCK_F,
                               rider=moe_up[0].reshape(n_e * d, d_e))
    (h2_m,) = _dense_ffn(h1_m, g_dense, wg, wu, wd, bm=nm, bf=FFN_BLOCK_F)

    g_conv = conv_norm[0].reshape(1, d)
    win = w_in[0].astype(BF16)
    bg, z, moe_down_bf = _conv_in_proj(h2, g_conv, win, bm=CONV_IN_BLOCK[0], bn=CONV_IN_BLOCK[1],
                                       rider=moe_down[0].reshape(n_e * d_e, d))
    _, z_m = _conv_in_proj(h2_m, g_conv, win, bm=nm, bn=CONV_IN_BLOCK[1])
    z_meta_last = z_m.reshape(batch, N_META, d)[:, N_META - 1:, :].astype(F32)
    g_moe = moe_norm[0].reshape(1, d)
    wr_pad = jnp.pad(w_router[0], ((0, 0), (0, LANES - n_experts)))
    wr_hi = wr_pad.astype(BF16)
    wr_lo = (wr_pad - wr_hi.astype(F32)).astype(BF16)
    h3, route = _conv_proj_residual(bg, z, z_meta_last, conv_w[0].T, w_out[0].astype(BF16), h2, g_moe, wr_hi, wr_lo,
                                    seq=seq, bm=PROJ_BLOCK_M, n_experts=n_experts)

    block_expert, n_real, buf_tok, p0, p1 = _route_plan(route, n_experts=n_experts, bm=FFN_BLOCK_M)
    ybuf = _expert_ffn(block_expert, n_real, buf_tok, h3, g_moe, moe_gate_bf.reshape(n_e, d, d_e),
                       moe_up_bf.reshape(n_e, d, d_e), moe_down_bf.reshape(n_e, d_e, d),
                       bm=FFN_BLOCK_M, bf=FFN_BLOCK_F)
    out = _combine(p0, p1, h3, route, ybuf, tm=COMBINE_BLOCK_M)
    return out.reshape(batch, seq, d)
```

```python
import functools

import numpy as np
import jax
import jax.numpy as jnp
from jax import lax
from jax.experimental import pallas as pl
from jax.experimental.pallas import tpu as pltpu

HEAD_DIM = 128
GRID_W = 64
WIN_ROWS = 8
WIN_COLS = 16
N_META = 16
TOP_K = 2
RMS_EPS = 1e-6
LANES = 128
MXU_WIDTH = 256
F32_SUBLANES = 8
BF16_SUBLANES = 16
VMEM_LIMIT_BYTES = 56 * 1024 * 1024
MASK_VALUE = -1e30
N_LOC = WIN_ROWS * GRID_W
ATTN_ROW_GROUP = 16
DMA_LOOP_UNROLL = 8
CONV_ROW_CHUNKS = 2

QKV_BLOCK = (1024, 2048)
ATTN_ROWS_PER_STEP = 64
PROJ_BLOCK_M = 512
FFN_BLOCK_M = 512
FFN_BLOCK_F = 1024
CONV_IN_BLOCK = (1024, 512)
COMBINE_BLOCK_M = 512

F32 = jnp.float32
BF16 = jnp.bfloat16


def _params(*sem):
    return pltpu.CompilerParams(dimension_semantics=sem, vmem_limit_bytes=VMEM_LIMIT_BYTES)


def _rms(xf, g):
    return xf * lax.rsqrt(jnp.mean(xf * xf, axis=-1, keepdims=True) + RMS_EPS) * g


def _dot(a, b):
    return jnp.dot(a, b, preferred_element_type=F32)


def _dot_nt(a, b):
    return lax.dot_general(a, b, (((1,), (1,)), ((), ())), preferred_element_type=F32)


def _with_cast_rider(body, n_in, n_out):
    def wrapped(*refs):
        rider_src, rider_dst = refs[n_in], refs[n_in + 1 + n_out]
        rider_dst[...] = rider_src[...].astype(rider_dst.dtype)
        body(*refs[:n_in], *refs[n_in + 1:n_in + 1 + n_out], *refs[n_in + 2 + n_out:])
    return wrapped


def _hosted_call(body, *, rider, rider_blocks, rider_index, in_specs, out_specs, out_shape, args, **kwargs):
    if rider is not None:
        rows, cols = rider.shape
        nr, nc = rider_blocks
        assert rows % (nr * BF16_SUBLANES) == 0 and cols % (nc * LANES) == 0, (rider.shape, rider_blocks)
        spec = pl.BlockSpec((rows // nr, cols // nc), rider_index)
        body = _with_cast_rider(body, len(in_specs), len(out_specs))
        in_specs, out_specs = in_specs + [spec], out_specs + [spec]
        out_shape = out_shape + [jax.ShapeDtypeStruct(rider.shape, BF16)]
        args = args + [rider]
    return pl.pallas_call(body, in_specs=in_specs, out_specs=out_specs, out_shape=out_shape, **kwargs)(*args)


def _qkv_kernel(x_ref, g_ref, w_ref, hg_ref, o_ref, xn_ref, *, n_norm_blocks):
    j = pl.program_id(1)

    @pl.when(j == 0)
    def _():
        xn_ref[...] = _rms(x_ref[...], g_ref[...]).astype(BF16)

    normed = j < n_norm_blocks
    xn = xn_ref[...]
    for c in range(o_ref.shape[1] // MXU_WIDTH):
        acc = _dot(xn, w_ref[:, c * MXU_WIDTH:(c + 1) * MXU_WIDTH])
        for hh in range(MXU_WIDTH // HEAD_DIM):
            sl = slice(c * MXU_WIDTH + hh * HEAD_DIM, c * MXU_WIDTH + (hh + 1) * HEAD_DIM)
            t = acc[:, hh * HEAD_DIM:(hh + 1) * HEAD_DIM]
            o_ref[:, sl] = jnp.where(normed, _rms(t, hg_ref[:, sl]), t).astype(BF16)


def _qkv(x, g, w, hg, *, bm, bn):
    m, d = x.shape
    n = w.shape[1]
    bm, bn = min(bm, m), min(bn, d)
    return pl.pallas_call(
        functools.partial(_qkv_kernel, n_norm_blocks=2 * d // bn),
        grid=(m // bm, n // bn),
        in_specs=[
            pl.BlockSpec((bm, d), lambda i, j: (i, 0)),
            pl.BlockSpec((1, d), lambda i, j: (0, 0)),
            pl.BlockSpec((d, bn), lambda i, j: (0, j)),
            pl.BlockSpec((1, bn), lambda i, j: (0, j)),
        ],
        out_specs=pl.BlockSpec((bm, bn), lambda i, j: (i, j)),
        out_shape=jax.ShapeDtypeStruct((m, n), BF16),
        scratch_shapes=[pltpu.VMEM((bm, d), BF16)],
        compiler_params=_params("arbitrary", "arbitrary"),
        name="qkv_proj",
    )(x, g, w, hg)


def _attn_kernel(q_ref, k_ref, v_ref, km_ref, vm_ref, bias_ref, o_ref, *, rb, n_rows):
    rblk = pl.program_id(2)
    km = km_ref[0, 0]
    vm = vm_ref[0, 0]

    def group(gi, carry):
        scores, probs = [], []
        for u in range(ATTN_ROW_GROUP):
            rr = gi * ATTN_ROW_GROUP + u
            r = rblk * rb + rr
            r0 = jnp.clip(r - WIN_ROWS // 2, 0, n_rows - WIN_ROWS)
            start = pl.multiple_of(r0 * GRID_W, GRID_W)
            qs = pl.multiple_of(rr * GRID_W, GRID_W)
            q = q_ref[pl.ds(qs, GRID_W), :]
            kw = k_ref[pl.ds(start, N_LOC), :]
            s = jnp.concatenate([_dot_nt(q, kw), _dot_nt(q, km)], axis=-1) + bias_ref[r - r0, 0]
            scores.append((qs, start, s))
        for qs, start, s in scores:
            p = jnp.exp(s - jnp.max(s, axis=-1, keepdims=True))
            probs.append((qs, start, p.astype(BF16), jnp.sum(p, axis=-1, keepdims=True)))
        for qs, start, pb, l in probs:
            vw = v_ref[pl.ds(start, N_LOC), :]
            o = _dot(pb[:, :N_LOC], vw) + _dot(pb[:, N_LOC:], vm)
            o_ref[pl.ds(qs, GRID_W), :] = (o / l).astype(BF16)
        return carry

    lax.fori_loop(0, rb // ATTN_ROW_GROUP, group, 0)


def _attention(qkv, km, vm, bias, *, batch, seq, rb, rider=None):
    n, d3 = qkv.shape
    d = d3 // 3
    h = d // HEAD_DIM
    n_rows = seq // GRID_W
    rb = min(rb, n_rows)
    nrb = n_rows // rb
    kv_spec = lambda off: pl.BlockSpec((seq, HEAD_DIM), lambda b, hh, r: (b, off + hh))
    meta_spec = pl.BlockSpec((1, 1, LANES, HEAD_DIM), lambda b, hh, r: (b, hh, 0, 0))
    return _hosted_call(
        functools.partial(_attn_kernel, rb=rb, n_rows=n_rows),
        rider=rider, rider_blocks=(batch * h * nrb, 1), rider_index=lambda b, hh, r: ((b * h + hh) * nrb + r, 0),
        grid=(batch, h, nrb),
        in_specs=[
            pl.BlockSpec((rb * GRID_W, HEAD_DIM), lambda b, hh, r: (b * nrb + r, hh)),
            kv_spec(h),
            kv_spec(2 * h),
            meta_spec,
            meta_spec,
            pl.BlockSpec((WIN_ROWS, 1, GRID_W, N_LOC + LANES), lambda b, hh, r: (0, hh, 0, 0)),
        ],
        out_specs=[pl.BlockSpec((rb * GRID_W, HEAD_DIM), lambda b, hh, r: (b * nrb + r, hh))],
        out_shape=[jax.ShapeDtypeStruct((n, d), BF16)],
        args=[qkv, qkv, qkv, km, vm, bias],
        compiler_params=_params("arbitrary", "arbitrary", "arbitrary"),
        name="nbr_attention",
    )


def _meta_attn_kernel(q_ref, k_ref, v_ref, mb_ref, o_ref):
    s = _dot_nt(q_ref[...], k_ref[...]) + mb_ref[0]
    p = jnp.exp(s - jnp.max(s, axis=-1, keepdims=True))
    l = jnp.sum(p, axis=-1, keepdims=True)
    o_ref[...] = (_dot(p.astype(BF16), v_ref[...]) / l).astype(BF16)


def _meta_attention(qkv_m, meta_bias, *, batch):
    d = qkv_m.shape[1] // 3
    h = d // HEAD_DIM
    spec = lambda off: pl.BlockSpec((N_META, HEAD_DIM), lambda b, hh: (b, off + hh))
    return pl.pallas_call(
        _meta_attn_kernel,
        grid=(batch, h),
        in_specs=[spec(0), spec(h), spec(2 * h),
                  pl.BlockSpec((1, 1, N_META), lambda b, hh: (hh, 0, 0))],
        out_specs=spec(0),
        out_shape=jax.ShapeDtypeStruct((batch * N_META, d), BF16),
        compiler_params=_params("arbitrary", "arbitrary"),
        name="meta_attention",
    )(qkv_m, qkv_m, qkv_m, meta_bias.reshape(h, 1, N_META))


def _attention_bias(rel_bias, meta_bias):
    h = rel_bias.shape[0]
    cq = np.arange(GRID_W)
    c0 = np.clip(cq - WIN_COLS // 2, 0, GRID_W - WIN_COLS)
    ck = np.arange(GRID_W)
    in_win = (ck[None, :] >= c0[:, None]) & (ck[None, :] < c0[:, None] + WIN_COLS)
    dc = ck[None, :] - cq[:, None] + WIN_COLS - 1
    by_col = jnp.full((h, 2 * WIN_ROWS - 1, GRID_W, GRID_W), MASK_VALUE, F32)
    for j in range(2 * WIN_COLS - 1):
        by_col = jnp.where((in_win & (dc == j))[None, None], rel_bias[:, :, j, None, None], by_col)
    loc = jnp.stack([by_col[:, WIN_ROWS - 1 - off:2 * WIN_ROWS - 1 - off] for off in range(WIN_ROWS)])
    loc = loc.transpose(0, 1, 3, 2, 4).reshape(WIN_ROWS, h, GRID_W, N_LOC)
    met = jnp.concatenate([meta_bias, jnp.full((h, LANES - N_META), MASK_VALUE, F32)], axis=-1)
    met = jnp.broadcast_to(met[None, :, None, :], (WIN_ROWS, h, GRID_W, LANES))
    return jnp.concatenate([loc, met], axis=-1).astype(F32)


def _proj_kernel(a_ref, w_ref, r_ref, o_ref):
    o_ref[...] = r_ref[...] + _dot(a_ref[...], w_ref[...])


def _proj_residual(a, w, r, *, bm):
    m, d = r.shape
    bm = min(bm, m)
    row = pl.BlockSpec((bm, d), lambda i: (i, 0))
    return pl.pallas_call(
        _proj_kernel,
        grid=(m // bm,),
        in_specs=[row, pl.BlockSpec((d, d), lambda i: (0, 0)), row],
        out_specs=row,
        out_shape=jax.ShapeDtypeStruct((m, d), F32),
        compiler_params=_params("arbitrary"),
        name="attn_out_proj",
    )(a, w, r)


def _conv_proj_kernel(bg_ref, z_ref, zp_ref, zn_ref, zm_ref, cw_ref, w_ref, r_ref, gr_ref, wr_hi_ref, wr_lo_ref,
                      o_ref, route_ref, *, blocks_per_seq, n_experts):
    i = pl.program_id(0)
    z = z_ref[...].astype(F32)
    bm = z.shape[0]
    first = (i % blocks_per_seq) == 0
    last = (i % blocks_per_seq) == blocks_per_seq - 1
    prev_row = jnp.where(first, zm_ref[0], zp_ref[BF16_SUBLANES - 1:BF16_SUBLANES, :].astype(F32))
    next_row = jnp.where(last, 0.0, zn_ref[0:1, :].astype(F32))
    row = lax.broadcasted_iota(jnp.int32, z.shape, 0)
    z_prev = jnp.where(row == 0, prev_row, pltpu.roll(z, 1, axis=0))
    z_next = jnp.where(row == bm - 1, next_row, pltpu.roll(z, bm - 1, axis=0))
    y = z_prev * cw_ref[0:1, :] + z * cw_ref[1:2, :] + z_next * cw_ref[2:3, :]
    a = (bg_ref[...].astype(F32) * y).astype(BF16)
    for s in range(0, bm, bm // CONV_ROW_CHUNKS):
        rows = slice(s, s + bm // CONV_ROW_CHUNKS)
        h = r_ref[rows, :] + _dot(a[rows, :], w_ref[...])
        o_ref[rows, :] = h
        route_ref[rows, :] = _route(h, gr_ref[...], wr_hi_ref[...], wr_lo_ref[...], n_experts)


def _conv_proj_residual(bg, z, z_meta_last, conv_w_t, w, r, g_route, wr_hi, wr_lo, *, seq, bm, n_experts):
    m, d = r.shape
    bm = min(bm, seq)
    bps = seq // bm
    halo = bm // BF16_SUBLANES
    n_halo = m // BF16_SUBLANES
    row = pl.BlockSpec((bm, d), lambda i: (i, 0))
    const = lambda shape: pl.BlockSpec(shape, lambda i: (0,) * len(shape), pipeline_mode=pl.Buffered(1))
    return pl.pallas_call(
        functools.partial(_conv_proj_kernel, blocks_per_seq=bps, n_experts=n_experts),
        grid=(m // bm,),
        in_specs=[
            row, row,
            pl.BlockSpec((BF16_SUBLANES, d), lambda i: (jnp.maximum(i * halo - 1, 0), 0)),
            pl.BlockSpec((BF16_SUBLANES, d), lambda i: (jnp.minimum((i + 1) * halo, n_halo - 1), 0)),
            pl.BlockSpec((1, 1, d), lambda i: (i // bps, 0, 0)),
            const((3, d)),
            const((d, d)),
            row,
            const((1, d)),
            const((d, LANES)),
            const((d, LANES)),
        ],
        out_specs=[row, pl.BlockSpec((bm, LANES), lambda i: (i, 0))],
        out_shape=[jax.ShapeDtypeStruct((m, d), F32), jax.ShapeDtypeStruct((m, LANES), F32)],
        compiler_params=_params("arbitrary"),
        name="conv_out_proj",
    )(bg, z, z, z, z_meta_last, conv_w_t, w, r, g_route, wr_hi, wr_lo)


def _win_kernel(h_ref, g_ref, wb_ref, wc_ref, wu_ref, bg_ref, z_ref, xn_ref):
    @pl.when(pl.program_id(1) == 0)
    def _():
        xn_ref[...] = _rms(h_ref[...], g_ref[...]).astype(BF16)

    xn = xn_ref[...]
    bg_ref[...] = _dot(xn, wb_ref[...]).astype(BF16)
    z_ref[...] = (_dot(xn, wc_ref[...]) * _dot(xn, wu_ref[...])).astype(BF16)


def _conv_in_proj(h, g, w_in, *, bm, bn, rider=None):
    m, d = h.shape
    bm, bn = min(bm, m), min(bn, d)
    nb = d // bn
    wspec = lambda part: pl.BlockSpec((d, bn), lambda i, j: (0, part * nb + j))
    out = pl.BlockSpec((bm, bn), lambda i, j: (i, j))
    return _hosted_call(
        _win_kernel,
        rider=rider, rider_blocks=(m // bm, nb), rider_index=lambda i, j: (i, j),
        grid=(m // bm, nb),
        in_specs=[pl.BlockSpec((bm, d), lambda i, j: (i, 0)), pl.BlockSpec((1, d), lambda i, j: (0, 0)),
                  wspec(0), wspec(1), wspec(2)],
        out_specs=[out, out],
        out_shape=[jax.ShapeDtypeStruct((m, d), BF16), jax.ShapeDtypeStruct((m, d), BF16)],
        args=[h, g, w_in, w_in, w_in],
        scratch_shapes=[pltpu.VMEM((bm, d), BF16)],
        compiler_params=_params("arbitrary", "arbitrary"),
        name="conv_in_proj",
    )


def _ffn_kernel(h_ref, g_ref, wg_ref, wu_ref, wd_ref, o_ref, xn_ref):
    f = pl.program_id(1)

    @pl.when(f == 0)
    def _():
        xn_ref[...] = _rms(h_ref[...], g_ref[...]).astype(BF16)
        o_ref[...] = h_ref[...]

    xn = xn_ref[...]
    a = (jax.nn.silu(_dot(xn, wg_ref[...])) * _dot(xn, wu_ref[...])).astype(BF16)
    o_ref[...] += _dot(a, wd_ref[...])


def _dense_ffn(h, g, wg, wu, wd, *, bm, bf, rider=None):
    m, d = h.shape
    ff = wg.shape[1]
    bm, bf = min(bm, m), min(bf, ff)
    row = pl.BlockSpec((bm, d), lambda i, f: (i, 0))
    return _hosted_call(
        _ffn_kernel,
        rider=rider, rider_blocks=(m // bm, ff // bf), rider_index=lambda i, f: (i, f),
        grid=(m // bm, ff // bf),
        in_specs=[row, pl.BlockSpec((1, d), lambda i, f: (0, 0)),
                  pl.BlockSpec((d, bf), lambda i, f: (0, f)),
                  pl.BlockSpec((d, bf), lambda i, f: (0, f)),
                  pl.BlockSpec((bf, d), lambda i, f: (f, 0))],
        out_specs=[row],
        out_shape=[jax.ShapeDtypeStruct((m, d), F32)],
        args=[h, g, wg, wu, wd],
        scratch_shapes=[pltpu.VMEM((bm, d), BF16)],
        compiler_params=_params("arbitrary", "arbitrary"),
        name="dense_swiglu",
    )


def _route(h, g, wr_hi, wr_lo, n_experts):
    xn = _rms(h, g)
    x_hi = xn.astype(BF16)
    x_lo = (xn - x_hi.astype(F32)).astype(BF16)
    logits = _dot(x_hi, wr_hi) + (_dot(x_lo, wr_hi) + _dot(x_hi, wr_lo))
    lane = lax.broadcasted_iota(jnp.int32, logits.shape, 1)
    neg = -jnp.inf
    l1 = jnp.where(lane < n_experts, logits, neg)
    m1 = jnp.max(l1, axis=-1, keepdims=True)
    i1 = jnp.min(jnp.where(l1 == m1, lane, LANES), axis=-1, keepdims=True)
    l2 = jnp.where(lane == i1, neg, l1)
    m2 = jnp.max(l2, axis=-1, keepdims=True)
    i2 = jnp.min(jnp.where(l2 == m2, lane, LANES), axis=-1, keepdims=True)
    t = jnp.exp(m2 - m1)
    g1 = 1.0 / (1.0 + t)
    g2 = t / (1.0 + t)
    return jnp.where(lane == 0, i1.astype(F32),
                     jnp.where(lane == 1, i2.astype(F32),
                               jnp.where(lane == 2, g1, jnp.where(lane == 3, g2, 0.0))))


def _expert_kernel(be_ref, nreal_ref, tok_ref, tok_next_ref, h_ref, g_ref, wg_ref, wu_ref, wd_ref, o_ref,
                   xg_ref, xn_ref, sem, *, rows_per_step):
    i = pl.program_id(0)
    f = pl.program_id(1)
    nblk, nf = pl.num_programs(0), pl.num_programs(1)
    n_real = nreal_ref[0]
    bm = xn_ref.shape[0]
    n_gather = tok_ref.shape[2]

    def row_copy(tok, r):
        return pltpu.make_async_copy(h_ref.at[pl.ds(tok, 1)], xg_ref.at[pl.ds(r, 1)], sem)

    def wait_gather():
        def body(r, c):
            row_copy(0, r).wait()
            return c
        lax.fori_loop(0, n_gather, body, 0, unroll=DMA_LOOP_UNROLL)

    @pl.when(jnp.logical_and(i == 0, f == 0))
    def _():
        def body(r, c):
            row_copy(tok_ref[0, 0, r], r).start()
            return c
        lax.fori_loop(0, n_gather, body, 0, unroll=DMA_LOOP_UNROLL)

    @pl.when(jnp.logical_and(f == 0, i <= n_real))
    def _():
        wait_gather()

    @pl.when(i < n_real)
    def _():
        @pl.when(f == 0)
        def _():
            xn_ref[...] = _rms(xg_ref[:bm, :], g_ref[...]).astype(BF16)
            o_ref[...] = jnp.zeros_like(o_ref)

        for u in range(rows_per_step):
            r = f * rows_per_step + u
            row_copy(tok_next_ref[0, 0, r], r).start()

        xn = xn_ref[...]
        a = (jax.nn.silu(_dot(xn, wg_ref[0])) * _dot(xn, wu_ref[0])).astype(BF16)
        o_ref[...] += _dot(a, wd_ref[0])

        @pl.when(jnp.logical_and(i == nblk - 1, f == nf - 1))
        def _():
            wait_gather()

    @pl.when(jnp.logical_and(i >= n_real, f == nf - 1))
    def _():
        o_ref[...] = jnp.zeros_like(o_ref)


def _expert_ffn(block_expert, n_real, buf_tok, h, g, wg, wu, wd, *, bm, bf):
    d = h.shape[1]
    ff = wg.shape[2]
    bf = min(bf, ff)
    nblk, nf = buf_tok.shape[0] // bm, ff // bf
    rows_per_step = pl.cdiv(bm, nf)
    n_gather = rows_per_step * nf
    tok = jnp.pad(buf_tok.reshape(nblk, 1, bm), ((0, 0), (0, 0), (0, n_gather - bm)))

    def blk(i, nr):
        return jnp.minimum(i, nr[0] - 1)

    def fblk(i, f, nr):
        return jnp.where(i < nr[0], f, nf - 1)

    return pl.pallas_call(
        functools.partial(_expert_kernel, rows_per_step=rows_per_step),
        grid_spec=pltpu.PrefetchScalarGridSpec(
            num_scalar_prefetch=2,
            grid=(nblk, nf),
            in_specs=[
                pl.BlockSpec((1, 1, n_gather), lambda i, f, be, nr: (i, 0, 0), memory_space=pltpu.SMEM),
                pl.BlockSpec((1, 1, n_gather), lambda i, f, be, nr: (jnp.minimum(i + 1, nblk - 1), 0, 0),
                             memory_space=pltpu.SMEM),
                pl.BlockSpec(memory_space=pl.ANY),
                pl.BlockSpec((1, d), lambda i, f, be, nr: (0, 0)),
                pl.BlockSpec((1, d, bf), lambda i, f, be, nr: (be[blk(i, nr)], 0, fblk(i, f, nr))),
                pl.BlockSpec((1, d, bf), lambda i, f, be, nr: (be[blk(i, nr)], 0, fblk(i, f, nr))),
                pl.BlockSpec((1, bf, d), lambda i, f, be, nr: (be[blk(i, nr)], fblk(i, f, nr), 0)),
            ],
            out_specs=pl.BlockSpec((bm, d), lambda i, f, be, nr: (i, 0)),
            scratch_shapes=[pltpu.VMEM((pl.cdiv(n_gather, F32_SUBLANES) * F32_SUBLANES, d), F32), pltpu.VMEM((bm, d), BF16),
                            pltpu.SemaphoreType.DMA(())],
        ),
        out_shape=jax.ShapeDtypeStruct((nblk * bm, d), F32),
        compiler_params=_params("arbitrary", "arbitrary"),
        name="moe_experts",
    )(block_expert, n_real, tok, tok, h, g, wg, wu, wd)


def _combine_kernel(p0_ref, p1_ref, p0_next_ref, p1_next_ref, h_ref, route_ref, y_ref, o_ref, ybuf, sem, *, tm):
    i = pl.program_id(0)
    slot = i % 2

    def row_copy(src_row, s, k, r):
        return pltpu.make_async_copy(y_ref.at[pl.ds(src_row, 1)], ybuf.at[s, k, pl.ds(r, 1)], sem.at[s])

    def start_gather(a_ref, b_ref, s):
        def body(r, c):
            row_copy(a_ref[0, 0, r], s, 0, r).start()
            row_copy(b_ref[0, 0, r], s, 1, r).start()
            return c
        lax.fori_loop(0, tm, body, 0, unroll=DMA_LOOP_UNROLL)

    def wait_gather(s):
        def body(r, c):
            row_copy(0, s, 0, r).wait()
            row_copy(0, s, 1, r).wait()
            return c
        lax.fori_loop(0, tm, body, 0, unroll=DMA_LOOP_UNROLL)

    @pl.when(i == 0)
    def _():
        start_gather(p0_ref, p1_ref, 0)

    @pl.when(i + 1 < pl.num_programs(0))
    def _():
        start_gather(p0_next_ref, p1_next_ref, 1 - slot)

    wait_gather(slot)
    g0 = route_ref[:, TOP_K:TOP_K + 1]
    g1 = route_ref[:, TOP_K + 1:TOP_K + 2]
    o_ref[...] = h_ref[...] + (ybuf[slot, 0] * g0 + ybuf[slot, 1] * g1)


def _combine(p0, p1, h, route, y, *, tm):
    m, d = h.shape
    tm = min(tm, m)
    nt = m // tm
    idx = pl.BlockSpec((1, 1, tm), lambda i: (i, 0, 0), memory_space=pltpu.SMEM)
    idx_next = pl.BlockSpec((1, 1, tm), lambda i: (jnp.minimum(i + 1, nt - 1), 0, 0), memory_space=pltpu.SMEM)
    row = pl.BlockSpec((tm, d), lambda i: (i, 0))
    p0, p1 = p0.reshape(nt, 1, tm), p1.reshape(nt, 1, tm)
    return pl.pallas_call(
        functools.partial(_combine_kernel, tm=tm),
        grid=(nt,),
        in_specs=[idx, idx, idx_next, idx_next, row, pl.BlockSpec((tm, LANES), lambda i: (i, 0)),
                  pl.BlockSpec(memory_space=pl.ANY)],
        out_specs=row,
        out_shape=jax.ShapeDtypeStruct((m, d), F32),
        scratch_shapes=[pltpu.VMEM((2, 2, tm, d), F32), pltpu.SemaphoreType.DMA((2,))],
        compiler_params=_params("arbitrary"),
        name="moe_combine",
    )(p0, p1, p0, p1, h, route, y)


def _route_plan(route, *, n_experts, bm):
    n = route.shape[0]
    flat_e = route[:, :TOP_K].astype(jnp.int32).reshape(-1)
    onehot = (flat_e[:, None] == jnp.arange(n_experts, dtype=jnp.int32)[None, :]).astype(jnp.int32)
    csum = jnp.cumsum(onehot, axis=0)
    rank = jnp.take_along_axis(csum, flat_e[:, None], axis=1)[:, 0] - 1
    counts = csum[-1]
    blocks = (counts + bm - 1) // bm
    blk_end = jnp.cumsum(blocks)
    blk_start = blk_end - blocks
    rows = blk_start[flat_e] * bm + rank
    nblk = -(-(n * TOP_K) // bm) + n_experts
    block_expert = jnp.minimum(jnp.searchsorted(blk_end, jnp.arange(nblk), side="right"),
                               n_experts - 1).astype(jnp.int32)
    buf_tok = jnp.zeros((nblk * bm,), jnp.int32).at[rows].set(jnp.arange(n * TOP_K, dtype=jnp.int32) // TOP_K)
    n_real = blk_end[-1:].astype(jnp.int32)
    return block_expert, n_real, buf_tok, rows[0::TOP_K], rows[1::TOP_K]


def kernel(x, meta_tokens, attn_norm, w_qkv, q_gain, k_gain, rel_bias, meta_bias, w_o, conv_norm, w_in, conv_w,
           w_out, dense_norm, w_gate, w_up, w_down, moe_norm, w_router, moe_gate, moe_up, moe_down):
    batch, seq, d = x.shape
    n = batch * seq
    h = d // HEAD_DIM
    n_experts = w_router.shape[-1]
    assert d % HEAD_DIM == 0 and seq % GRID_W == 0 and seq // GRID_W >= WIN_ROWS
    assert meta_tokens.shape[0] == N_META and rel_bias.shape[1:] == (h, 2 * WIN_ROWS - 1, 2 * WIN_COLS - 1)
    assert all(p.shape[0] == 1 for p in (attn_norm, conv_norm, dense_norm, moe_norm)), "two-layer trunk only"

    xt = x.reshape(n, d)
    mt = jnp.broadcast_to(meta_tokens[None].astype(x.dtype), (batch, N_META, d)).reshape(batch * N_META, d)
    nm = batch * N_META

    g_attn = attn_norm[0].reshape(1, d)
    wqkv = w_qkv[0].astype(BF16)
    head_gain = jnp.concatenate([jnp.tile(q_gain[0] * (HEAD_DIM ** -0.5), h), jnp.tile(k_gain[0], h),
                                 jnp.ones((d,), F32)]).reshape(1, 3 * d)
    qkv = _qkv(xt, g_attn, wqkv, head_gain, bm=QKV_BLOCK[0], bn=QKV_BLOCK[1])
    qkv_m = _qkv(mt, g_attn, wqkv, head_gain, bm=nm, bn=QKV_BLOCK[1])

    def meta_heads(part):
        t = qkv_m[:, part * d:(part + 1) * d].reshape(batch, N_META, h, HEAD_DIM).transpose(0, 2, 1, 3)
        return jnp.pad(t, ((0, 0), (0, 0), (0, LANES - N_META), (0, 0)))

    bias = _attention_bias(rel_bias[0], meta_bias[0])
    n_e, _, d_e = moe_gate.shape[1:]
    o, moe_gate_bf = _attention(qkv, meta_heads(1), meta_heads(2), bias, batch=batch, seq=seq, rb=ATTN_ROWS_PER_STEP,
                                rider=moe_gate[0].reshape(n_e * d, d_e))
    o_m = _meta_attention(qkv_m, meta_bias[0], batch=batch)
    wo = w_o[0].astype(BF16)
    h1 = _proj_residual(o, wo, xt, bm=PROJ_BLOCK_M)
    h1_m = _proj_residual(o_m, wo, mt, bm=nm)

    g_dense = dense_norm[0].reshape(1, d)
    wg, wu, wd = w_gate[0].astype(BF16), w_up[0].astype(BF16), w_down[0].astype(BF16)
    h2, moe_up_bf = _dense_ffn(h1, g_dense, wg, wu, wd, bm=FFN_BLOCK_M, bf=FFN_BLOCK_F,
                               rider=moe_up[0].reshape(n_e * d, d_e))
    (h2_m,) = _dense_ffn(h1_m, g_dense, wg, wu, wd, bm=nm, bf=FFN_BLOCK_F)

    g_conv = conv_norm[0].reshape(1, d)
    win = w_in[0].astype(BF16)
    bg, z, moe_down_bf = _conv_in_proj(h2, g_conv, win, bm=CONV_IN_BLOCK[0], bn=CONV_IN_BLOCK[1],
                                       rider=moe_down[0].reshape(n_e * d_e, d))
    _, z_m = _conv_in_proj(h2_m, g_conv, win, bm=nm, bn=CONV_IN_BLOCK[1])
    z_meta_last = z_m.reshape(batch, N_META, d)[:, N_META - 1:, :].astype(F32)
    g_moe = moe_norm[0].reshape(1, d)
    wr_pad = jnp.pad(w_router[0], ((0, 0), (0, LANES - n_experts)))
    wr_hi = wr_pad.astype(BF16)
    wr_lo = (wr_pad - wr_hi.astype(F32)).astype(BF16)
    h3, route = _conv_proj_residual(bg, z, z_meta_last, conv_w[0].T, w_out[0].astype(BF16), h2, g_moe, wr_hi, wr_lo,
                                    seq=seq, bm=PROJ_BLOCK_M, n_experts=n_experts)

    block_expert, n_real, buf_tok, p0, p1 = _route_plan(route, n_experts=n_experts, bm=FFN_BLOCK_M)
    ybuf = _expert_ffn(block_expert, n_real, buf_tok, h3, g_moe, moe_gate_bf.reshape(n_e, d, d_e),
                       moe_up_bf.reshape(n_e, d, d_e), moe_down_bf.reshape(n_e, d_e, d),
                       bm=FFN_BLOCK_M, bf=FFN_BLOCK_F)
    out = _combine(p0, p1, h3, route, ybuf, tm=COMBINE_BLOCK_M)
    return out.reshape(batch, seq, d)
```

```python
import functools

import numpy as np
import jax
import jax.numpy as jnp
from jax import lax
from jax.experimental import pallas as pl
from jax.experimental.pallas import tpu as pltpu

HEAD_DIM = 128
GRID_W = 64
WIN_ROWS = 8
WIN_COLS = 16
N_META = 16
TOP_K = 2
RMS_EPS = 1e-6
LANES = 128
MXU_WIDTH = 256
F32_SUBLANES = 8
BF16_SUBLANES = 16
VMEM_LIMIT_BYTES = 56 * 1024 * 1024
MASK_VALUE = -1e30
N_LOC = WIN_ROWS * GRID_W
ATTN_ROW_GROUP = 16
DMA_LOOP_UNROLL = 8
CONV_ROW_CHUNKS = 2

QKV_BLOCK = (1024, 2048)
ATTN_ROWS_PER_STEP = 64
PROJ_BLOCK_M = 512
FFN_BLOCK_M = 512
FFN_BLOCK_F = 1024
CONV_IN_BLOCK = (1024, 512)
COMBINE_BLOCK_M = 256

F32 = jnp.float32
BF16 = jnp.bfloat16


def _params(*sem):
    return pltpu.CompilerParams(dimension_semantics=sem, vmem_limit_bytes=VMEM_LIMIT_BYTES)


def _rms(xf, g):
    return xf * lax.rsqrt(jnp.mean(xf * xf, axis=-1, keepdims=True) + RMS_EPS) * g


def _dot(a, b):
    return jnp.dot(a, b, preferred_element_type=F32)


def _dot_nt(a, b):
    return lax.dot_general(a, b, (((1,), (1,)), ((), ())), preferred_element_type=F32)


def _with_cast_rider(body, n_in, n_out):
    def wrapped(*refs):
        rider_src, rider_dst = refs[n_in], refs[n_in + 1 + n_out]
        rider_dst[...] = rider_src[...].astype(rider_dst.dtype)
        body(*refs[:n_in], *refs[n_in + 1:n_in + 1 + n_out], *refs[n_in + 2 + n_out:])
    return wrapped


def _hosted_call(body, *, rider, rider_blocks, rider_index, in_specs, out_specs, out_shape, args, **kwargs):
    if rider is not None:
        rows, cols = rider.shape
        nr, nc = rider_blocks
        assert rows % (nr * BF16_SUBLANES) == 0 and cols % (nc * LANES) == 0, (rider.shape, rider_blocks)
        spec = pl.BlockSpec((rows // nr, cols // nc), rider_index)
        body = _with_cast_rider(body, len(in_specs), len(out_specs))
        in_specs, out_specs = in_specs + [spec], out_specs + [spec]
        out_shape = out_shape + [jax.ShapeDtypeStruct(rider.shape, BF16)]
        args = args + [rider]
    return pl.pallas_call(body, in_specs=in_specs, out_specs=out_specs, out_shape=out_shape, **kwargs)(*args)


def _qkv_kernel(x_ref, g_ref, w_ref, hg_ref, o_ref, xn_ref, *, n_norm_blocks):
    j = pl.program_id(1)

    @pl.when(j == 0)
    def _():
        xn_ref[...] = _rms(x_ref[...], g_ref[...]).astype(BF16)

    normed = j < n_norm_blocks
    xn = xn_ref[...]
    for c in range(o_ref.shape[1] // MXU_WIDTH):
        acc = _dot(xn, w_ref[:, c * MXU_WIDTH:(c + 1) * MXU_WIDTH])
        for hh in range(MXU_WIDTH // HEAD_DIM):
            sl = slice(c * MXU_WIDTH + hh * HEAD_DIM, c * MXU_WIDTH + (hh + 1) * HEAD_DIM)
            t = acc[:, hh * HEAD_DIM:(hh + 1) * HEAD_DIM]
            o_ref[:, sl] = jnp.where(normed, _rms(t, hg_ref[:, sl]), t).astype(BF16)


def _qkv(x, g, w, hg, *, bm, bn):
    m, d = x.shape
    n = w.shape[1]
    bm, bn = min(bm, m), min(bn, d)
    return pl.pallas_call(
        functools.partial(_qkv_kernel, n_norm_blocks=2 * d // bn),
        grid=(m // bm, n // bn),
        in_specs=[
            pl.BlockSpec((bm, d), lambda i, j: (i, 0)),
            pl.BlockSpec((1, d), lambda i, j: (0, 0)),
            pl.BlockSpec((d, bn), lambda i, j: (0, j)),
            pl.BlockSpec((1, bn), lambda i, j: (0, j)),
        ],
        out_specs=pl.BlockSpec((bm, bn), lambda i, j: (i, j)),
        out_shape=jax.ShapeDtypeStruct((m, n), BF16),
        scratch_shapes=[pltpu.VMEM((bm, d), BF16)],
        compiler_params=_params("arbitrary", "arbitrary"),
        name="qkv_proj",
    )(x, g, w, hg)


def _attn_kernel(q_ref, k_ref, v_ref, km_ref, vm_ref, bias_ref, o_ref, *, rb, n_rows):
    rblk = pl.program_id(2)
    km = km_ref[0, 0]
    vm = vm_ref[0, 0]

    def group(gi, carry):
        scores, probs = [], []
        for u in range(ATTN_ROW_GROUP):
            rr = gi * ATTN_ROW_GROUP + u
            r = rblk * rb + rr
            r0 = jnp.clip(r - WIN_ROWS // 2, 0, n_rows - WIN_ROWS)
            start = pl.multiple_of(r0 * GRID_W, GRID_W)
            qs = pl.multiple_of(rr * GRID_W, GRID_W)
            q = q_ref[pl.ds(qs, GRID_W), :]
            kw = k_ref[pl.ds(start, N_LOC), :]
            s = jnp.concatenate([_dot_nt(q, kw), _dot_nt(q, km)], axis=-1) + bias_ref[r - r0, 0]
            scores.append((qs, start, s))
        for qs, start, s in scores:
            p = jnp.exp(s - jnp.max(s, axis=-1, keepdims=True))
            probs.append((qs, start, p.astype(BF16), jnp.sum(p, axis=-1, keepdims=True)))
        for qs, start, pb, l in probs:
            vw = v_ref[pl.ds(start, N_LOC), :]
            o = _dot(pb[:, :N_LOC], vw) + _dot(pb[:, N_LOC:], vm)
            o_ref[pl.ds(qs, GRID_W), :] = (o / l).astype(BF16)
        return carry

    lax.fori_loop(0, rb // ATTN_ROW_GROUP, group, 0)


def _attention(qkv, km, vm, bias, *, batch, seq, rb, rider=None):
    n, d3 = qkv.shape
    d = d3 // 3
    h = d // HEAD_DIM
    n_rows = seq // GRID_W
    rb = min(rb, n_rows)
    nrb = n_rows // rb
    kv_spec = lambda off: pl.BlockSpec((seq, HEAD_DIM), lambda b, hh, r: (b, off + hh))
    meta_spec = pl.BlockSpec((1, 1, LANES, HEAD_DIM), lambda b, hh, r: (b, hh, 0, 0))
    return _hosted_call(
        functools.partial(_attn_kernel, rb=rb, n_rows=n_rows),
        rider=rider, rider_blocks=(batch * h * nrb, 1), rider_index=lambda b, hh, r: ((b * h + hh) * nrb + r, 0),
        grid=(batch, h, nrb),
        in_specs=[
            pl.BlockSpec((rb * GRID_W, HEAD_DIM), lambda b, hh, r: (b * nrb + r, hh)),
            kv_spec(h),
            kv_spec(2 * h),
            meta_spec,
            meta_spec,
            pl.BlockSpec((WIN_ROWS, 1, GRID_W, N_LOC + LANES), lambda b, hh, r: (0, hh, 0, 0)),
        ],
        out_specs=[pl.BlockSpec((rb * GRID_W, HEAD_DIM), lambda b, hh, r: (b * nrb + r, hh))],
        out_shape=[jax.ShapeDtypeStruct((n, d), BF16)],
        args=[qkv, qkv, qkv, km, vm, bias],
        compiler_params=_params("arbitrary", "arbitrary", "arbitrary"),
        name="nbr_attention",
    )


def _meta_attn_kernel(q_ref, k_ref, v_ref, mb_ref, o_ref):
    s = _dot_nt(q_ref[...], k_ref[...]) + mb_ref[0]
    p = jnp.exp(s - jnp.max(s, axis=-1, keepdims=True))
    l = jnp.sum(p, axis=-1, keepdims=True)
    o_ref[...] = (_dot(p.astype(BF16), v_ref[...]) / l).astype(BF16)


def _meta_attention(qkv_m, meta_bias, *, batch):
    d = qkv_m.shape[1] // 3
    h = d // HEAD_DIM
    spec = lambda off: pl.BlockSpec((N_META, HEAD_DIM), lambda b, hh: (b, off + hh))
    return pl.pallas_call(
        _meta_attn_kernel,
        grid=(batch, h),
        in_specs=[spec(0), spec(h), spec(2 * h),
                  pl.BlockSpec((1, 1, N_META), lambda b, hh: (hh, 0, 0))],
        out_specs=spec(0),
        out_shape=jax.ShapeDtypeStruct((batch * N_META, d), BF16),
        compiler_params=_params("arbitrary", "arbitrary"),
        name="meta_attention",
    )(qkv_m, qkv_m, qkv_m, meta_bias.reshape(h, 1, N_META))


def _attention_bias(rel_bias, meta_bias):
    h = rel_bias.shape[0]
    cq = np.arange(GRID_W)
    c0 = np.clip(cq - WIN_COLS // 2, 0, GRID_W - WIN_COLS)
    ck = np.arange(GRID_W)
    in_win = (ck[None, :] >= c0[:, None]) & (ck[None, :] < c0[:, None] + WIN_COLS)
    dc = ck[None, :] - cq[:, None] + WIN_COLS - 1
    by_col = jnp.full((h, 2 * WIN_ROWS - 1, GRID_W, GRID_W), MASK_VALUE, F32)
    for j in range(2 * WIN_COLS - 1):
        by_col = jnp.where((in_win & (dc == j))[None, None], rel_bias[:, :, j, None, None], by_col)
    loc = jnp.stack([by_col[:, WIN_ROWS - 1 - off:2 * WIN_ROWS - 1 - off] for off in range(WIN_ROWS)])
    loc = loc.transpose(0, 1, 3, 2, 4).reshape(WIN_ROWS, h, GRID_W, N_LOC)
    met = jnp.concatenate([meta_bias, jnp.full((h, LANES - N_META), MASK_VALUE, F32)], axis=-1)
    met = jnp.broadcast_to(met[None, :, None, :], (WIN_ROWS, h, GRID_W, LANES))
    return jnp.concatenate([loc, met], axis=-1).astype(F32)


def _proj_kernel(a_ref, w_ref, r_ref, o_ref):
    o_ref[...] = r_ref[...] + _dot(a_ref[...], w_ref[...])


def _proj_residual(a, w, r, *, bm):
    m, d = r.shape
    bm = min(bm, m)
    row = pl.BlockSpec((bm, d), lambda i: (i, 0))
    return pl.pallas_call(
        _proj_kernel,
        grid=(m // bm,),
        in_specs=[row, pl.BlockSpec((d, d), lambda i: (0, 0)), row],
        out_specs=row,
        out_shape=jax.ShapeDtypeStruct((m, d), F32),
        compiler_params=_params("arbitrary"),
        name="attn_out_proj",
    )(a, w, r)


def _conv_proj_kernel(bg_ref, z_ref, zp_ref, zn_ref, zm_ref, cw_ref, w_ref, r_ref, gr_ref, wr_hi_ref, wr_lo_ref,
                      o_ref, route_ref, *, blocks_per_seq, n_experts):
    i = pl.program_id(0)
    z = z_ref[...].astype(F32)
    bm = z.shape[0]
    first = (i % blocks_per_seq) == 0
    last = (i % blocks_per_seq) == blocks_per_seq - 1
    prev_row = jnp.where(first, zm_ref[0], zp_ref[BF16_SUBLANES - 1:BF16_SUBLANES, :].astype(F32))
    next_row = jnp.where(last, 0.0, zn_ref[0:1, :].astype(F32))
    row = lax.broadcasted_iota(jnp.int32, z.shape, 0)
    z_prev = jnp.where(row == 0, prev_row, pltpu.roll(z, 1, axis=0))
    z_next = jnp.where(row == bm - 1, next_row, pltpu.roll(z, bm - 1, axis=0))
    y = z_prev * cw_ref[0:1, :] + z * cw_ref[1:2, :] + z_next * cw_ref[2:3, :]
    a = (bg_ref[...].astype(F32) * y).astype(BF16)
    for s in range(0, bm, bm // CONV_ROW_CHUNKS):
        rows = slice(s, s + bm // CONV_ROW_CHUNKS)
        h = r_ref[rows, :] + _dot(a[rows, :], w_ref[...])
        o_ref[rows, :] = h
        route_ref[rows, :] = _route(h, gr_ref[...], wr_hi_ref[...], wr_lo_ref[...], n_experts)


def _conv_proj_residual(bg, z, z_meta_last, conv_w_t, w, r, g_route, wr_hi, wr_lo, *, seq, bm, n_experts):
    m, d = r.shape
    bm = min(bm, seq)
    bps = seq // bm
    halo = bm // BF16_SUBLANES
    n_halo = m // BF16_SUBLANES
    row = pl.BlockSpec((bm, d), lambda i: (i, 0))
    const = lambda shape: pl.BlockSpec(shape, lambda i: (0,) * len(shape), pipeline_mode=pl.Buffered(1))
    return pl.pallas_call(
        functools.partial(_conv_proj_kernel, blocks_per_seq=bps, n_experts=n_experts),
        grid=(m // bm,),
        in_specs=[
            row, row,
            pl.BlockSpec((BF16_SUBLANES, d), lambda i: (jnp.maximum(i * halo - 1, 0), 0)),
            pl.BlockSpec((BF16_SUBLANES, d), lambda i: (jnp.minimum((i + 1) * halo, n_halo - 1), 0)),
            pl.BlockSpec((1, 1, d), lambda i: (i // bps, 0, 0)),
            const((3, d)),
            const((d, d)),
            row,
            const((1, d)),
            const((d, LANES)),
            const((d, LANES)),
        ],
        out_specs=[row, pl.BlockSpec((bm, LANES), lambda i: (i, 0))],
        out_shape=[jax.ShapeDtypeStruct((m, d), F32), jax.ShapeDtypeStruct((m, LANES), F32)],
        compiler_params=_params("arbitrary"),
        name="conv_out_proj",
    )(bg, z, z, z, z_meta_last, conv_w_t, w, r, g_route, wr_hi, wr_lo)


def _win_kernel(h_ref, g_ref, wb_ref, wc_ref, wu_ref, bg_ref, z_ref, xn_ref):
    @pl.when(pl.program_id(1) == 0)
    def _():
        xn_ref[...] = _rms(h_ref[...], g_ref[...]).astype(BF16)

    xn = xn_ref[...]
    bg_ref[...] = _dot(xn, wb_ref[...]).astype(BF16)
    z_ref[...] = (_dot(xn, wc_ref[...]) * _dot(xn, wu_ref[...])).astype(BF16)


def _conv_in_proj(h, g, w_in, *, bm, bn, rider=None):
    m, d = h.shape
    bm, bn = min(bm, m), min(bn, d)
    nb = d // bn
    wspec = lambda part: pl.BlockSpec((d, bn), lambda i, j: (0, part * nb + j))
    out = pl.BlockSpec((bm, bn), lambda i, j: (i, j))
    return _hosted_call(
        _win_kernel,
        rider=rider, rider_blocks=(m // bm, nb), rider_index=lambda i, j: (i, j),
        grid=(m // bm, nb),
        in_specs=[pl.BlockSpec((bm, d), lambda i, j: (i, 0)), pl.BlockSpec((1, d), lambda i, j: (0, 0)),
                  wspec(0), wspec(1), wspec(2)],
        out_specs=[out, out],
        out_shape=[jax.ShapeDtypeStruct((m, d), BF16), jax.ShapeDtypeStruct((m, d), BF16)],
        args=[h, g, w_in, w_in, w_in],
        scratch_shapes=[pltpu.VMEM((bm, d), BF16)],
        compiler_params=_params("arbitrary", "arbitrary"),
        name="conv_in_proj",
    )


def _ffn_kernel(h_ref, g_ref, wg_ref, wu_ref, wd_ref, o_ref, xn_ref):
    f = pl.program_id(1)

    @pl.when(f == 0)
    def _():
        xn_ref[...] = _rms(h_ref[...], g_ref[...]).astype(BF16)
        o_ref[...] = h_ref[...]

    xn = xn_ref[...]
    a = (jax.nn.silu(_dot(xn, wg_ref[...])) * _dot(xn, wu_ref[...])).astype(BF16)
    o_ref[...] += _dot(a, wd_ref[...])


def _dense_ffn(h, g, wg, wu, wd, *, bm, bf, rider=None):
    m, d = h.shape
    ff = wg.shape[1]
    bm, bf = min(bm, m), min(bf, ff)
    row = pl.BlockSpec((bm, d), lambda i, f: (i, 0))
    return _hosted_call(
        _ffn_kernel,
        rider=rider, rider_blocks=(m // bm, ff // bf), rider_index=lambda i, f: (i, f),
        grid=(m // bm, ff // bf),
        in_specs=[row, pl.BlockSpec((1, d), lambda i, f: (0, 0)),
                  pl.BlockSpec((d, bf), lambda i, f: (0, f)),
                  pl.BlockSpec((d, bf), lambda i, f: (0, f)),
                  pl.BlockSpec((bf, d), lambda i, f: (f, 0))],
        out_specs=[row],
        out_shape=[jax.ShapeDtypeStruct((m, d), F32)],
        args=[h, g, wg, wu, wd],
        scratch_shapes=[pltpu.VMEM((bm, d), BF16)],
        compiler_params=_params("arbitrary", "arbitrary"),
        name="dense_swiglu",
    )


def _route(h, g, wr_hi, wr_lo, n_experts):
    xn = _rms(h, g)
    x_hi = xn.astype(BF16)
    x_lo = (xn - x_hi.astype(F32)).astype(BF16)
    logits = _dot(x_hi, wr_hi) + (_dot(x_lo, wr_hi) + _dot(x_hi, wr_lo))
    lane = lax.broadcasted_iota(jnp.int32, logits.shape, 1)
    neg = -jnp.inf
    l1 = jnp.where(lane < n_experts, logits, neg)
    m1 = jnp.max(l1, axis=-1, keepdims=True)
    i1 = jnp.min(jnp.where(l1 == m1, lane, LANES), axis=-1, keepdims=True)
    l2 = jnp.where(lane == i1, neg, l1)
    m2 = jnp.max(l2, axis=-1, keepdims=True)
    i2 = jnp.min(jnp.where(l2 == m2, lane, LANES), axis=-1, keepdims=True)
    t = jnp.exp(m2 - m1)
    g1 = 1.0 / (1.0 + t)
    g2 = t / (1.0 + t)
    return jnp.where(lane == 0, i1.astype(F32),
                     jnp.where(lane == 1, i2.astype(F32),
                               jnp.where(lane == 2, g1, jnp.where(lane == 3, g2, 0.0))))


def _expert_kernel(be_ref, nreal_ref, tok_ref, tok_next_ref, h_ref, g_ref, wg_ref, wu_ref, wd_ref, o_ref,
                   xg_ref, xn_ref, sem, *, rows_per_step):
    i = pl.program_id(0)
    f = pl.program_id(1)
    nblk, nf = pl.num_programs(0), pl.num_programs(1)
    n_real = nreal_ref[0]
    bm, d = xn_ref.shape

    def row_copy(tok, step, u):
        dst = xg_ref.at[step, u // F32_SUBLANES, pl.ds(u % F32_SUBLANES, 1)]
        return pltpu.make_async_copy(h_ref.at[pl.ds(tok, 1)], dst, sem)

    def start_step(idx_ref, step):
        for u in range(rows_per_step):
            row_copy(idx_ref[0, 0, step * rows_per_step + u], step, u).start()

    def wait_gather():
        def body(step, c):
            for u in range(rows_per_step):
                row_copy(0, step, u).wait()
            return c
        lax.fori_loop(0, xg_ref.shape[0], body, 0)

    @pl.when(jnp.logical_and(i == 0, f == 0))
    def _():
        def body(step, c):
            start_step(tok_ref, step)
            return c
        lax.fori_loop(0, xg_ref.shape[0], body, 0)

    @pl.when(jnp.logical_and(f == 0, i <= n_real))
    def _():
        wait_gather()

    @pl.when(i < n_real)
    def _():
        @pl.when(f == 0)
        def _():
            x = xg_ref[...].reshape(-1, d)[:bm, :]
            xn_ref[...] = _rms(x, g_ref[...]).astype(BF16)
            o_ref[...] = jnp.zeros_like(o_ref)

        start_step(tok_next_ref, f)

        xn = xn_ref[...]
        a = (jax.nn.silu(_dot(xn, wg_ref[0])) * _dot(xn, wu_ref[0])).astype(BF16)
        o_ref[...] += _dot(a, wd_ref[0])

        @pl.when(jnp.logical_and(i == nblk - 1, f == nf - 1))
        def _():
            wait_gather()

    @pl.when(jnp.logical_and(i >= n_real, f == nf - 1))
    def _():
        o_ref[...] = jnp.zeros_like(o_ref)


def _expert_ffn(block_expert, n_real, buf_tok, h, g, wg, wu, wd, *, bm, bf):
    d = h.shape[1]
    ff = wg.shape[2]
    bf = min(bf, ff)
    nblk, nf = buf_tok.shape[0] // bm, ff // bf
    rows_per_step = pl.cdiv(pl.cdiv(bm, nf), F32_SUBLANES) * F32_SUBLANES
    n_gather = rows_per_step * nf
    tok = jnp.pad(buf_tok.reshape(nblk, 1, bm), ((0, 0), (0, 0), (0, n_gather - bm)))

    def blk(i, nr):
        return jnp.minimum(i, nr[0] - 1)

    def fblk(i, f, nr):
        return jnp.where(i < nr[0], f, nf - 1)

    return pl.pallas_call(
        functools.partial(_expert_kernel, rows_per_step=rows_per_step),
        grid_spec=pltpu.PrefetchScalarGridSpec(
            num_scalar_prefetch=2,
            grid=(nblk, nf),
            in_specs=[
                pl.BlockSpec((1, 1, n_gather), lambda i, f, be, nr: (i, 0, 0), memory_space=pltpu.SMEM),
                pl.BlockSpec((1, 1, n_gather), lambda i, f, be, nr: (jnp.minimum(i + 1, nblk - 1), 0, 0),
                             memory_space=pltpu.SMEM),
                pl.BlockSpec(memory_space=pl.ANY),
                pl.BlockSpec((1, d), lambda i, f, be, nr: (0, 0)),
                pl.BlockSpec((1, d, bf), lambda i, f, be, nr: (be[blk(i, nr)], 0, fblk(i, f, nr))),
                pl.BlockSpec((1, d, bf), lambda i, f, be, nr: (be[blk(i, nr)], 0, fblk(i, f, nr))),
                pl.BlockSpec((1, bf, d), lambda i, f, be, nr: (be[blk(i, nr)], fblk(i, f, nr), 0)),
            ],
            out_specs=pl.BlockSpec((bm, d), lambda i, f, be, nr: (i, 0)),
            scratch_shapes=[pltpu.VMEM((nf, rows_per_step // F32_SUBLANES, F32_SUBLANES, d), F32),
                            pltpu.VMEM((bm, d), BF16),
                            pltpu.SemaphoreType.DMA(())],
        ),
        out_shape=jax.ShapeDtypeStruct((nblk * bm, d), F32),
        compiler_params=_params("arbitrary", "arbitrary"),
        name="moe_experts",
    )(block_expert, n_real, tok, tok, h, g, wg, wu, wd)


def _combine_kernel(p0_ref, p1_ref, p0_next_ref, p1_next_ref, h_ref, route_ref, y_ref, o_ref, ybuf, sem, *, tm):
    i = pl.program_id(0)
    slot = i % 2

    def row_copy(src_row, s, k, t, u):
        return pltpu.make_async_copy(y_ref.at[pl.ds(src_row, 1)], ybuf.at[s, k, t, pl.ds(u, 1)], sem.at[s])

    def start_gather(a_ref, b_ref, s):
        def body(t, c):
            for u in range(F32_SUBLANES):
                row_copy(a_ref[0, 0, t * F32_SUBLANES + u], s, 0, t, u).start()
                row_copy(b_ref[0, 0, t * F32_SUBLANES + u], s, 1, t, u).start()
            return c
        lax.fori_loop(0, tm // F32_SUBLANES, body, 0)

    def wait_gather(s):
        def body(t, c):
            for u in range(F32_SUBLANES):
                row_copy(0, s, 0, t, u).wait()
                row_copy(0, s, 1, t, u).wait()
            return c
        lax.fori_loop(0, tm // F32_SUBLANES, body, 0)

    @pl.when(i == 0)
    def _():
        start_gather(p0_ref, p1_ref, 0)

    @pl.when(i + 1 < pl.num_programs(0))
    def _():
        start_gather(p0_next_ref, p1_next_ref, 1 - slot)

    wait_gather(slot)
    g0 = route_ref[:, TOP_K:TOP_K + 1]
    g1 = route_ref[:, TOP_K + 1:TOP_K + 2]
    y0 = ybuf[slot, 0].reshape(tm, -1)
    y1 = ybuf[slot, 1].reshape(tm, -1)
    o_ref[...] = h_ref[...] + (y0 * g0 + y1 * g1)


def _combine(p0, p1, h, route, y, *, tm):
    m, d = h.shape
    tm = min(tm, m)
    nt = m // tm
    idx = pl.BlockSpec((1, 1, tm), lambda i: (i, 0, 0), memory_space=pltpu.SMEM)
    idx_next = pl.BlockSpec((1, 1, tm), lambda i: (jnp.minimum(i + 1, nt - 1), 0, 0), memory_space=pltpu.SMEM)
    row = pl.BlockSpec((tm, d), lambda i: (i, 0))
    p0, p1 = p0.reshape(nt, 1, tm), p1.reshape(nt, 1, tm)
    return pl.pallas_call(
        functools.partial(_combine_kernel, tm=tm),
        grid=(nt,),
        in_specs=[idx, idx, idx_next, idx_next, row, pl.BlockSpec((tm, LANES), lambda i: (i, 0)),
                  pl.BlockSpec(memory_space=pl.ANY)],
        out_specs=row,
        out_shape=jax.ShapeDtypeStruct((m, d), F32),
        scratch_shapes=[pltpu.VMEM((2, 2, tm // F32_SUBLANES, F32_SUBLANES, d), F32),
                        pltpu.SemaphoreType.DMA((2,))],
        compiler_params=_params("arbitrary"),
        name="moe_combine",
    )(p0, p1, p0, p1, h, route, y)


def _route_plan(route, *, n_experts, bm):
    n = route.shape[0]
    flat_e = route[:, :TOP_K].astype(jnp.int32).reshape(-1)
    onehot = (flat_e[:, None] == jnp.arange(n_experts, dtype=jnp.int32)[None, :]).astype(jnp.int32)
    csum = jnp.cumsum(onehot, axis=0)
    rank = jnp.take_along_axis(csum, flat_e[:, None], axis=1)[:, 0] - 1
    counts = csum[-1]
    blocks = (counts + bm - 1) // bm
    blk_end = jnp.cumsum(blocks)
    blk_start = blk_end - blocks
    rows = blk_start[flat_e] * bm + rank
    nblk = -(-(n * TOP_K) // bm) + n_experts
    block_expert = jnp.minimum(jnp.searchsorted(blk_end, jnp.arange(nblk), side="right"),
                               n_experts - 1).astype(jnp.int32)
    buf_tok = jnp.zeros((nblk * bm,), jnp.int32).at[rows].set(jnp.arange(n * TOP_K, dtype=jnp.int32) // TOP_K)
    n_real = blk_end[-1:].astype(jnp.int32)
    return block_expert, n_real, buf_tok, rows[0::TOP_K], rows[1::TOP_K]


def kernel(x, meta_tokens, attn_norm, w_qkv, q_gain, k_gain, rel_bias, meta_bias, w_o, conv_norm, w_in, conv_w,
           w_out, dense_norm, w_gate, w_up, w_down, moe_norm, w_router, moe_gate, moe_up, moe_down):
    batch, seq, d = x.shape
    n = batch * seq
    h = d // HEAD_DIM
    n_experts = w_router.shape[-1]
    assert d % HEAD_DIM == 0 and seq % GRID_W == 0 and seq // GRID_W >= WIN_ROWS
    assert meta_tokens.shape[0] == N_META and rel_bias.shape[1:] == (h, 2 * WIN_ROWS - 1, 2 * WIN_COLS - 1)
    assert all(p.shape[0] == 1 for p in (attn_norm, conv_norm, dense_norm, moe_norm)), "two-layer trunk only"

    xt = x.reshape(n, d)
    mt = jnp.broadcast_to(meta_tokens[None].astype(x.dtype), (batch, N_META, d)).reshape(batch * N_META, d)
    nm = batch * N_META

    g_attn = attn_norm[0].reshape(1, d)
    wqkv = w_qkv[0].astype(BF16)
    head_gain = jnp.concatenate([jnp.tile(q_gain[0] * (HEAD_DIM ** -0.5), h), jnp.tile(k_gain[0], h),
                                 jnp.ones((d,), F32)]).reshape(1, 3 * d)
    qkv = _qkv(xt, g_attn, wqkv, head_gain, bm=QKV_BLOCK[0], bn=QKV_BLOCK[1])
    qkv_m = _qkv(mt, g_attn, wqkv, head_gain, bm=nm, bn=QKV_BLOCK[1])

    def meta_heads(part):
        t = qkv_m[:, part * d:(part + 1) * d].reshape(batch, N_META, h, HEAD_DIM).transpose(0, 2, 1, 3)
        return jnp.pad(t, ((0, 0), (0, 0), (0, LANES - N_META), (0, 0)))

    bias = _attention_bias(rel_bias[0], meta_bias[0])
    n_e, _, d_e = moe_gate.shape[1:]
    o, moe_gate_bf = _attention(qkv, meta_heads(1), meta_heads(2), bias, batch=batch, seq=seq, rb=ATTN_ROWS_PER_STEP,
                                rider=moe_gate[0].reshape(n_e * d, d_e))
    o_m = _meta_attention(qkv_m, meta_bias[0], batch=batch)
    wo = w_o[0].astype(BF16)
    h1 = _proj_residual(o, wo, xt, bm=PROJ_BLOCK_M)
    h1_m = _proj_residual(o_m, wo, mt, bm=nm)

    g_dense = dense_norm[0].reshape(1, d)
    wg, wu, wd = w_gate[0].astype(BF16), w_up[0].astype(BF16), w_down[0].astype(BF16)
    h2, moe_up_bf = _dense_ffn(h1, g_dense, wg, wu, wd, bm=FFN_BLOCK_M, bf=FFN_BLOCK_F,
                               rider=moe_up[0].reshape(n_e * d, d_e))
    (h2_m,) = _dense_ffn(h1_m, g_dense, wg, wu, wd, bm=nm, bf=FFN_BLOCK_F)

    g_conv = conv_norm[0].reshape(1, d)
    win = w_in[0].astype(BF16)
    bg, z, moe_down_bf = _conv_in_proj(h2, g_conv, win, bm=CONV_IN_BLOCK[0], bn=CONV_IN_BLOCK[1],
                                       rider=moe_down[0].reshape(n_e * d_e, d))
    _, z_m = _conv_in_proj(h2_m, g_conv, win, bm=nm, bn=CONV_IN_BLOCK[1])
    z_meta_last = z_m.reshape(batch, N_META, d)[:, N_META - 1:, :].astype(F32)
    g_moe = moe_norm[0].reshape(1, d)
    wr_pad = jnp.pad(w_router[0], ((0, 0), (0, LANES - n_experts)))
    wr_hi = wr_pad.astype(BF16)
    wr_lo = (wr_pad - wr_hi.astype(F32)).astype(BF16)
    h3, route = _conv_proj_residual(bg, z, z_meta_last, conv_w[0].T, w_out[0].astype(BF16), h2, g_moe, wr_hi, wr_lo,
                                    seq=seq, bm=PROJ_BLOCK_M, n_experts=n_experts)

    block_expert, n_real, buf_tok, p0, p1 = _route_plan(route, n_experts=n_experts, bm=FFN_BLOCK_M)
    ybuf = _expert_ffn(block_expert, n_real, buf_tok, h3, g_moe, moe_gate_bf.reshape(n_e, d, d_e),
                       moe_up_bf.reshape(n_e, d, d_e), moe_down_bf.reshape(n_e, d_e, d),
                       bm=FFN_BLOCK_M, bf=FFN_BLOCK_F)
    out = _combine(p0, p1, h3, route, ybuf, tm=COMBINE_BLOCK_M)
    return out.reshape(batch, seq, d)
```

```python
import functools

import numpy as np
import jax
import jax.numpy as jnp
from jax import lax
from jax.experimental import pallas as pl
from jax.experimental.pallas import tpu as pltpu

HEAD_DIM = 128
GRID_W = 64
WIN_ROWS = 8
WIN_COLS = 16
N_META = 16
TOP_K = 2
RMS_EPS = 1e-6
LANES = 128
MXU_WIDTH = 256
F32_SUBLANES = 8
BF16_SUBLANES = 16
VMEM_LIMIT_BYTES = 56 * 1024 * 1024
MASK_VALUE = -1e30
N_LOC = WIN_ROWS * GRID_W
ATTN_ROW_GROUP = 16
DMA_LOOP_UNROLL = 8
CONV_ROW_CHUNKS = 2

QKV_BLOCK = (1024, 2048)
ATTN_ROWS_PER_STEP = 64
PROJ_BLOCK_M = 512
FFN_BLOCK_M = 512
FFN_BLOCK_F = 1024
CONV_IN_BLOCK = (1024, 512)
COMBINE_BLOCK_M = 256

F32 = jnp.float32
BF16 = jnp.bfloat16


def _params(*sem):
    return pltpu.CompilerParams(dimension_semantics=sem, vmem_limit_bytes=VMEM_LIMIT_BYTES)


def _rms(xf, g):
    return xf * lax.rsqrt(jnp.mean(xf * xf, axis=-1, keepdims=True) + RMS_EPS) * g


def _dot(a, b):
    return jnp.dot(a, b, preferred_element_type=F32)


def _dot_nt(a, b):
    return lax.dot_general(a, b, (((1,), (1,)), ((), ())), preferred_element_type=F32)


def _with_cast_rider(body, n_in, n_out):
    def wrapped(*refs):
        rider_src, rider_dst = refs[n_in], refs[n_in + 1 + n_out]
        rider_dst[...] = rider_src[...].astype(rider_dst.dtype)
        body(*refs[:n_in], *refs[n_in + 1:n_in + 1 + n_out], *refs[n_in + 2 + n_out:])
    return wrapped


def _hosted_call(body, *, rider, rider_blocks, rider_index, in_specs, out_specs, out_shape, args, **kwargs):
    if rider is not None:
        rows, cols = rider.shape
        nr, nc = rider_blocks
        assert rows % (nr * BF16_SUBLANES) == 0 and cols % (nc * LANES) == 0, (rider.shape, rider_blocks)
        spec = pl.BlockSpec((rows // nr, cols // nc), rider_index)
        body = _with_cast_rider(body, len(in_specs), len(out_specs))
        in_specs, out_specs = in_specs + [spec], out_specs + [spec]
        out_shape = out_shape + [jax.ShapeDtypeStruct(rider.shape, BF16)]
        args = args + [rider]
    return pl.pallas_call(body, in_specs=in_specs, out_specs=out_specs, out_shape=out_shape, **kwargs)(*args)


def _qkv_kernel(x_ref, g_ref, w_ref, hg_ref, o_ref, xn_ref, *, n_norm_blocks):
    j = pl.program_id(1)

    @pl.when(j == 0)
    def _():
        xn_ref[...] = _rms(x_ref[...], g_ref[...]).astype(BF16)

    normed = j < n_norm_blocks
    xn = xn_ref[...]
    for c in range(o_ref.shape[1] // MXU_WIDTH):
        acc = _dot(xn, w_ref[:, c * MXU_WIDTH:(c + 1) * MXU_WIDTH])
        for hh in range(MXU_WIDTH // HEAD_DIM):
            sl = slice(c * MXU_WIDTH + hh * HEAD_DIM, c * MXU_WIDTH + (hh + 1) * HEAD_DIM)
            t = acc[:, hh * HEAD_DIM:(hh + 1) * HEAD_DIM]
            o_ref[:, sl] = jnp.where(normed, _rms(t, hg_ref[:, sl]), t).astype(BF16)


def _qkv(x, g, w, hg, *, bm, bn):
    m, d = x.shape
    n = w.shape[1]
    bm, bn = min(bm, m), min(bn, d)
    return pl.pallas_call(
        functools.partial(_qkv_kernel, n_norm_blocks=2 * d // bn),
        grid=(m // bm, n // bn),
        in_specs=[
            pl.BlockSpec((bm, d), lambda i, j: (i, 0)),
            pl.BlockSpec((1, d), lambda i, j: (0, 0)),
            pl.BlockSpec((d, bn), lambda i, j: (0, j)),
            pl.BlockSpec((1, bn), lambda i, j: (0, j)),
        ],
        out_specs=pl.BlockSpec((bm, bn), lambda i, j: (i, j)),
        out_shape=jax.ShapeDtypeStruct((m, n), BF16),
        scratch_shapes=[pltpu.VMEM((bm, d), BF16)],
        compiler_params=_params("arbitrary", "arbitrary"),
        name="qkv_proj",
    )(x, g, w, hg)


def _attn_kernel(q_ref, k_ref, v_ref, km_ref, vm_ref, bias_ref, o_ref, *, rb, n_rows):
    rblk = pl.program_id(2)
    km = km_ref[0, 0]
    vm = vm_ref[0, 0]

    def group(gi, carry):
        scores, probs = [], []
        for u in range(ATTN_ROW_GROUP):
            rr = gi * ATTN_ROW_GROUP + u
            r = rblk * rb + rr
            r0 = jnp.clip(r - WIN_ROWS // 2, 0, n_rows - WIN_ROWS)
            start = pl.multiple_of(r0 * GRID_W, GRID_W)
            qs = pl.multiple_of(rr * GRID_W, GRID_W)
            q = q_ref[pl.ds(qs, GRID_W), :]
            kw = k_ref[pl.ds(start, N_LOC), :]
            s = jnp.concatenate([_dot_nt(q, kw), _dot_nt(q, km)], axis=-1) + bias_ref[r - r0, 0]
            scores.append((qs, start, s))
        for qs, start, s in scores:
            p = jnp.exp(s - jnp.max(s, axis=-1, keepdims=True))
            probs.append((qs, start, p.astype(BF16), jnp.sum(p, axis=-1, keepdims=True)))
        for qs, start, pb, l in probs:
            vw = v_ref[pl.ds(start, N_LOC), :]
            o = _dot(pb[:, :N_LOC], vw) + _dot(pb[:, N_LOC:], vm)
            o_ref[pl.ds(qs, GRID_W), :] = (o / l).astype(BF16)
        return carry

    lax.fori_loop(0, rb // ATTN_ROW_GROUP, group, 0)


def _attention(qkv, km, vm, bias, *, batch, seq, rb, rider=None):
    n, d3 = qkv.shape
    d = d3 // 3
    h = d // HEAD_DIM
    n_rows = seq // GRID_W
    rb = min(rb, n_rows)
    nrb = n_rows // rb
    kv_spec = lambda off: pl.BlockSpec((seq, HEAD_DIM), lambda b, hh, r: (b, off + hh))
    meta_spec = pl.BlockSpec((1, 1, LANES, HEAD_DIM), lambda b, hh, r: (b, hh, 0, 0))
    return _hosted_call(
        functools.partial(_attn_kernel, rb=rb, n_rows=n_rows),
        rider=rider, rider_blocks=(batch * h * nrb, 1), rider_index=lambda b, hh, r: ((b * h + hh) * nrb + r, 0),
        grid=(batch, h, nrb),
        in_specs=[
            pl.BlockSpec((rb * GRID_W, HEAD_DIM), lambda b, hh, r: (b * nrb + r, hh)),
            kv_spec(h),
            kv_spec(2 * h),
            meta_spec,
            meta_spec,
            pl.BlockSpec((WIN_ROWS, 1, GRID_W, N_LOC + LANES), lambda b, hh, r: (0, hh, 0, 0)),
        ],
        out_specs=[pl.BlockSpec((rb * GRID_W, HEAD_DIM), lambda b, hh, r: (b * nrb + r, hh))],
        out_shape=[jax.ShapeDtypeStruct((n, d), BF16)],
        args=[qkv, qkv, qkv, km, vm, bias],
        compiler_params=_params("arbitrary", "arbitrary", "arbitrary"),
        name="nbr_attention",
    )


def _meta_attn_kernel(q_ref, k_ref, v_ref, mb_ref, o_ref):
    s = _dot_nt(q_ref[...], k_ref[...]) + mb_ref[0]
    p = jnp.exp(s - jnp.max(s, axis=-1, keepdims=True))
    l = jnp.sum(p, axis=-1, keepdims=True)
    o_ref[...] = (_dot(p.astype(BF16), v_ref[...]) / l).astype(BF16)


def _meta_attention(qkv_m, meta_bias, *, batch):
    d = qkv_m.shape[1] // 3
    h = d // HEAD_DIM
    spec = lambda off: pl.BlockSpec((N_META, HEAD_DIM), lambda b, hh: (b, off + hh))
    return pl.pallas_call(
        _meta_attn_kernel,
        grid=(batch, h),
        in_specs=[spec(0), spec(h), spec(2 * h),
                  pl.BlockSpec((1, 1, N_META), lambda b, hh: (hh, 0, 0))],
        out_specs=spec(0),
        out_shape=jax.ShapeDtypeStruct((batch * N_META, d), BF16),
        compiler_params=_params("arbitrary", "arbitrary"),
        name="meta_attention",
    )(qkv_m, qkv_m, qkv_m, meta_bias.reshape(h, 1, N_META))


def _attention_bias(rel_bias, meta_bias):
    h = rel_bias.shape[0]
    cq = np.arange(GRID_W)
    c0 = np.clip(cq - WIN_COLS // 2, 0, GRID_W - WIN_COLS)
    ck = np.arange(GRID_W)
    in_win = (ck[None, :] >= c0[:, None]) & (ck[None, :] < c0[:, None] + WIN_COLS)
    dc = ck[None, :] - cq[:, None] + WIN_COLS - 1
    by_col = jnp.full((h, 2 * WIN_ROWS - 1, GRID_W, GRID_W), MASK_VALUE, F32)
    for j in range(2 * WIN_COLS - 1):
        by_col = jnp.where((in_win & (dc == j))[None, None], rel_bias[:, :, j, None, None], by_col)
    loc = jnp.stack([by_col[:, WIN_ROWS - 1 - off:2 * WIN_ROWS - 1 - off] for off in range(WIN_ROWS)])
    loc = loc.transpose(0, 1, 3, 2, 4).reshape(WIN_ROWS, h, GRID_W, N_LOC)
    met = jnp.concatenate([meta_bias, jnp.full((h, LANES - N_META), MASK_VALUE, F32)], axis=-1)
    met = jnp.broadcast_to(met[None, :, None, :], (WIN_ROWS, h, GRID_W, LANES))
    return jnp.concatenate([loc, met], axis=-1).astype(F32)


def _proj_kernel(a_ref, w_ref, r_ref, o_ref):
    o_ref[...] = r_ref[...] + _dot(a_ref[...], w_ref[...])


def _proj_residual(a, w, r, *, bm):
    m, d = r.shape
    bm = min(bm, m)
    row = pl.BlockSpec((bm, d), lambda i: (i, 0))
    return pl.pallas_call(
        _proj_kernel,
        grid=(m // bm,),
        in_specs=[row, pl.BlockSpec((d, d), lambda i: (0, 0)), row],
        out_specs=row,
        out_shape=jax.ShapeDtypeStruct((m, d), F32),
        compiler_params=_params("arbitrary"),
        name="attn_out_proj",
    )(a, w, r)


def _conv_proj_kernel(bg_ref, z_ref, zp_ref, zn_ref, zm_ref, cw_ref, w_ref, r_ref, gr_ref, wr_hi_ref, wr_lo_ref,
                      o_ref, route_ref, *, blocks_per_seq, n_experts):
    i = pl.program_id(0)
    z = z_ref[...].astype(F32)
    bm = z.shape[0]
    first = (i % blocks_per_seq) == 0
    last = (i % blocks_per_seq) == blocks_per_seq - 1
    prev_row = jnp.where(first, zm_ref[0], zp_ref[BF16_SUBLANES - 1:BF16_SUBLANES, :].astype(F32))
    next_row = jnp.where(last, 0.0, zn_ref[0:1, :].astype(F32))
    row = lax.broadcasted_iota(jnp.int32, z.shape, 0)
    z_prev = jnp.where(row == 0, prev_row, pltpu.roll(z, 1, axis=0))
    z_next = jnp.where(row == bm - 1, next_row, pltpu.roll(z, bm - 1, axis=0))
    y = z_prev * cw_ref[0:1, :] + z * cw_ref[1:2, :] + z_next * cw_ref[2:3, :]
    a = (bg_ref[...].astype(F32) * y).astype(BF16)
    for s in range(0, bm, bm // CONV_ROW_CHUNKS):
        rows = slice(s, s + bm // CONV_ROW_CHUNKS)
        h = r_ref[rows, :] + _dot(a[rows, :], w_ref[...])
        o_ref[rows, :] = h
        route_ref[rows, :] = _route(h, gr_ref[...], wr_hi_ref[...], wr_lo_ref[...], n_experts)


def _conv_proj_residual(bg, z, z_meta_last, conv_w_t, w, r, g_route, wr_hi, wr_lo, *, seq, bm, n_experts):
    m, d = r.shape
    bm = min(bm, seq)
    bps = seq // bm
    halo = bm // BF16_SUBLANES
    n_halo = m // BF16_SUBLANES
    row = pl.BlockSpec((bm, d), lambda i: (i, 0))
    const = lambda shape: pl.BlockSpec(shape, lambda i: (0,) * len(shape), pipeline_mode=pl.Buffered(1))
    return pl.pallas_call(
        functools.partial(_conv_proj_kernel, blocks_per_seq=bps, n_experts=n_experts),
        grid=(m // bm,),
        in_specs=[
            row, row,
            pl.BlockSpec((BF16_SUBLANES, d), lambda i: (jnp.maximum(i * halo - 1, 0), 0)),
            pl.BlockSpec((BF16_SUBLANES, d), lambda i: (jnp.minimum((i + 1) * halo, n_halo - 1), 0)),
            pl.BlockSpec((1, 1, d), lambda i: (i // bps, 0, 0)),
            const((3, d)),
            const((d, d)),
            row,
            const((1, d)),
            const((d, LANES)),
            const((d, LANES)),
        ],
        out_specs=[row, pl.BlockSpec((bm, LANES), lambda i: (i, 0))],
        out_shape=[jax.ShapeDtypeStruct((m, d), F32), jax.ShapeDtypeStruct((m, LANES), F32)],
        compiler_params=_params("arbitrary"),
        name="conv_out_proj",
    )(bg, z, z, z, z_meta_last, conv_w_t, w, r, g_route, wr_hi, wr_lo)


def _win_kernel(h_ref, g_ref, wb_ref, wc_ref, wu_ref, bg_ref, z_ref, xn_ref):
    @pl.when(pl.program_id(1) == 0)
    def _():
        xn_ref[...] = _rms(h_ref[...], g_ref[...]).astype(BF16)

    xn = xn_ref[...]
    bg_ref[...] = _dot(xn, wb_ref[...]).astype(BF16)
    z_ref[...] = (_dot(xn, wc_ref[...]) * _dot(xn, wu_ref[...])).astype(BF16)


def _conv_in_proj(h, g, w_in, *, bm, bn, rider=None):
    m, d = h.shape
    bm, bn = min(bm, m), min(bn, d)
    nb = d // bn
    wspec = lambda part: pl.BlockSpec((d, bn), lambda i, j: (0, part * nb + j))
    out = pl.BlockSpec((bm, bn), lambda i, j: (i, j))
    return _hosted_call(
        _win_kernel,
        rider=rider, rider_blocks=(m // bm, nb), rider_index=lambda i, j: (i, j),
        grid=(m // bm, nb),
        in_specs=[pl.BlockSpec((bm, d), lambda i, j: (i, 0)), pl.BlockSpec((1, d), lambda i, j: (0, 0)),
                  wspec(0), wspec(1), wspec(2)],
        out_specs=[out, out],
        out_shape=[jax.ShapeDtypeStruct((m, d), BF16), jax.ShapeDtypeStruct((m, d), BF16)],
        args=[h, g, w_in, w_in, w_in],
        scratch_shapes=[pltpu.VMEM((bm, d), BF16)],
        compiler_params=_params("arbitrary", "arbitrary"),
        name="conv_in_proj",
    )


def _ffn_kernel(h_ref, g_ref, wg_ref, wu_ref, wd_ref, o_ref, xn_ref):
    f = pl.program_id(1)

    @pl.when(f == 0)
    def _():
        xn_ref[...] = _rms(h_ref[...], g_ref[...]).astype(BF16)
        o_ref[...] = h_ref[...]

    xn = xn_ref[...]
    a = (jax.nn.silu(_dot(xn, wg_ref[...])) * _dot(xn, wu_ref[...])).astype(BF16)
    o_ref[...] += _dot(a, wd_ref[...])


def _dense_ffn(h, g, wg, wu, wd, *, bm, bf, rider=None):
    m, d = h.shape
    ff = wg.shape[1]
    bm, bf = min(bm, m), min(bf, ff)
    row = pl.BlockSpec((bm, d), lambda i, f: (i, 0))
    return _hosted_call(
        _ffn_kernel,
        rider=rider, rider_blocks=(m // bm, ff // bf), rider_index=lambda i, f: (i, f),
        grid=(m // bm, ff // bf),
        in_specs=[row, pl.BlockSpec((1, d), lambda i, f: (0, 0)),
                  pl.BlockSpec((d, bf), lambda i, f: (0, f)),
                  pl.BlockSpec((d, bf), lambda i, f: (0, f)),
                  pl.BlockSpec((bf, d), lambda i, f: (f, 0))],
        out_specs=[row],
        out_shape=[jax.ShapeDtypeStruct((m, d), F32)],
        args=[h, g, wg, wu, wd],
        scratch_shapes=[pltpu.VMEM((bm, d), BF16)],
        compiler_params=_params("arbitrary", "arbitrary"),
        name="dense_swiglu",
    )


def _route(h, g, wr_hi, wr_lo, n_experts):
    xn = _rms(h, g)
    x_hi = xn.astype(BF16)
    x_lo = (xn - x_hi.astype(F32)).astype(BF16)
    logits = _dot(x_hi, wr_hi) + (_dot(x_lo, wr_hi) + _dot(x_hi, wr_lo))
    lane = lax.broadcasted_iota(jnp.int32, logits.shape, 1)
    neg = -jnp.inf
    l1 = jnp.where(lane < n_experts, logits, neg)
    m1 = jnp.max(l1, axis=-1, keepdims=True)
    i1 = jnp.min(jnp.where(l1 == m1, lane, LANES), axis=-1, keepdims=True)
    l2 = jnp.where(lane == i1, neg, l1)
    m2 = jnp.max(l2, axis=-1, keepdims=True)
    i2 = jnp.min(jnp.where(l2 == m2, lane, LANES), axis=-1, keepdims=True)
    t = jnp.exp(m2 - m1)
    g1 = 1.0 / (1.0 + t)
    g2 = t / (1.0 + t)
    return jnp.where(lane == 0, i1.astype(F32),
                     jnp.where(lane == 1, i2.astype(F32),
                               jnp.where(lane == 2, g1, jnp.where(lane == 3, g2, 0.0))))


def _expert_kernel(be_ref, nreal_ref, tok_ref, tok_next_ref, h_ref, g_ref, wg_ref, wu_ref, wd_ref, o_ref,
                   xg_ref, xn_ref, sem, *, rows_per_step):
    i = pl.program_id(0)
    f = pl.program_id(1)
    nblk, nf = pl.num_programs(0), pl.num_programs(1)
    n_real = nreal_ref[0]
    bm = xn_ref.shape[0]
    n_gather = tok_ref.shape[2]

    def row_copy(tok, r):
        return pltpu.make_async_copy(h_ref.at[pl.ds(tok, 1)], xg_ref.at[pl.ds(r, 1)], sem)

    def wait_gather():
        def body(r, c):
            row_copy(0, r).wait()
            return c
        lax.fori_loop(0, n_gather, body, 0, unroll=DMA_LOOP_UNROLL)

    @pl.when(jnp.logical_and(i == 0, f == 0))
    def _():
        def body(r, c):
            row_copy(tok_ref[0, 0, r], r).start()
            return c
        lax.fori_loop(0, n_gather, body, 0, unroll=DMA_LOOP_UNROLL)

    @pl.when(jnp.logical_and(f == 0, i <= n_real))
    def _():
        wait_gather()

    @pl.when(i < n_real)
    def _():
        @pl.when(f == 0)
        def _():
            xn_ref[...] = _rms(xg_ref[:bm, :], g_ref[...]).astype(BF16)
            o_ref[...] = jnp.zeros_like(o_ref)

        for u in range(rows_per_step):
            r = f * rows_per_step + u
            row_copy(tok_next_ref[0, 0, r], r).start()

        xn = xn_ref[...]
        a = (jax.nn.silu(_dot(xn, wg_ref[0])) * _dot(xn, wu_ref[0])).astype(BF16)
        o_ref[...] += _dot(a, wd_ref[0])

        @pl.when(jnp.logical_and(i == nblk - 1, f == nf - 1))
        def _():
            wait_gather()

    @pl.when(jnp.logical_and(i >= n_real, f == nf - 1))
    def _():
        o_ref[...] = jnp.zeros_like(o_ref)


def _expert_ffn(block_expert, n_real, buf_tok, h, g, wg, wu, wd, *, bm, bf):
    d = h.shape[1]
    ff = wg.shape[2]
    bf = min(bf, ff)
    nblk, nf = buf_tok.shape[0] // bm, ff // bf
    rows_per_step = pl.cdiv(bm, nf)
    n_gather = rows_per_step * nf
    tok = jnp.pad(buf_tok.reshape(nblk, 1, bm), ((0, 0), (0, 0), (0, n_gather - bm)))

    def blk(i, nr):
        return jnp.minimum(i, nr[0] - 1)

    def fblk(i, f, nr):
        return jnp.where(i < nr[0], f, nf - 1)

    return pl.pallas_call(
        functools.partial(_expert_kernel, rows_per_step=rows_per_step),
        grid_spec=pltpu.PrefetchScalarGridSpec(
            num_scalar_prefetch=2,
            grid=(nblk, nf),
            in_specs=[
                pl.BlockSpec((1, 1, n_gather), lambda i, f, be, nr: (i, 0, 0), memory_space=pltpu.SMEM),
                pl.BlockSpec((1, 1, n_gather), lambda i, f, be, nr: (jnp.minimum(i + 1, nblk - 1), 0, 0),
                             memory_space=pltpu.SMEM),
                pl.BlockSpec(memory_space=pl.ANY),
                pl.BlockSpec((1, d), lambda i, f, be, nr: (0, 0)),
                pl.BlockSpec((1, d, bf), lambda i, f, be, nr: (be[blk(i, nr)], 0, fblk(i, f, nr))),
                pl.BlockSpec((1, d, bf), lambda i, f, be, nr: (be[blk(i, nr)], 0, fblk(i, f, nr))),
                pl.BlockSpec((1, bf, d), lambda i, f, be, nr: (be[blk(i, nr)], fblk(i, f, nr), 0)),
            ],
            out_specs=pl.BlockSpec((bm, d), lambda i, f, be, nr: (i, 0)),
            scratch_shapes=[pltpu.VMEM((pl.cdiv(n_gather, F32_SUBLANES) * F32_SUBLANES, d), F32), pltpu.VMEM((bm, d), BF16),
                            pltpu.SemaphoreType.DMA(())],
        ),
        out_shape=jax.ShapeDtypeStruct((nblk * bm, d), F32),
        compiler_params=_params("arbitrary", "arbitrary"),
        name="moe_experts",
    )(block_expert, n_real, tok, tok, h, g, wg, wu, wd)


def _combine_kernel(p0_ref, p1_ref, p0_next_ref, p1_next_ref, h_ref, route_ref, y_ref, o_ref, ybuf, sem, *, tm):
    i = pl.program_id(0)
    slot = i % 2

    def row_copy(src_row, s, k, t, u):
        return pltpu.make_async_copy(y_ref.at[pl.ds(src_row, 1)], ybuf.at[s, k, t, pl.ds(u, 1)], sem.at[s])

    def start_gather(a_ref, b_ref, s):
        def body(t, c):
            for u in range(F32_SUBLANES):
                row_copy(a_ref[0, 0, t * F32_SUBLANES + u], s, 0, t, u).start()
                row_copy(b_ref[0, 0, t * F32_SUBLANES + u], s, 1, t, u).start()
            return c
        lax.fori_loop(0, tm // F32_SUBLANES, body, 0)

    def wait_gather(s):
        def body(t, c):
            for u in range(F32_SUBLANES):
                row_copy(0, s, 0, t, u).wait()
                row_copy(0, s, 1, t, u).wait()
            return c
        lax.fori_loop(0, tm // F32_SUBLANES, body, 0)

    @pl.when(i == 0)
    def _():
        start_gather(p0_ref, p1_ref, 0)

    @pl.when(i + 1 < pl.num_programs(0))
    def _():
        start_gather(p0_next_ref, p1_next_ref, 1 - slot)

    wait_gather(slot)
    g0 = route_ref[:, TOP_K:TOP_K + 1]
    g1 = route_ref[:, TOP_K + 1:TOP_K + 2]
    y0 = ybuf[slot, 0].reshape(tm, -1)
    y1 = ybuf[slot, 1].reshape(tm, -1)
    o_ref[...] = h_ref[...] + (y0 * g0 + y1 * g1)


def _combine(p0, p1, h, route, y, *, tm):
    m, d = h.shape
    tm = min(tm, m)
    nt = m // tm
    idx = pl.BlockSpec((1, 1, tm), lambda i: (i, 0, 0), memory_space=pltpu.SMEM)
    idx_next = pl.BlockSpec((1, 1, tm), lambda i: (jnp.minimum(i + 1, nt - 1), 0, 0), memory_space=pltpu.SMEM)
    row = pl.BlockSpec((tm, d), lambda i: (i, 0))
    p0, p1 = p0.reshape(nt, 1, tm), p1.reshape(nt, 1, tm)
    return pl.pallas_call(
        functools.partial(_combine_kernel, tm=tm),
        grid=(nt,),
        in_specs=[idx, idx, idx_next, idx_next, row, pl.BlockSpec((tm, LANES), lambda i: (i, 0)),
                  pl.BlockSpec(memory_space=pl.ANY)],
        out_specs=row,
        out_shape=jax.ShapeDtypeStruct((m, d), F32),
        scratch_shapes=[pltpu.VMEM((2, 2, tm // F32_SUBLANES, F32_SUBLANES, d), F32),
                        pltpu.SemaphoreType.DMA((2,))],
        compiler_params=_params("arbitrary"),
        name="moe_combine",
    )(p0, p1, p0, p1, h, route, y)


def _route_plan(route, *, n_experts, bm):
    n = route.shape[0]
    flat_e = route[:, :TOP_K].astype(jnp.int32).reshape(-1)
    onehot = (flat_e[:, None] == jnp.arange(n_experts, dtype=jnp.int32)[None, :]).astype(jnp.int32)
    csum = jnp.cumsum(onehot, axis=0)
    rank = jnp.take_along_axis(csum, flat_e[:, None], axis=1)[:, 0] - 1
    counts = csum[-1]
    blocks = (counts + bm - 1) // bm
    blk_end = jnp.cumsum(blocks)
    blk_start = blk_end - blocks
    rows = blk_start[flat_e] * bm + rank
    nblk = -(-(n * TOP_K) // bm) + n_experts
    block_expert = jnp.minimum(jnp.searchsorted(blk_end, jnp.arange(nblk), side="right"),
                               n_experts - 1).astype(jnp.int32)
    buf_tok = jnp.zeros((nblk * bm,), jnp.int32).at[rows].set(jnp.arange(n * TOP_K, dtype=jnp.int32) // TOP_K)
    n_real = blk_end[-1:].astype(jnp.int32)
    return block_expert, n_real, buf_tok, rows[0::TOP_K], rows[1::TOP_K]


def kernel(x, meta_tokens, attn_norm, w_qkv, q_gain, k_gain, rel_bias, meta_bias, w_o, conv_norm, w_in, conv_w,
           w_out, dense_norm, w_gate, w_up, w_down, moe_norm, w_router, moe_gate, moe_up, moe_down):
    batch, seq, d = x.shape
    n = batch * seq
    h = d // HEAD_DIM
    n_experts = w_router.shape[-1]
    assert d % HEAD_DIM == 0 and seq % GRID_W == 0 and seq // GRID_W >= WIN_ROWS
    assert meta_tokens.shape[0] == N_META and rel_bias.shape[1:] == (h, 2 * WIN_ROWS - 1, 2 * WIN_COLS - 1)
    assert all(p.shape[0] == 1 for p in (attn_norm, conv_norm, dense_norm, moe_norm)), "two-layer trunk only"

    xt = x.reshape(n, d)
    mt = jnp.broadcast_to(meta_tokens[None].astype(x.dtype), (batch, N_META, d)).reshape(batch * N_META, d)
    nm = batch * N_META

    g_attn = attn_norm[0].reshape(1, d)
    wqkv = w_qkv[0].astype(BF16)
    head_gain = jnp.concatenate([jnp.tile(q_gain[0] * (HEAD_DIM ** -0.5), h), jnp.tile(k_gain[0], h),
                                 jnp.ones((d,), F32)]).reshape(1, 3 * d)
    qkv = _qkv(xt, g_attn, wqkv, head_gain, bm=QKV_BLOCK[0], bn=QKV_BLOCK[1])
    qkv_m = _qkv(mt, g_attn, wqkv, head_gain, bm=nm, bn=QKV_BLOCK[1])

    def meta_heads(part):
        t = qkv_m[:, part * d:(part + 1) * d].reshape(batch, N_META, h, HEAD_DIM).transpose(0, 2, 1, 3)
        return jnp.pad(t, ((0, 0), (0, 0), (0, LANES - N_META), (0, 0)))

    bias = _attention_bias(rel_bias[0], meta_bias[0])
    n_e, _, d_e = moe_gate.shape[1:]
    o, moe_gate_bf = _attention(qkv, meta_heads(1), meta_heads(2), bias, batch=batch, seq=seq, rb=ATTN_ROWS_PER_STEP,
                                rider=moe_gate[0].reshape(n_e * d, d_e))
    o_m = _meta_attention(qkv_m, meta_bias[0], batch=batch)
    wo = w_o[0].astype(BF16)
    h1 = _proj_residual(o, wo, xt, bm=PROJ_BLOCK_M)
    h1_m = _proj_residual(o_m, wo, mt, bm=nm)

    g_dense = dense_norm[0].reshape(1, d)
    wg, wu, wd = w_gate[0].astype(BF16), w_up[0].astype(BF16), w_down[0].astype(BF16)
    h2, moe_up_bf = _dense_ffn(h1, g_dense, wg, wu, wd, bm=FFN_BLOCK_M, bf=FFN_BLOCK_F,
                               rider=moe_up[0].reshape(n_e * d, d_e))
    (h2_m,) = _dense_ffn(h1_m, g_dense, wg, wu, wd, bm=nm, bf=FFN_BLOCK_F)

    g_conv = conv_norm[0].reshape(1, d)
    win = w_in[0].astype(BF16)
    bg, z, moe_down_bf = _conv_in_proj(h2, g_conv, win, bm=CONV_IN_BLOCK[0], bn=CONV_IN_BLOCK[1],
                                       rider=moe_down[0].reshape(n_e * d_e, d))
    _, z_m = _conv_in_proj(h2_m, g_conv, win, bm=nm, bn=CONV_IN_BLOCK[1])
    z_meta_last = z_m.reshape(batch, N_META, d)[:, N_META - 1:, :].astype(F32)
    g_moe = moe_norm[0].reshape(1, d)
    wr_pad = jnp.pad(w_router[0], ((0, 0), (0, LANES - n_experts)))
    wr_hi = wr_pad.astype(BF16)
    wr_lo = (wr_pad - wr_hi.astype(F32)).astype(BF16)
    h3, route = _conv_proj_residual(bg, z, z_meta_last, conv_w[0].T, w_out[0].astype(BF16), h2, g_moe, wr_hi, wr_lo,
                                    seq=seq, bm=PROJ_BLOCK_M, n_experts=n_experts)

    block_expert, n_real, buf_tok, p0, p1 = _route_plan(route, n_experts=n_experts, bm=FFN_BLOCK_M)
    ybuf = _expert_ffn(block_expert, n_real, buf_tok, h3, g_moe, moe_gate_bf.reshape(n_e, d, d_e),
                       moe_up_bf.reshape(n_e, d, d_e), moe_down_bf.reshape(n_e, d_e, d),
                       bm=FFN_BLOCK_M, bf=FFN_BLOCK_F)
    out = _combine(p0, p1, h3, route, ybuf, tm=COMBINE_BLOCK_M)
    return out.reshape(batch, seq, d)
```

```python
import functools

import numpy as np
import jax
import jax.numpy as jnp
from jax import lax
from jax.experimental import pallas as pl
from jax.experimental.pallas import tpu as pltpu

HEAD_DIM = 128
GRID_W = 64
WIN_ROWS = 8
WIN_COLS = 16
N_META = 16
TOP_K = 2
RMS_EPS = 1e-6
LANES = 128
MXU_WIDTH = 256
F32_SUBLANES = 8
BF16_SUBLANES = 16
VMEM_LIMIT_BYTES = 56 * 1024 * 1024
MASK_VALUE = -1e30
N_LOC = WIN_ROWS * GRID_W
ATTN_ROW_GROUP = 16
DMA_LOOP_UNROLL = 8
CONV_ROW_CHUNKS = 2

QKV_BLOCK = (1024, 2048)
ATTN_ROWS_PER_STEP = 64
PROJ_BLOCK_M = 512
FFN_BLOCK_M = 512
FFN_BLOCK_F = 1024
CONV_IN_BLOCK = (1024, 512)
COMBINE_BLOCK_M = 256

F32 = jnp.float32
BF16 = jnp.bfloat16


def _params(*sem):
    return pltpu.CompilerParams(dimension_semantics=sem, vmem_limit_bytes=VMEM_LIMIT_BYTES)


def _rms(xf, g):
    return xf * lax.rsqrt(jnp.mean(xf * xf, axis=-1, keepdims=True) + RMS_EPS) * g


def _dot(a, b):
    return jnp.dot(a, b, preferred_element_type=F32)


def _dot_nt(a, b):
    return lax.dot_general(a, b, (((1,), (1,)), ((), ())), preferred_element_type=F32)


def _with_cast_rider(body, n_in, n_out):
    def wrapped(*refs):
        rider_src, rider_dst = refs[n_in], refs[n_in + 1 + n_out]
        rider_dst[...] = rider_src[...].astype(rider_dst.dtype)
        body(*refs[:n_in], *refs[n_in + 1:n_in + 1 + n_out], *refs[n_in + 2 + n_out:])
    return wrapped


def _hosted_call(body, *, rider, rider_blocks, rider_index, in_specs, out_specs, out_shape, args, **kwargs):
    if rider is not None:
        rows, cols = rider.shape
        nr, nc = rider_blocks
        assert rows % (nr * BF16_SUBLANES) == 0 and cols % (nc * LANES) == 0, (rider.shape, rider_blocks)
        spec = pl.BlockSpec((rows // nr, cols // nc), rider_index)
        body = _with_cast_rider(body, len(in_specs), len(out_specs))
        in_specs, out_specs = in_specs + [spec], out_specs + [spec]
        out_shape = out_shape + [jax.ShapeDtypeStruct(rider.shape, BF16)]
        args = args + [rider]
    return pl.pallas_call(body, in_specs=in_specs, out_specs=out_specs, out_shape=out_shape, **kwargs)(*args)


def _qkv_kernel(x_ref, g_ref, w_ref, hg_ref, o_ref, xn_ref, *, n_norm_blocks):
    j = pl.program_id(1)

    @pl.when(j == 0)
    def _():
        xn_ref[...] = _rms(x_ref[...], g_ref[...]).astype(BF16)

    normed = j < n_norm_blocks
    xn = xn_ref[...]
    for c in range(o_ref.shape[1] // MXU_WIDTH):
        acc = _dot(xn, w_ref[:, c * MXU_WIDTH:(c + 1) * MXU_WIDTH])
        for hh in range(MXU_WIDTH // HEAD_DIM):
            sl = slice(c * MXU_WIDTH + hh * HEAD_DIM, c * MXU_WIDTH + (hh + 1) * HEAD_DIM)
            t = acc[:, hh * HEAD_DIM:(hh + 1) * HEAD_DIM]
            o_ref[:, sl] = jnp.where(normed, _rms(t, hg_ref[:, sl]), t).astype(BF16)


def _qkv(x, g, w, hg, *, bm, bn):
    m, d = x.shape
    n = w.shape[1]
    bm, bn = min(bm, m), min(bn, d)
    return pl.pallas_call(
        functools.partial(_qkv_kernel, n_norm_blocks=2 * d // bn),
        grid=(m // bm, n // bn),
        in_specs=[
            pl.BlockSpec((bm, d), lambda i, j: (i, 0)),
            pl.BlockSpec((1, d), lambda i, j: (0, 0)),
            pl.BlockSpec((d, bn), lambda i, j: (0, j)),
            pl.BlockSpec((1, bn), lambda i, j: (0, j)),
        ],
        out_specs=pl.BlockSpec((bm, bn), lambda i, j: (i, j)),
        out_shape=jax.ShapeDtypeStruct((m, n), BF16),
        scratch_shapes=[pltpu.VMEM((bm, d), BF16)],
        compiler_params=_params("arbitrary", "arbitrary"),
        name="qkv_proj",
    )(x, g, w, hg)


def _attn_kernel(q_ref, k_ref, v_ref, km_ref, vm_ref, bias_ref, o_ref, *, rb, n_rows):
    rblk = pl.program_id(2)
    km = km_ref[0, 0]
    vm = vm_ref[0, 0]

    def group(gi, carry):
        scores, probs = [], []
        for u in range(ATTN_ROW_GROUP):
            rr = gi * ATTN_ROW_GROUP + u
            r = rblk * rb + rr
            r0 = jnp.clip(r - WIN_ROWS // 2, 0, n_rows - WIN_ROWS)
            start = pl.multiple_of(r0 * GRID_W, GRID_W)
            qs = pl.multiple_of(rr * GRID_W, GRID_W)
            q = q_ref[pl.ds(qs, GRID_W), :]
            kw = k_ref[pl.ds(start, N_LOC), :]
            s = jnp.concatenate([_dot_nt(q, kw), _dot_nt(q, km)], axis=-1) + bias_ref[r - r0, 0]
            scores.append((qs, start, s))
        for qs, start, s in scores:
            p = jnp.exp(s - jnp.max(s, axis=-1, keepdims=True))
            probs.append((qs, start, p.astype(BF16), jnp.sum(p, axis=-1, keepdims=True)))
        for qs, start, pb, l in probs:
            vw = v_ref[pl.ds(start, N_LOC), :]
            o = _dot(pb[:, :N_LOC], vw) + _dot(pb[:, N_LOC:], vm)
            o_ref[pl.ds(qs, GRID_W), :] = (o / l).astype(BF16)
        return carry

    lax.fori_loop(0, rb // ATTN_ROW_GROUP, group, 0)


def _attention(qkv, km, vm, bias, *, batch, seq, rb, rider=None):
    n, d3 = qkv.shape
    d = d3 // 3
    h = d // HEAD_DIM
    n_rows = seq // GRID_W
    rb = min(rb, n_rows)
    nrb = n_rows // rb
    kv_spec = lambda off: pl.BlockSpec((seq, HEAD_DIM), lambda b, hh, r: (b, off + hh))
    meta_spec = pl.BlockSpec((1, 1, LANES, HEAD_DIM), lambda b, hh, r: (b, hh, 0, 0))
    return _hosted_call(
        functools.partial(_attn_kernel, rb=rb, n_rows=n_rows),
        rider=rider, rider_blocks=(batch * h * nrb, 1), rider_index=lambda b, hh, r: ((b * h + hh) * nrb + r, 0),
        grid=(batch, h, nrb),
        in_specs=[
            pl.BlockSpec((rb * GRID_W, HEAD_DIM), lambda b, hh, r: (b * nrb + r, hh)),
            kv_spec(h),
            kv_spec(2 * h),
            meta_spec,
            meta_spec,
            pl.BlockSpec((WIN_ROWS, 1, GRID_W, N_LOC + LANES), lambda b, hh, r: (0, hh, 0, 0)),
        ],
        out_specs=[pl.BlockSpec((rb * GRID_W, HEAD_DIM), lambda b, hh, r: (b * nrb + r, hh))],
        out_shape=[jax.ShapeDtypeStruct((n, d), BF16)],
        args=[qkv, qkv, qkv, km, vm, bias],
        compiler_params=_params("arbitrary", "arbitrary", "arbitrary"),
        name="nbr_attention",
    )


def _meta_attn_kernel(q_ref, k_ref, v_ref, mb_ref, o_ref):
    s = _dot_nt(q_ref[...], k_ref[...]) + mb_ref[0]
    p = jnp.exp(s - jnp.max(s, axis=-1, keepdims=True))
    l = jnp.sum(p, axis=-1, keepdims=True)
    o_ref[...] = (_dot(p.astype(BF16), v_ref[...]) / l).astype(BF16)


def _meta_attention(qkv_m, meta_bias, *, batch):
    d = qkv_m.shape[1] // 3
    h = d // HEAD_DIM
    spec = lambda off: pl.BlockSpec((N_META, HEAD_DIM), lambda b, hh: (b, off + hh))
    return pl.pallas_call(
        _meta_attn_kernel,
        grid=(batch, h),
        in_specs=[spec(0), spec(h), spec(2 * h),
                  pl.BlockSpec((1, 1, N_META), lambda b, hh: (hh, 0, 0))],
        out_specs=spec(0),
        out_shape=jax.ShapeDtypeStruct((batch * N_META, d), BF16),
        compiler_params=_params("arbitrary", "arbitrary"),
        name="meta_attention",
    )(qkv_m, qkv_m, qkv_m, meta_bias.reshape(h, 1, N_META))


def _attention_bias(rel_bias, meta_bias):
    h = rel_bias.shape[0]
    cq = np.arange(GRID_W)
    c0 = np.clip(cq - WIN_COLS // 2, 0, GRID_W - WIN_COLS)
    ck = np.arange(GRID_W)
    in_win = (ck[None, :] >= c0[:, None]) & (ck[None, :] < c0[:, None] + WIN_COLS)
    dc = ck[None, :] - cq[:, None] + WIN_COLS - 1
    by_col = jnp.full((h, 2 * WIN_ROWS - 1, GRID_W, GRID_W), MASK_VALUE, F32)
    for j in range(2 * WIN_COLS - 1):
        by_col = jnp.where((in_win & (dc == j))[None, None], rel_bias[:, :, j, None, None], by_col)
    loc = jnp.stack([by_col[:, WIN_ROWS - 1 - off:2 * WIN_ROWS - 1 - off] for off in range(WIN_ROWS)])
    loc = loc.transpose(0, 1, 3, 2, 4).reshape(WIN_ROWS, h, GRID_W, N_LOC)
    met = jnp.concatenate([meta_bias, jnp.full((h, LANES - N_META), MASK_VALUE, F32)], axis=-1)
    met = jnp.broadcast_to(met[None, :, None, :], (WIN_ROWS, h, GRID_W, LANES))
    return jnp.concatenate([loc, met], axis=-1).astype(F32)


def _proj_kernel(a_ref, w_ref, r_ref, o_ref):
    o_ref[...] = r_ref[...] + _dot(a_ref[...], w_ref[...])


def _proj_residual(a, w, r, *, bm):
    m, d = r.shape
    bm = min(bm, m)
    row = pl.BlockSpec((bm, d), lambda i: (i, 0))
    return pl.pallas_call(
        _proj_kernel,
        grid=(m // bm,),
        in_specs=[row, pl.BlockSpec((d, d), lambda i: (0, 0)), row],
        out_specs=row,
        out_shape=jax.ShapeDtypeStruct((m, d), F32),
        compiler_params=_params("arbitrary"),
        name="attn_out_proj",
    )(a, w, r)


def _conv_proj_kernel(bg_ref, z_ref, zp_ref, zn_ref, zm_ref, cw_ref, w_ref, r_ref, gr_ref, wr_hi_ref, wr_lo_ref,
                      o_ref, route_ref, *, blocks_per_seq, n_experts):
    i = pl.program_id(0)
    z = z_ref[...].astype(F32)
    bm = z.shape[0]
    first = (i % blocks_per_seq) == 0
    last = (i % blocks_per_seq) == blocks_per_seq - 1
    prev_row = jnp.where(first, zm_ref[0], zp_ref[BF16_SUBLANES - 1:BF16_SUBLANES, :].astype(F32))
    next_row = jnp.where(last, 0.0, zn_ref[0:1, :].astype(F32))
    row = lax.broadcasted_iota(jnp.int32, z.shape, 0)
    z_prev = jnp.where(row == 0, prev_row, pltpu.roll(z, 1, axis=0))
    z_next = jnp.where(row == bm - 1, next_row, pltpu.roll(z, bm - 1, axis=0))
    y = z_prev * cw_ref[0:1, :] + z * cw_ref[1:2, :] + z_next * cw_ref[2:3, :]
    a = (bg_ref[...].astype(F32) * y).astype(BF16)
    for s in range(0, bm, bm // CONV_ROW_CHUNKS):
        rows = slice(s, s + bm // CONV_ROW_CHUNKS)
        h = r_ref[rows, :] + _dot(a[rows, :], w_ref[...])
        o_ref[rows, :] = h
        route_ref[rows, :] = _route(h, gr_ref[...], wr_hi_ref[...], wr_lo_ref[...], n_experts)


def _conv_proj_residual(bg, z, z_meta_last, conv_w_t, w, r, g_route, wr_hi, wr_lo, *, seq, bm, n_experts):
    m, d = r.shape
    bm = min(bm, seq)
    bps = seq // bm
    halo = bm // BF16_SUBLANES
    n_halo = m // BF16_SUBLANES
    row = pl.BlockSpec((bm, d), lambda i: (i, 0))
    const = lambda shape: pl.BlockSpec(shape, lambda i: (0,) * len(shape), pipeline_mode=pl.Buffered(1))
    return pl.pallas_call(
        functools.partial(_conv_proj_kernel, blocks_per_seq=bps, n_experts=n_experts),
        grid=(m // bm,),
        in_specs=[
            row, row,
            pl.BlockSpec((BF16_SUBLANES, d), lambda i: (jnp.maximum(i * halo - 1, 0), 0)),
            pl.BlockSpec((BF16_SUBLANES, d), lambda i: (jnp.minimum((i + 1) * halo, n_halo - 1), 0)),
            pl.BlockSpec((1, 1, d), lambda i: (i // bps, 0, 0)),
            const((3, d)),
            const((d, d)),
            row,
            const((1, d)),
            const((d, LANES)),
            const((d, LANES)),
        ],
        out_specs=[row, pl.BlockSpec((bm, LANES), lambda i: (i, 0))],
        out_shape=[jax.ShapeDtypeStruct((m, d), F32), jax.ShapeDtypeStruct((m, LANES), F32)],
        compiler_params=_params("arbitrary"),
        name="conv_out_proj",
    )(bg, z, z, z, z_meta_last, conv_w_t, w, r, g_route, wr_hi, wr_lo)


def _win_kernel(h_ref, g_ref, wb_ref, wc_ref, wu_ref, bg_ref, z_ref, xn_ref):
    @pl.when(pl.program_id(1) == 0)
    def _():
        xn_ref[...] = _rms(h_ref[...], g_ref[...]).astype(BF16)

    xn = xn_ref[...]
    bg_ref[...] = _dot(xn, wb_ref[...]).astype(BF16)
    z_ref[...] = (_dot(xn, wc_ref[...]) * _dot(xn, wu_ref[...])).astype(BF16)


def _conv_in_proj(h, g, w_in, *, bm, bn, rider=None):
    m, d = h.shape
    bm, bn = min(bm, m), min(bn, d)
    nb = d // bn
    wspec = lambda part: pl.BlockSpec((d, bn), lambda i, j: (0, part * nb + j))
    out = pl.BlockSpec((bm, bn), lambda i, j: (i, j))
    return _hosted_call(
        _win_kernel,
        rider=rider, rider_blocks=(m // bm, nb), rider_index=lambda i, j: (i, j),
        grid=(m // bm, nb),
        in_specs=[pl.BlockSpec((bm, d), lambda i, j: (i, 0)), pl.BlockSpec((1, d), lambda i, j: (0, 0)),
                  wspec(0), wspec(1), wspec(2)],
        out_specs=[out, out],
        out_shape=[jax.ShapeDtypeStruct((m, d), BF16), jax.ShapeDtypeStruct((m, d), BF16)],
        args=[h, g, w_in, w_in, w_in],
        scratch_shapes=[pltpu.VMEM((bm, d), BF16)],
        compiler_params=_params("arbitrary", "arbitrary"),
        name="conv_in_proj",
    )


def _ffn_kernel(h_ref, g_ref, wg_ref, wu_ref, wd_ref, o_ref, xn_ref):
    f = pl.program_id(1)

    @pl.when(f == 0)
    def _():
        xn_ref[...] = _rms(h_ref[...], g_ref[...]).astype(BF16)
        o_ref[...] = h_ref[...]

    xn = xn_ref[...]
    a = (jax.nn.silu(_dot(xn, wg_ref[...])) * _dot(xn, wu_ref[...])).astype(BF16)
    o_ref[...] += _dot(a, wd_ref[...])


def _dense_ffn(h, g, wg, wu, wd, *, bm, bf, rider=None):
    m, d = h.shape
    ff = wg.shape[1]
    bm, bf = min(bm, m), min(bf, ff)
    row = pl.BlockSpec((bm, d), lambda i, f: (i, 0))
    return _hosted_call(
        _ffn_kernel,
        rider=rider, rider_blocks=(m // bm, ff // bf), rider_index=lambda i, f: (i, f),
        grid=(m // bm, ff // bf),
        in_specs=[row, pl.BlockSpec((1, d), lambda i, f: (0, 0)),
                  pl.BlockSpec((d, bf), lambda i, f: (0, f)),
                  pl.BlockSpec((d, bf), lambda i, f: (0, f)),
                  pl.BlockSpec((bf, d), lambda i, f: (f, 0))],
        out_specs=[row],
        out_shape=[jax.ShapeDtypeStruct((m, d), F32)],
        args=[h, g, wg, wu, wd],
        scratch_shapes=[pltpu.VMEM((bm, d), BF16)],
        compiler_params=_params("arbitrary", "arbitrary"),
        name="dense_swiglu",
    )


def _route(h, g, wr_hi, wr_lo, n_experts):
    xn = _rms(h, g)
    x_hi = xn.astype(BF16)
    x_lo = (xn - x_hi.astype(F32)).astype(BF16)
    logits = _dot(x_hi, wr_hi) + (_dot(x_lo, wr_hi) + _dot(x_hi, wr_lo))
    lane = lax.broadcasted_iota(jnp.int32, logits.shape, 1)
    neg = -jnp.inf
    l1 = jnp.where(lane < n_experts, logits, neg)
    m1 = jnp.max(l1, axis=-1, keepdims=True)
    i1 = jnp.min(jnp.where(l1 == m1, lane, LANES), axis=-1, keepdims=True)
    l2 = jnp.where(lane == i1, neg, l1)
    m2 = jnp.max(l2, axis=-1, keepdims=True)
    i2 = jnp.min(jnp.where(l2 == m2, lane, LANES), axis=-1, keepdims=True)
    t = jnp.exp(m2 - m1)
    g1 = 1.0 / (1.0 + t)
    g2 = t / (1.0 + t)
    return jnp.where(lane == 0, i1.astype(F32),
                     jnp.where(lane == 1, i2.astype(F32),
                               jnp.where(lane == 2, g1, jnp.where(lane == 3, g2, 0.0))))


def _expert_kernel(be_ref, nreal_ref, tok_ref, tok_next_ref, h_ref, g_ref, wg_ref, wu_ref, wd_ref, o_ref,
                   xg_ref, xn_ref, sem, *, rows_per_step):
    i = pl.program_id(0)
    f = pl.program_id(1)
    nblk, nf = pl.num_programs(0), pl.num_programs(1)
    n_real = nreal_ref[0]
    bm = xn_ref.shape[0]
    n_gather = tok_ref.shape[2]

    def row_copy(tok, r):
        return pltpu.make_async_copy(h_ref.at[pl.ds(tok, 1)], xg_ref.at[pl.ds(r, 1)], sem)

    def wait_gather():
        def body(r, c):
            row_copy(0, r).wait()
            return c
        lax.fori_loop(0, n_gather, body, 0, unroll=DMA_LOOP_UNROLL)

    @pl.when(jnp.logical_and(i == 0, f == 0))
    def _():
        def body(r, c):
            row_copy(tok_ref[0, 0, r], r).start()
            return c
        lax.fori_loop(0, n_gather, body, 0, unroll=DMA_LOOP_UNROLL)

    @pl.when(jnp.logical_and(f == 0, i <= n_real))
    def _():
        wait_gather()

    @pl.when(i < n_real)
    def _():
        @pl.when(f == 0)
        def _():
            xn_ref[...] = _rms(xg_ref[:bm, :], g_ref[...]).astype(BF16)
            o_ref[...] = jnp.zeros_like(o_ref)

        for u in range(rows_per_step):
            r = f * rows_per_step + u
            row_copy(tok_next_ref[0, 0, r], r).start()

        xn = xn_ref[...]
        a = (jax.nn.silu(_dot(xn, wg_ref[0])) * _dot(xn, wu_ref[0])).astype(BF16)
        o_ref[...] += _dot(a, wd_ref[0])

        @pl.when(jnp.logical_and(i == nblk - 1, f == nf - 1))
        def _():
            wait_gather()

    @pl.when(jnp.logical_and(i >= n_real, f == nf - 1))
    def _():
        o_ref[...] = jnp.zeros_like(o_ref)


def _expert_ffn(block_expert, n_real, buf_tok, h, g, wg, wu, wd, *, bm, bf):
    d = h.shape[1]
    ff = wg.shape[2]
    bf = min(bf, ff)
    nblk, nf = buf_tok.shape[0] // bm, ff // bf
    rows_per_step = pl.cdiv(bm, nf)
    n_gather = rows_per_step * nf
    tok = jnp.pad(buf_tok.reshape(nblk, 1, bm), ((0, 0), (0, 0), (0, n_gather - bm)))

    def blk(i, nr):
        return jnp.minimum(i, nr[0] - 1)

    def fblk(i, f, nr):
        return jnp.where(i < nr[0], f, nf - 1)

    return pl.pallas_call(
        functools.partial(_expert_kernel, rows_per_step=rows_per_step),
        grid_spec=pltpu.PrefetchScalarGridSpec(
            num_scalar_prefetch=2,
            grid=(nblk, nf),
            in_specs=[
                pl.BlockSpec((1, 1, n_gather), lambda i, f, be, nr: (i, 0, 0), memory_space=pltpu.SMEM),
                pl.BlockSpec((1, 1, n_gather), lambda i, f, be, nr: (jnp.minimum(i + 1, nblk - 1), 0, 0),
                             memory_space=pltpu.SMEM),
                pl.BlockSpec(memory_space=pl.ANY),
                pl.BlockSpec((1, d), lambda i, f, be, nr: (0, 0)),
                pl.BlockSpec((1, d, bf), lambda i, f, be, nr: (be[blk(i, nr)], 0, fblk(i, f, nr))),
                pl.BlockSpec((1, d, bf), lambda i, f, be, nr: (be[blk(i, nr)], 0, fblk(i, f, nr))),
                pl.BlockSpec((1, bf, d), lambda i, f, be, nr: (be[blk(i, nr)], fblk(i, f, nr), 0)),
            ],
            out_specs=pl.BlockSpec((bm, d), lambda i, f, be, nr: (i, 0)),
            scratch_shapes=[pltpu.VMEM((pl.cdiv(n_gather, F32_SUBLANES) * F32_SUBLANES, d), F32), pltpu.VMEM((bm, d), BF16),
                            pltpu.SemaphoreType.DMA(())],
        ),
        out_shape=jax.ShapeDtypeStruct((nblk * bm, d), F32),
        compiler_params=_params("arbitrary", "arbitrary"),
        name="moe_experts",
    )(block_expert, n_real, tok, tok, h, g, wg, wu, wd)


def _combine_kernel(p0_ref, p1_ref, p0_next_ref, p1_next_ref, h_ref, route_ref, y_ref, o_ref, ybuf, sem, *, tm):
    i = pl.program_id(0)
    slot = i % 2

    def row_copy(src_row, s, k, t, u):
        return pltpu.make_async_copy(y_ref.at[pl.ds(src_row, 1)], ybuf.at[s, k, t, pl.ds(u, 1)], sem.at[s])

    def start_gather(a_ref, b_ref, s):
        def body(t, c):
            for u in range(F32_SUBLANES):
                row_copy(a_ref[0, 0, t * F32_SUBLANES + u], s, 0, t, u).start()
                row_copy(b_ref[0, 0, t * F32_SUBLANES + u], s, 1, t, u).start()
            return c
        lax.fori_loop(0, tm // F32_SUBLANES, body, 0)

    def wait_gather(s):
        def body(t, c):
            for u in range(F32_SUBLANES):
                row_copy(0, s, 0, t, u).wait()
                row_copy(0, s, 1, t, u).wait()
            return c
        lax.fori_loop(0, tm // F32_SUBLANES, body, 0)

    @pl.when(i == 0)
    def _():
        start_gather(p0_ref, p1_ref, 0)

    @pl.when(i + 1 < pl.num_programs(0))
    def _():
        start_gather(p0_next_ref, p1_next_ref, 1 - slot)

    wait_gather(slot)
    g0 = route_ref[:, TOP_K:TOP_K + 1]
    g1 = route_ref[:, TOP_K + 1:TOP_K + 2]
    y0 = ybuf[slot, 0].reshape(tm, -1)
    y1 = ybuf[slot, 1].reshape(tm, -1)
    o_ref[...] = h_ref[...] + (y0 * g0 + y1 * g1)


def _combine(p0, p1, h, route, y, *, tm):
    m, d = h.shape
    tm = min(tm, m)
    nt = m // tm
    idx = pl.BlockSpec((1, 1, tm), lambda i: (i, 0, 0), memory_space=pltpu.SMEM)
    idx_next = pl.BlockSpec((1, 1, tm), lambda i: (jnp.minimum(i + 1, nt - 1), 0, 0), memory_space=pltpu.SMEM)
    row = pl.BlockSpec((tm, d), lambda i: (i, 0))
    p0, p1 = p0.reshape(nt, 1, tm), p1.reshape(nt, 1, tm)
    return pl.pallas_call(
        functools.partial(_combine_kernel, tm=tm),
        grid=(nt,),
        in_specs=[idx, idx, idx_next, idx_next, row, pl.BlockSpec((tm, LANES), lambda i: (i, 0)),
                  pl.BlockSpec(memory_space=pl.ANY)],
        out_specs=row,
        out_shape=jax.ShapeDtypeStruct((m, d), F32),
        scratch_shapes=[pltpu.VMEM((2, 2, tm // F32_SUBLANES, F32_SUBLANES, d), F32),
                        pltpu.SemaphoreType.DMA((2,))],
        compiler_params=_params("arbitrary"),
        name="moe_combine",
    )(p0, p1, p0, p1, h, route, y)


def _route_plan(route, *, n_experts, bm):
    n = route.shape[0]
    flat_e = route[:, :TOP_K].astype(jnp.int32).T.reshape(-1)
    onehot = (flat_e[:, None] == jnp.arange(n_experts, dtype=jnp.int32)[None, :]).astype(jnp.int32)
    csum = jnp.cumsum(onehot, axis=0)
    counts = csum[-1]
    blocks = (counts + bm - 1) // bm
    blk_end = jnp.cumsum(blocks)
    blk_start = blk_end - blocks
    rows = jnp.sum(onehot * (csum - 1 + (blk_start * bm)[None, :]), axis=1)
    nblk = -(-(n * TOP_K) // bm) + n_experts
    block_expert = jnp.minimum(jnp.searchsorted(blk_end, jnp.arange(nblk), side="right"),
                               n_experts - 1).astype(jnp.int32)
    buf_tok = jnp.zeros((nblk * bm,), jnp.int32).at[rows].set(jnp.tile(jnp.arange(n, dtype=jnp.int32), TOP_K))
    n_real = blk_end[-1:].astype(jnp.int32)
    return block_expert, n_real, buf_tok, rows[:n], rows[n:]


def kernel(x, meta_tokens, attn_norm, w_qkv, q_gain, k_gain, rel_bias, meta_bias, w_o, conv_norm, w_in, conv_w,
           w_out, dense_norm, w_gate, w_up, w_down, moe_norm, w_router, moe_gate, moe_up, moe_down):
    batch, seq, d = x.shape
    n = batch * seq
    h = d // HEAD_DIM
    n_experts = w_router.shape[-1]
    assert d % HEAD_DIM == 0 and seq % GRID_W == 0 and seq // GRID_W >= WIN_ROWS
    assert meta_tokens.shape[0] == N_META and rel_bias.shape[1:] == (h, 2 * WIN_ROWS - 1, 2 * WIN_COLS - 1)
    assert all(p.shape[0] == 1 for p in (attn_norm, conv_norm, dense_norm, moe_norm)), "two-layer trunk only"

    xt = x.reshape(n, d)
    mt = jnp.broadcast_to(meta_tokens[None].astype(x.dtype), (batch, N_META, d)).reshape(batch * N_META, d)
    nm = batch * N_META

    g_attn = attn_norm[0].reshape(1, d)
    wqkv = w_qkv[0].astype(BF16)
    head_gain = jnp.concatenate([jnp.tile(q_gain[0] * (HEAD_DIM ** -0.5), h), jnp.tile(k_gain[0], h),
                                 jnp.ones((d,), F32)]).reshape(1, 3 * d)
    qkv = _qkv(xt, g_attn, wqkv, head_gain, bm=QKV_BLOCK[0], bn=QKV_BLOCK[1])
    qkv_m = _qkv(mt, g_attn, wqkv, head_gain, bm=nm, bn=QKV_BLOCK[1])

    def meta_heads(part):
        t = qkv_m[:, part * d:(part + 1) * d].reshape(batch, N_META, h, HEAD_DIM).transpose(0, 2, 1, 3)
        return jnp.pad(t, ((0, 0), (0, 0), (0, LANES - N_META), (0, 0)))

    bias = _attention_bias(rel_bias[0], meta_bias[0])
    n_e, _, d_e = moe_gate.shape[1:]
    o, moe_gate_bf = _attention(qkv, meta_heads(1), meta_heads(2), bias, batch=batch, seq=seq, rb=ATTN_ROWS_PER_STEP,
                                rider=moe_gate[0].reshape(n_e * d, d_e))
    o_m = _meta_attention(qkv_m, meta_bias[0], batch=batch)
    wo = w_o[0].astype(BF16)
    h1 = _proj_residual(o, wo, xt, bm=PROJ_BLOCK_M)
    h1_m = _proj_residual(o_m, wo, mt, bm=nm)

    g_dense = dense_norm[0].reshape(1, d)
    wg, wu, wd = w_gate[0].astype(BF16), w_up[0].astype(BF16), w_down[0].astype(BF16)
    h2, moe_up_bf = _dense_ffn(h1, g_dense, wg, wu, wd, bm=FFN_BLOCK_M, bf=FFN_BLOCK_F,
                               rider=moe_up[0].reshape(n_e * d, d_e))
    (h2_m,) = _dense_ffn(h1_m, g_dense, wg, wu, wd, bm=nm, bf=FFN_BLOCK_F)

    g_conv = conv_norm[0].reshape(1, d)
    win = w_in[0].astype(BF16)
    bg, z, moe_down_bf = _conv_in_proj(h2, g_conv, win, bm=CONV_IN_BLOCK[0], bn=CONV_IN_BLOCK[1],
                                       rider=moe_down[0].reshape(n_e * d_e, d))
    _, z_m = _conv_in_proj(h2_m, g_conv, win, bm=nm, bn=CONV_IN_BLOCK[1])
    z_meta_last = z_m.reshape(batch, N_META, d)[:, N_META - 1:, :].astype(F32)
    g_moe = moe_norm[0].reshape(1, d)
    wr_pad = jnp.pad(w_router[0], ((0, 0), (0, LANES - n_experts)))
    wr_hi = wr_pad.astype(BF16)
    wr_lo = (wr_pad - wr_hi.astype(F32)).astype(BF16)
    h3, route = _conv_proj_residual(bg, z, z_meta_last, conv_w[0].T, w_out[0].astype(BF16), h2, g_moe, wr_hi, wr_lo,
                                    seq=seq, bm=PROJ_BLOCK_M, n_experts=n_experts)

    block_expert, n_real, buf_tok, p0, p1 = _route_plan(route, n_experts=n_experts, bm=FFN_BLOCK_M)
    ybuf = _expert_ffn(block_expert, n_real, buf_tok, h3, g_moe, moe_gate_bf.reshape(n_e, d, d_e),
                       moe_up_bf.reshape(n_e, d, d_e), moe_down_bf.reshape(n_e, d_e, d),
                       bm=FFN_BLOCK_M, bf=FFN_BLOCK_F)
    out = _combine(p0, p1, h3, route, ybuf, tm=COMBINE_BLOCK_M)
    return out.reshape(batch, seq, d)
```

```python
import functools

import numpy as np
import jax
import jax.numpy as jnp
from jax import lax
from jax.experimental import pallas as pl
from jax.experimental.pallas import tpu as pltpu

HEAD_DIM = 128
GRID_W = 64
WIN_ROWS = 8
WIN_COLS = 16
N_META = 16
TOP_K = 2
RMS_EPS = 1e-6
LANES = 128
MXU_WIDTH = 256
F32_SUBLANES = 8
BF16_SUBLANES = 16
VMEM_LIMIT_BYTES = 56 * 1024 * 1024
MASK_VALUE = -1e30
N_LOC = WIN_ROWS * GRID_W
ATTN_ROW_GROUP = 16
DMA_LOOP_UNROLL = 8
CONV_ROW_CHUNKS = 2

QKV_BLOCK = (1024, 2048)
ATTN_ROWS_PER_STEP = 64
PROJ_BLOCK_M = 512
FFN_BLOCK_M = 512
FFN_BLOCK_F = 1024
CONV_IN_BLOCK = (1024, 512)
COMBINE_BLOCK_M = 256

F32 = jnp.float32
BF16 = jnp.bfloat16


def _params(*sem):
    return pltpu.CompilerParams(dimension_semantics=sem, vmem_limit_bytes=VMEM_LIMIT_BYTES)


def _rms(xf, g):
    return xf * lax.rsqrt(jnp.mean(xf * xf, axis=-1, keepdims=True) + RMS_EPS) * g


def _dot(a, b):
    return jnp.dot(a, b, preferred_element_type=F32)


def _dot_nt(a, b):
    return lax.dot_general(a, b, (((1,), (1,)), ((), ())), preferred_element_type=F32)


def _with_cast_rider(body, n_in, n_out):
    def wrapped(*refs):
        rider_src, rider_dst = refs[n_in], refs[n_in + 1 + n_out]
        rider_dst[...] = rider_src[...].astype(rider_dst.dtype)
        body(*refs[:n_in], *refs[n_in + 1:n_in + 1 + n_out], *refs[n_in + 2 + n_out:])
    return wrapped


def _hosted_call(body, *, rider, rider_blocks, rider_index, in_specs, out_specs, out_shape, args, **kwargs):
    if rider is not None:
        rows, cols = rider.shape
        nr, nc = rider_blocks
        assert rows % (nr * BF16_SUBLANES) == 0 and cols % (nc * LANES) == 0, (rider.shape, rider_blocks)
        spec = pl.BlockSpec((rows // nr, cols // nc), rider_index)
        body = _with_cast_rider(body, len(in_specs), len(out_specs))
        in_specs, out_specs = in_specs + [spec], out_specs + [spec]
        out_shape = out_shape + [jax.ShapeDtypeStruct(rider.shape, BF16)]
        args = args + [rider]
    return pl.pallas_call(body, in_specs=in_specs, out_specs=out_specs, out_shape=out_shape, **kwargs)(*args)


def _qkv_kernel(x_ref, g_ref, w_ref, hg_ref, o_ref, xn_ref, *, n_norm_blocks):
    j = pl.program_id(1)

    @pl.when(j == 0)
    def _():
        xn_ref[...] = _rms(x_ref[...], g_ref[...]).astype(BF16)

    normed = j < n_norm_blocks
    xn = xn_ref[...]
    for c in range(o_ref.shape[1] // MXU_WIDTH):
        acc = _dot(xn, w_ref[:, c * MXU_WIDTH:(c + 1) * MXU_WIDTH])
        for hh in range(MXU_WIDTH // HEAD_DIM):
            sl = slice(c * MXU_WIDTH + hh * HEAD_DIM, c * MXU_WIDTH + (hh + 1) * HEAD_DIM)
            t = acc[:, hh * HEAD_DIM:(hh + 1) * HEAD_DIM]
            o_ref[:, sl] = jnp.where(normed, _rms(t, hg_ref[:, sl]), t).astype(BF16)


def _qkv(x, g, w, hg, *, bm, bn, rider=None):
    m, d = x.shape
    n = w.shape[1]
    bm, bn = min(bm, m), min(bn, d)
    return _hosted_call(
        functools.partial(_qkv_kernel, n_norm_blocks=2 * d // bn),
        rider=rider, rider_blocks=(m // bm, 1), rider_index=lambda i, j: (i, 0),
        grid=(m // bm, n // bn),
        in_specs=[
            pl.BlockSpec((bm, d), lambda i, j: (i, 0)),
            pl.BlockSpec((1, d), lambda i, j: (0, 0)),
            pl.BlockSpec((d, bn), lambda i, j: (0, j)),
            pl.BlockSpec((1, bn), lambda i, j: (0, j)),
        ],
        out_specs=[pl.BlockSpec((bm, bn), lambda i, j: (i, j))],
        out_shape=[jax.ShapeDtypeStruct((m, n), BF16)],
        args=[x, g, w, hg],
        scratch_shapes=[pltpu.VMEM((bm, d), BF16)],
        compiler_params=_params("arbitrary", "arbitrary"),
        name="qkv_proj",
    )


def _attn_kernel(q_ref, k_ref, v_ref, km_ref, vm_ref, bias_ref, o_ref, *, rb, n_rows):
    rblk = pl.program_id(2)
    km = km_ref[0, 0]
    vm = vm_ref[0, 0]

    def group(gi, carry):
        scores, probs = [], []
        for u in range(ATTN_ROW_GROUP):
            rr = gi * ATTN_ROW_GROUP + u
            r = rblk * rb + rr
            r0 = jnp.clip(r - WIN_ROWS // 2, 0, n_rows - WIN_ROWS)
            start = pl.multiple_of(r0 * GRID_W, GRID_W)
            qs = pl.multiple_of(rr * GRID_W, GRID_W)
            q = q_ref[pl.ds(qs, GRID_W), :]
            kw = k_ref[pl.ds(start, N_LOC), :]
            s = jnp.concatenate([_dot_nt(q, kw), _dot_nt(q, km)], axis=-1) + bias_ref[r - r0, 0]
            scores.append((qs, start, s))
        for qs, start, s in scores:
            p = jnp.exp(s - jnp.max(s, axis=-1, keepdims=True))
            probs.append((qs, start, p.astype(BF16), jnp.sum(p, axis=-1, keepdims=True)))
        for qs, start, pb, l in probs:
            vw = v_ref[pl.ds(start, N_LOC), :]
            o = _dot(pb[:, :N_LOC], vw) + _dot(pb[:, N_LOC:], vm)
            o_ref[pl.ds(qs, GRID_W), :] = (o / l).astype(BF16)
        return carry

    lax.fori_loop(0, rb // ATTN_ROW_GROUP, group, 0)


def _attention(qkv, km, vm, bias, *, batch, seq, rb, rider=None):
    n, d3 = qkv.shape
    d = d3 // 3
    h = d // HEAD_DIM
    n_rows = seq // GRID_W
    rb = min(rb, n_rows)
    nrb = n_rows // rb
    kv_spec = lambda off: pl.BlockSpec((seq, HEAD_DIM), lambda b, hh, r: (b, off + hh))
    meta_spec = pl.BlockSpec((1, 1, LANES, HEAD_DIM), lambda b, hh, r: (b, hh, 0, 0))
    return _hosted_call(
        functools.partial(_attn_kernel, rb=rb, n_rows=n_rows),
        rider=rider, rider_blocks=(batch * h * nrb, 1), rider_index=lambda b, hh, r: ((b * h + hh) * nrb + r, 0),
        grid=(batch, h, nrb),
        in_specs=[
            pl.BlockSpec((rb * GRID_W, HEAD_DIM), lambda b, hh, r: (b * nrb + r, hh)),
            kv_spec(h),
            kv_spec(2 * h),
            meta_spec,
            meta_spec,
            pl.BlockSpec((WIN_ROWS, 1, GRID_W, N_LOC + LANES), lambda b, hh, r: (0, hh, 0, 0)),
        ],
        out_specs=[pl.BlockSpec((rb * GRID_W, HEAD_DIM), lambda b, hh, r: (b * nrb + r, hh))],
        out_shape=[jax.ShapeDtypeStruct((n, d), BF16)],
        args=[qkv, qkv, qkv, km, vm, bias],
        compiler_params=_params("arbitrary", "arbitrary", "arbitrary"),
        name="nbr_attention",
    )


def _meta_attn_kernel(q_ref, k_ref, v_ref, mb_ref, o_ref):
    s = _dot_nt(q_ref[...], k_ref[...]) + mb_ref[0]
    p = jnp.exp(s - jnp.max(s, axis=-1, keepdims=True))
    l = jnp.sum(p, axis=-1, keepdims=True)
    o_ref[...] = (_dot(p.astype(BF16), v_ref[...]) / l).astype(BF16)


def _meta_attention(qkv_m, meta_bias, *, batch):
    d = qkv_m.shape[1] // 3
    h = d // HEAD_DIM
    spec = lambda off: pl.BlockSpec((N_META, HEAD_DIM), lambda b, hh: (b, off + hh))
    return pl.pallas_call(
        _meta_attn_kernel,
        grid=(batch, h),
        in_specs=[spec(0), spec(h), spec(2 * h),
                  pl.BlockSpec((1, 1, N_META), lambda b, hh: (hh, 0, 0))],
        out_specs=spec(0),
        out_shape=jax.ShapeDtypeStruct((batch * N_META, d), BF16),
        compiler_params=_params("arbitrary", "arbitrary"),
        name="meta_attention",
    )(qkv_m, qkv_m, qkv_m, meta_bias.reshape(h, 1, N_META))


def _attention_bias(rel_bias, meta_bias):
    h = rel_bias.shape[0]
    cq = np.arange(GRID_W)
    c0 = np.clip(cq - WIN_COLS // 2, 0, GRID_W - WIN_COLS)
    ck = np.arange(GRID_W)
    in_win = (ck[None, :] >= c0[:, None]) & (ck[None, :] < c0[:, None] + WIN_COLS)
    dc = ck[None, :] - cq[:, None] + WIN_COLS - 1
    by_col = jnp.full((h, 2 * WIN_ROWS - 1, GRID_W, GRID_W), MASK_VALUE, F32)
    for j in range(2 * WIN_COLS - 1):
        by_col = jnp.where((in_win & (dc == j))[None, None], rel_bias[:, :, j, None, None], by_col)
    loc = jnp.stack([by_col[:, WIN_ROWS - 1 - off:2 * WIN_ROWS - 1 - off] for off in range(WIN_ROWS)])
    loc = loc.transpose(0, 1, 3, 2, 4).reshape(WIN_ROWS, h, GRID_W, N_LOC)
    met = jnp.concatenate([meta_bias, jnp.full((h, LANES - N_META), MASK_VALUE, F32)], axis=-1)
    met = jnp.broadcast_to(met[None, :, None, :], (WIN_ROWS, h, GRID_W, LANES))
    return jnp.concatenate([loc, met], axis=-1).astype(F32)


def _proj_kernel(a_ref, w_ref, r_ref, o_ref):
    o_ref[...] = r_ref[...] + _dot(a_ref[...], w_ref[...])


def _proj_residual(a, w, r, *, bm, rider=None):
    m, d = r.shape
    bm = min(bm, m)
    row = pl.BlockSpec((bm, d), lambda i: (i, 0))
    return _hosted_call(
        _proj_kernel,
        rider=rider, rider_blocks=(m // bm, 1), rider_index=lambda i: (i, 0),
        grid=(m // bm,),
        in_specs=[row, pl.BlockSpec((d, d), lambda i: (0, 0)), row],
        out_specs=[row],
        out_shape=[jax.ShapeDtypeStruct((m, d), F32)],
        args=[a, w, r],
        compiler_params=_params("arbitrary"),
        name="attn_out_proj",
    )


def _conv_proj_kernel(bg_ref, z_ref, zp_ref, zn_ref, zm_ref, cw_ref, w_ref, r_ref, gr_ref, wr_hi_ref, wr_lo_ref,
                      o_ref, route_ref, *, blocks_per_seq, n_experts):
    i = pl.program_id(0)
    z = z_ref[...].astype(F32)
    bm = z.shape[0]
    first = (i % blocks_per_seq) == 0
    last = (i % blocks_per_seq) == blocks_per_seq - 1
    prev_row = jnp.where(first, zm_ref[0], zp_ref[BF16_SUBLANES - 1:BF16_SUBLANES, :].astype(F32))
    next_row = jnp.where(last, 0.0, zn_ref[0:1, :].astype(F32))
    row = lax.broadcasted_iota(jnp.int32, z.shape, 0)
    z_prev = jnp.where(row == 0, prev_row, pltpu.roll(z, 1, axis=0))
    z_next = jnp.where(row == bm - 1, next_row, pltpu.roll(z, bm - 1, axis=0))
    y = z_prev * cw_ref[0:1, :] + z * cw_ref[1:2, :] + z_next * cw_ref[2:3, :]
    a = (bg_ref[...].astype(F32) * y).astype(BF16)
    for s in range(0, bm, bm // CONV_ROW_CHUNKS):
        rows = slice(s, s + bm // CONV_ROW_CHUNKS)
        h = r_ref[rows, :] + _dot(a[rows, :], w_ref[...])
        o_ref[rows, :] = h
        route_ref[rows, :] = _route(h, gr_ref[...], wr_hi_ref[...], wr_lo_ref[...], n_experts)


def _conv_proj_residual(bg, z, z_meta_last, conv_w_t, w, r, g_route, wr_hi, wr_lo, *, seq, bm, n_experts):
    m, d = r.shape
    bm = min(bm, seq)
    bps = seq // bm
    halo = bm // BF16_SUBLANES
    n_halo = m // BF16_SUBLANES
    row = pl.BlockSpec((bm, d), lambda i: (i, 0))
    const = lambda shape: pl.BlockSpec(shape, lambda i: (0,) * len(shape), pipeline_mode=pl.Buffered(1))
    return pl.pallas_call(
        functools.partial(_conv_proj_kernel, blocks_per_seq=bps, n_experts=n_experts),
        grid=(m // bm,),
        in_specs=[
            row, row,
            pl.BlockSpec((BF16_SUBLANES, d), lambda i: (jnp.maximum(i * halo - 1, 0), 0)),
            pl.BlockSpec((BF16_SUBLANES, d), lambda i: (jnp.minimum((i + 1) * halo, n_halo - 1), 0)),
            pl.BlockSpec((1, 1, d), lambda i: (i // bps, 0, 0)),
            const((3, d)),
            const((d, d)),
            row,
            const((1, d)),
            const((d, LANES)),
            const((d, LANES)),
        ],
        out_specs=[row, pl.BlockSpec((bm, LANES), lambda i: (i, 0))],
        out_shape=[jax.ShapeDtypeStruct((m, d), F32), jax.ShapeDtypeStruct((m, LANES), F32)],
        compiler_params=_params("arbitrary"),
        name="conv_out_proj",
    )(bg, z, z, z, z_meta_last, conv_w_t, w, r, g_route, wr_hi, wr_lo)


def _win_kernel(h_ref, g_ref, wb_ref, wc_ref, wu_ref, bg_ref, z_ref, xn_ref):
    @pl.when(pl.program_id(1) == 0)
    def _():
        xn_ref[...] = _rms(h_ref[...], g_ref[...]).astype(BF16)

    xn = xn_ref[...]
    bg_ref[...] = _dot(xn, wb_ref[...]).astype(BF16)
    z_ref[...] = (_dot(xn, wc_ref[...]) * _dot(xn, wu_ref[...])).astype(BF16)


def _conv_in_proj(h, g, w_in, *, bm, bn, rider=None):
    m, d = h.shape
    bm, bn = min(bm, m), min(bn, d)
    nb = d // bn
    wspec = lambda part: pl.BlockSpec((d, bn), lambda i, j: (0, part * nb + j))
    out = pl.BlockSpec((bm, bn), lambda i, j: (i, j))
    return _hosted_call(
        _win_kernel,
        rider=rider, rider_blocks=(m // bm, nb), rider_index=lambda i, j: (i, j),
        grid=(m // bm, nb),
        in_specs=[pl.BlockSpec((bm, d), lambda i, j: (i, 0)), pl.BlockSpec((1, d), lambda i, j: (0, 0)),
                  wspec(0), wspec(1), wspec(2)],
        out_specs=[out, out],
        out_shape=[jax.ShapeDtypeStruct((m, d), BF16), jax.ShapeDtypeStruct((m, d), BF16)],
        args=[h, g, w_in, w_in, w_in],
        scratch_shapes=[pltpu.VMEM((bm, d), BF16)],
        compiler_params=_params("arbitrary", "arbitrary"),
        name="conv_in_proj",
    )


def _ffn_kernel(h_ref, g_ref, wg_ref, wu_ref, wd_ref, o_ref, xn_ref):
    f = pl.program_id(1)

    @pl.when(f == 0)
    def _():
        xn_ref[...] = _rms(h_ref[...], g_ref[...]).astype(BF16)
        o_ref[...] = h_ref[...]

    xn = xn_ref[...]
    a = (jax.nn.silu(_dot(xn, wg_ref[...])) * _dot(xn, wu_ref[...])).astype(BF16)
    o_ref[...] += _dot(a, wd_ref[...])


def _dense_ffn(h, g, wg, wu, wd, *, bm, bf, rider=None):
    m, d = h.shape
    ff = wg.shape[1]
    bm, bf = min(bm, m), min(bf, ff)
    row = pl.BlockSpec((bm, d), lambda i, f: (i, 0))
    return _hosted_call(
        _ffn_kernel,
        rider=rider, rider_blocks=(m // bm, ff // bf), rider_index=lambda i, f: (i, f),
        grid=(m // bm, ff // bf),
        in_specs=[row, pl.BlockSpec((1, d), lambda i, f: (0, 0)),
                  pl.BlockSpec((d, bf), lambda i, f: (0, f)),
                  pl.BlockSpec((d, bf), lambda i, f: (0, f)),
                  pl.BlockSpec((bf, d), lambda i, f: (f, 0))],
        out_specs=[row],
        out_shape=[jax.ShapeDtypeStruct((m, d), F32)],
        args=[h, g, wg, wu, wd],
        scratch_shapes=[pltpu.VMEM((bm, d), BF16)],
        compiler_params=_params("arbitrary", "arbitrary"),
        name="dense_swiglu",
    )


def _route(h, g, wr_hi, wr_lo, n_experts):
    xn = _rms(h, g)
    x_hi = xn.astype(BF16)
    x_lo = (xn - x_hi.astype(F32)).astype(BF16)
    logits = _dot(x_hi, wr_hi) + (_dot(x_lo, wr_hi) + _dot(x_hi, wr_lo))
    lane = lax.broadcasted_iota(jnp.int32, logits.shape, 1)
    neg = -jnp.inf
    l1 = jnp.where(lane < n_experts, logits, neg)
    m1 = jnp.max(l1, axis=-1, keepdims=True)
    i1 = jnp.min(jnp.where(l1 == m1, lane, LANES), axis=-1, keepdims=True)
    l2 = jnp.where(lane == i1, neg, l1)
    m2 = jnp.max(l2, axis=-1, keepdims=True)
    i2 = jnp.min(jnp.where(l2 == m2, lane, LANES), axis=-1, keepdims=True)
    t = jnp.exp(m2 - m1)
    g1 = 1.0 / (1.0 + t)
    g2 = t / (1.0 + t)
    return jnp.where(lane == 0, i1.astype(F32),
                     jnp.where(lane == 1, i2.astype(F32),
                               jnp.where(lane == 2, g1, jnp.where(lane == 3, g2, 0.0))))


def _expert_kernel(be_ref, nreal_ref, tok_ref, tok_next_ref, h_ref, g_ref, wg_ref, wu_ref, wd_ref, o_ref,
                   xg_ref, xn_ref, sem, *, rows_per_step):
    i = pl.program_id(0)
    f = pl.program_id(1)
    nblk, nf = pl.num_programs(0), pl.num_programs(1)
    n_real = nreal_ref[0]
    bm = xn_ref.shape[0]
    n_gather = tok_ref.shape[2]

    def row_copy(tok, r):
        return pltpu.make_async_copy(h_ref.at[pl.ds(tok, 1)], xg_ref.at[pl.ds(r, 1)], sem)

    def wait_gather():
        def body(r, c):
            row_copy(0, r).wait()
            return c
        lax.fori_loop(0, n_gather, body, 0, unroll=DMA_LOOP_UNROLL)

    @pl.when(jnp.logical_and(i == 0, f == 0))
    def _():
        def body(r, c):
            row_copy(tok_ref[0, 0, r], r).start()
            return c
        lax.fori_loop(0, n_gather, body, 0, unroll=DMA_LOOP_UNROLL)

    @pl.when(jnp.logical_and(f == 0, i <= n_real))
    def _():
        wait_gather()

    @pl.when(i < n_real)
    def _():
        @pl.when(f == 0)
        def _():
            xn_ref[...] = _rms(xg_ref[:bm, :], g_ref[...]).astype(BF16)
            o_ref[...] = jnp.zeros_like(o_ref)

        for u in range(rows_per_step):
            r = f * rows_per_step + u
            row_copy(tok_next_ref[0, 0, r], r).start()

        xn = xn_ref[...]
        a = (jax.nn.silu(_dot(xn, wg_ref[0])) * _dot(xn, wu_ref[0])).astype(BF16)
        o_ref[...] += _dot(a, wd_ref[0])

        @pl.when(jnp.logical_and(i == nblk - 1, f == nf - 1))
        def _():
            wait_gather()

    @pl.when(jnp.logical_and(i >= n_real, f == nf - 1))
    def _():
        o_ref[...] = jnp.zeros_like(o_ref)


def _expert_ffn(block_expert, n_real, buf_tok, h, g, wg, wu, wd, *, bm, bf):
    d = h.shape[1]
    ff = wg.shape[2]
    bf = min(bf, ff)
    nblk, nf = buf_tok.shape[0] // bm, ff // bf
    rows_per_step = pl.cdiv(bm, nf)
    n_gather = rows_per_step * nf
    tok = jnp.pad(buf_tok.reshape(nblk, 1, bm), ((0, 0), (0, 0), (0, n_gather - bm)))

    def blk(i, nr):
        return jnp.minimum(i, nr[0] - 1)

    def fblk(i, f, nr):
        return jnp.where(i < nr[0], f, nf - 1)

    return pl.pallas_call(
        functools.partial(_expert_kernel, rows_per_step=rows_per_step),
        grid_spec=pltpu.PrefetchScalarGridSpec(
            num_scalar_prefetch=2,
            grid=(nblk, nf),
            in_specs=[
                pl.BlockSpec((1, 1, n_gather), lambda i, f, be, nr: (i, 0, 0), memory_space=pltpu.SMEM),
                pl.BlockSpec((1, 1, n_gather), lambda i, f, be, nr: (jnp.minimum(i + 1, nblk - 1), 0, 0),
                             memory_space=pltpu.SMEM),
                pl.BlockSpec(memory_space=pl.ANY),
                pl.BlockSpec((1, d), lambda i, f, be, nr: (0, 0)),
                pl.BlockSpec((1, d, bf), lambda i, f, be, nr: (be[blk(i, nr)], 0, fblk(i, f, nr))),
                pl.BlockSpec((1, d, bf), lambda i, f, be, nr: (be[blk(i, nr)], 0, fblk(i, f, nr))),
                pl.BlockSpec((1, bf, d), lambda i, f, be, nr: (be[blk(i, nr)], fblk(i, f, nr), 0)),
            ],
            out_specs=pl.BlockSpec((bm, d), lambda i, f, be, nr: (i, 0)),
            scratch_shapes=[pltpu.VMEM((pl.cdiv(n_gather, F32_SUBLANES) * F32_SUBLANES, d), F32), pltpu.VMEM((bm, d), BF16),
                            pltpu.SemaphoreType.DMA(())],
        ),
        out_shape=jax.ShapeDtypeStruct((nblk * bm, d), F32),
        compiler_params=_params("arbitrary", "arbitrary"),
        name="moe_experts",
    )(block_expert, n_real, tok, tok, h, g, wg, wu, wd)


def _combine_kernel(p0_ref, p1_ref, p0_next_ref, p1_next_ref, h_ref, route_ref, y_ref, o_ref, ybuf, sem, *, tm):
    i = pl.program_id(0)
    slot = i % 2

    def row_copy(src_row, s, k, t, u):
        return pltpu.make_async_copy(y_ref.at[pl.ds(src_row, 1)], ybuf.at[s, k, t, pl.ds(u, 1)], sem.at[s])

    def start_gather(a_ref, b_ref, s):
        def body(t, c):
            for u in range(F32_SUBLANES):
                row_copy(a_ref[0, 0, t * F32_SUBLANES + u], s, 0, t, u).start()
                row_copy(b_ref[0, 0, t * F32_SUBLANES + u], s, 1, t, u).start()
            return c
        lax.fori_loop(0, tm // F32_SUBLANES, body, 0)

    def wait_gather(s):
        def body(t, c):
            for u in range(F32_SUBLANES):
                row_copy(0, s, 0, t, u).wait()
                row_copy(0, s, 1, t, u).wait()
            return c
        lax.fori_loop(0, tm // F32_SUBLANES, body, 0)

    @pl.when(i == 0)
    def _():
        start_gather(p0_ref, p1_ref, 0)

    @pl.when(i + 1 < pl.num_programs(0))
    def _():
        start_gather(p0_next_ref, p1_next_ref, 1 - slot)

    wait_gather(slot)
    g0 = route_ref[:, TOP_K:TOP_K + 1]
    g1 = route_ref[:, TOP_K + 1:TOP_K + 2]
    y0 = ybuf[slot, 0].reshape(tm, -1)
    y1 = ybuf[slot, 1].reshape(tm, -1)
    o_ref[...] = h_ref[...] + (y0 * g0 + y1 * g1)


def _combine(p0, p1, h, route, y, *, tm):
    m, d = h.shape
    tm = min(tm, m)
    nt = m // tm
    idx = pl.BlockSpec((1, 1, tm), lambda i: (i, 0, 0), memory_space=pltpu.SMEM)
    idx_next = pl.BlockSpec((1, 1, tm), lambda i: (jnp.minimum(i + 1, nt - 1), 0, 0), memory_space=pltpu.SMEM)
    row = pl.BlockSpec((tm, d), lambda i: (i, 0))
    p0, p1 = p0.reshape(nt, 1, tm), p1.reshape(nt, 1, tm)
    return pl.pallas_call(
        functools.partial(_combine_kernel, tm=tm),
        grid=(nt,),
        in_specs=[idx, idx, idx_next, idx_next, row, pl.BlockSpec((tm, LANES), lambda i: (i, 0)),
                  pl.BlockSpec(memory_space=pl.ANY)],
        out_specs=row,
        out_shape=jax.ShapeDtypeStruct((m, d), F32),
        scratch_shapes=[pltpu.VMEM((2, 2, tm // F32_SUBLANES, F32_SUBLANES, d), F32),
                        pltpu.SemaphoreType.DMA((2,))],
        compiler_params=_params("arbitrary"),
        name="moe_combine",
    )(p0, p1, p0, p1, h, route, y)


def _route_plan(route, *, n_experts, bm):
    n = route.shape[0]
    flat_e = route[:, :TOP_K].astype(jnp.int32).T.reshape(-1)
    onehot = (flat_e[:, None] == jnp.arange(n_experts, dtype=jnp.int32)[None, :]).astype(jnp.int32)
    csum = jnp.cumsum(onehot, axis=0)
    counts = csum[-1]
    blocks = (counts + bm - 1) // bm
    blk_end = jnp.cumsum(blocks)
    blk_start = blk_end - blocks
    rows = jnp.sum(onehot * (csum - 1 + (blk_start * bm)[None, :]), axis=1)
    nblk = -(-(n * TOP_K) // bm) + n_experts
    block_expert = jnp.minimum(jnp.searchsorted(blk_end, jnp.arange(nblk), side="right"),
                               n_experts - 1).astype(jnp.int32)
    buf_tok = jnp.zeros((nblk * bm,), jnp.int32).at[rows].set(jnp.tile(jnp.arange(n, dtype=jnp.int32), TOP_K))
    n_real = blk_end[-1:].astype(jnp.int32)
    return block_expert, n_real, buf_tok, rows[:n], rows[n:]


def kernel(x, meta_tokens, attn_norm, w_qkv, q_gain, k_gain, rel_bias, meta_bias, w_o, conv_norm, w_in, conv_w,
           w_out, dense_norm, w_gate, w_up, w_down, moe_norm, w_router, moe_gate, moe_up, moe_down):
    batch, seq, d = x.shape
    n = batch * seq
    h = d // HEAD_DIM
    n_experts = w_router.shape[-1]
    assert d % HEAD_DIM == 0 and seq % GRID_W == 0 and seq // GRID_W >= WIN_ROWS
    assert meta_tokens.shape[0] == N_META and rel_bias.shape[1:] == (h, 2 * WIN_ROWS - 1, 2 * WIN_COLS - 1)
    assert all(p.shape[0] == 1 for p in (attn_norm, conv_norm, dense_norm, moe_norm)), "two-layer trunk only"

    xt = x.reshape(n, d)
    mt = jnp.broadcast_to(meta_tokens[None].astype(x.dtype), (batch, N_META, d)).reshape(batch * N_META, d)
    nm = batch * N_META

    g_attn = attn_norm[0].reshape(1, d)
    wqkv = w_qkv[0].astype(BF16)
    head_gain = jnp.concatenate([jnp.tile(q_gain[0] * (HEAD_DIM ** -0.5), h), jnp.tile(k_gain[0], h),
                                 jnp.ones((d,), F32)]).reshape(1, 3 * d)
    qkv, wg = _qkv(xt, g_attn, wqkv, head_gain, bm=QKV_BLOCK[0], bn=QKV_BLOCK[1], rider=w_gate[0])
    (qkv_m,) = _qkv(mt, g_attn, wqkv, head_gain, bm=nm, bn=QKV_BLOCK[1])

    def meta_heads(part):
        t = qkv_m[:, part * d:(part + 1) * d].reshape(batch, N_META, h, HEAD_DIM).transpose(0, 2, 1, 3)
        return jnp.pad(t, ((0, 0), (0, 0), (0, LANES - N_META), (0, 0)))

    bias = _attention_bias(rel_bias[0], meta_bias[0])
    n_e, _, d_e = moe_gate.shape[1:]
    o, moe_gate_bf = _attention(qkv, meta_heads(1), meta_heads(2), bias, batch=batch, seq=seq, rb=ATTN_ROWS_PER_STEP,
                                rider=moe_gate[0].reshape(n_e * d, d_e))
    o_m = _meta_attention(qkv_m, meta_bias[0], batch=batch)
    wo = w_o[0].astype(BF16)
    h1, wu = _proj_residual(o, wo, xt, bm=PROJ_BLOCK_M, rider=w_up[0])
    (h1_m,) = _proj_residual(o_m, wo, mt, bm=nm)

    g_dense = dense_norm[0].reshape(1, d)
    wd = w_down[0].astype(BF16)
    h2, moe_up_bf = _dense_ffn(h1, g_dense, wg, wu, wd, bm=FFN_BLOCK_M, bf=FFN_BLOCK_F,
                               rider=moe_up[0].reshape(n_e * d, d_e))
    (h2_m,) = _dense_ffn(h1_m, g_dense, wg, wu, wd, bm=nm, bf=FFN_BLOCK_F)

    g_conv = conv_norm[0].reshape(1, d)
    win = w_in[0].astype(BF16)
    bg, z, moe_down_bf = _conv_in_proj(h2, g_conv, win, bm=CONV_IN_BLOCK[0], bn=CONV_IN_BLOCK[1],
                                       rider=moe_down[0].reshape(n_e * d_e, d))
    _, z_m = _conv_in_proj(h2_m, g_conv, win, bm=nm, bn=CONV_IN_BLOCK[1])
    z_meta_last = z_m.reshape(batch, N_META, d)[:, N_META - 1:, :].astype(F32)
    g_moe = moe_norm[0].reshape(1, d)
    wr_pad = jnp.pad(w_router[0], ((0, 0), (0, LANES - n_experts)))
    wr_hi = wr_pad.astype(BF16)
    wr_lo = (wr_pad - wr_hi.astype(F32)).astype(BF16)
    h3, route = _conv_proj_residual(bg, z, z_meta_last, conv_w[0].T, w_out[0].astype(BF16), h2, g_moe, wr_hi, wr_lo,
                                    seq=seq, bm=PROJ_BLOCK_M, n_experts=n_experts)

    block_expert, n_real, buf_tok, p0, p1 = _route_plan(route, n_experts=n_experts, bm=FFN_BLOCK_M)
    ybuf = _expert_ffn(block_expert, n_real, buf_tok, h3, g_moe, moe_gate_bf.reshape(n_e, d, d_e),
                       moe_up_bf.reshape(n_e, d, d_e), moe_down_bf.reshape(n_e, d_e, d),
                       bm=FFN_BLOCK_M, bf=FFN_BLOCK_F)
    out = _combine(p0, p1, h3, route, ybuf, tm=COMBINE_BLOCK_M)
    return out.reshape(batch, seq, d)
```

```python
import functools

import numpy as np
import jax
import jax.numpy as jnp
from jax import lax
from jax.experimental import pallas as pl
from jax.experimental.pallas import tpu as pltpu

HEAD_DIM = 128
GRID_W = 64
WIN_ROWS = 8
WIN_COLS = 16
N_META = 16
TOP_K = 2
RMS_EPS = 1e-6
LANES = 128
MXU_WIDTH = 256
F32_SUBLANES = 8
BF16_SUBLANES = 16
VMEM_LIMIT_BYTES = 56 * 1024 * 1024
MASK_VALUE = -1e30
N_LOC = WIN_ROWS * GRID_W
ATTN_ROW_GROUP = 16
DMA_LOOP_UNROLL = 8
CONV_ROW_CHUNKS = 2

QKV_BLOCK = (1024, 2048)
ATTN_ROWS_PER_STEP = 128
PROJ_BLOCK_M = 512
FFN_BLOCK_M = 512
FFN_BLOCK_F = 1024
CONV_IN_BLOCK = (1024, 512)
COMBINE_BLOCK_M = 256

F32 = jnp.float32
BF16 = jnp.bfloat16


def _params(*sem):
    return pltpu.CompilerParams(dimension_semantics=sem, vmem_limit_bytes=VMEM_LIMIT_BYTES)


def _rms(xf, g):
    return xf * lax.rsqrt(jnp.mean(xf * xf, axis=-1, keepdims=True) + RMS_EPS) * g


def _dot(a, b):
    return jnp.dot(a, b, preferred_element_type=F32)


def _dot_nt(a, b):
    return lax.dot_general(a, b, (((1,), (1,)), ((), ())), preferred_element_type=F32)


def _with_cast_rider(body, n_in, n_out):
    def wrapped(*refs):
        rider_src, rider_dst = refs[n_in], refs[n_in + 1 + n_out]
        rider_dst[...] = rider_src[...].astype(rider_dst.dtype)
        body(*refs[:n_in], *refs[n_in + 1:n_in + 1 + n_out], *refs[n_in + 2 + n_out:])
    return wrapped


def _hosted_call(body, *, rider, rider_blocks, rider_index, in_specs, out_specs, out_shape, args, **kwargs):
    if rider is not None:
        rows, cols = rider.shape
        nr, nc = rider_blocks
        assert rows % (nr * BF16_SUBLANES) == 0 and cols % (nc * LANES) == 0, (rider.shape, rider_blocks)
        spec = pl.BlockSpec((rows // nr, cols // nc), rider_index)
        body = _with_cast_rider(body, len(in_specs), len(out_specs))
        in_specs, out_specs = in_specs + [spec], out_specs + [spec]
        out_shape = out_shape + [jax.ShapeDtypeStruct(rider.shape, BF16)]
        args = args + [rider]
    return pl.pallas_call(body, in_specs=in_specs, out_specs=out_specs, out_shape=out_shape, **kwargs)(*args)


def _qkv_kernel(x_ref, g_ref, w_ref, hg_ref, o_ref, xn_ref, *, n_norm_blocks):
    j = pl.program_id(1)

    @pl.when(j == 0)
    def _():
        xn_ref[...] = _rms(x_ref[...], g_ref[...]).astype(BF16)

    normed = j < n_norm_blocks
    xn = xn_ref[...]
    for c in range(o_ref.shape[1] // MXU_WIDTH):
        acc = _dot(xn, w_ref[:, c * MXU_WIDTH:(c + 1) * MXU_WIDTH])
        for hh in range(MXU_WIDTH // HEAD_DIM):
            sl = slice(c * MXU_WIDTH + hh * HEAD_DIM, c * MXU_WIDTH + (hh + 1) * HEAD_DIM)
            t = acc[:, hh * HEAD_DIM:(hh + 1) * HEAD_DIM]
            o_ref[:, sl] = jnp.where(normed, _rms(t, hg_ref[:, sl]), t).astype(BF16)


def _qkv(x, g, w, hg, *, bm, bn, rider=None):
    m, d = x.shape
    n = w.shape[1]
    bm, bn = min(bm, m), min(bn, d)
    return _hosted_call(
        functools.partial(_qkv_kernel, n_norm_blocks=2 * d // bn),
        rider=rider, rider_blocks=(m // bm, 1), rider_index=lambda i, j: (i, 0),
        grid=(m // bm, n // bn),
        in_specs=[
            pl.BlockSpec((bm, d), lambda i, j: (i, 0)),
            pl.BlockSpec((1, d), lambda i, j: (0, 0)),
            pl.BlockSpec((d, bn), lambda i, j: (0, j)),
            pl.BlockSpec((1, bn), lambda i, j: (0, j)),
        ],
        out_specs=[pl.BlockSpec((bm, bn), lambda i, j: (i, j))],
        out_shape=[jax.ShapeDtypeStruct((m, n), BF16)],
        args=[x, g, w, hg],
        scratch_shapes=[pltpu.VMEM((bm, d), BF16)],
        compiler_params=_params("arbitrary", "arbitrary"),
        name="qkv_proj",
    )


def _attn_kernel(q_ref, k_ref, v_ref, km_ref, vm_ref, bias_ref, o_ref, *, rb, n_rows):
    rblk = pl.program_id(2)
    km = km_ref[0, 0]
    vm = vm_ref[0, 0]

    def group(gi, carry):
        scores, probs = [], []
        for u in range(ATTN_ROW_GROUP):
            rr = gi * ATTN_ROW_GROUP + u
            r = rblk * rb + rr
            r0 = jnp.clip(r - WIN_ROWS // 2, 0, n_rows - WIN_ROWS)
            start = pl.multiple_of(r0 * GRID_W, GRID_W)
            qs = pl.multiple_of(rr * GRID_W, GRID_W)
            q = q_ref[pl.ds(qs, GRID_W), :]
            kw = k_ref[pl.ds(start, N_LOC), :]
            s = jnp.concatenate([_dot_nt(q, kw), _dot_nt(q, km)], axis=-1) + bias_ref[r - r0, 0]
            scores.append((qs, start, s))
        for qs, start, s in scores:
            p = jnp.exp(s - jnp.max(s, axis=-1, keepdims=True))
            probs.append((qs, start, p.astype(BF16), jnp.sum(p, axis=-1, keepdims=True)))
        for qs, start, pb, l in probs:
            vw = v_ref[pl.ds(start, N_LOC), :]
            o = _dot(pb[:, :N_LOC], vw) + _dot(pb[:, N_LOC:], vm)
            o_ref[pl.ds(qs, GRID_W), :] = (o / l).astype(BF16)
        return carry

    lax.fori_loop(0, rb // ATTN_ROW_GROUP, group, 0)


def _attention(qkv, km, vm, bias, *, batch, seq, rb, rider=None):
    n, d3 = qkv.shape
    d = d3 // 3
    h = d // HEAD_DIM
    n_rows = seq // GRID_W
    rb = min(rb, n_rows)
    nrb = n_rows // rb
    kv_spec = lambda off: pl.BlockSpec((seq, HEAD_DIM), lambda b, hh, r: (b, off + hh))
    meta_spec = pl.BlockSpec((1, 1, LANES, HEAD_DIM), lambda b, hh, r: (b, hh, 0, 0))
    return _hosted_call(
        functools.partial(_attn_kernel, rb=rb, n_rows=n_rows),
        rider=rider, rider_blocks=(batch * h * nrb, 1), rider_index=lambda b, hh, r: ((b * h + hh) * nrb + r, 0),
        grid=(batch, h, nrb),
        in_specs=[
            pl.BlockSpec((rb * GRID_W, HEAD_DIM), lambda b, hh, r: (b * nrb + r, hh)),
            kv_spec(h),
            kv_spec(2 * h),
            meta_spec,
            meta_spec,
            pl.BlockSpec((WIN_ROWS, 1, GRID_W, N_LOC + LANES), lambda b, hh, r: (0, hh, 0, 0)),
        ],
        out_specs=[pl.BlockSpec((rb * GRID_W, HEAD_DIM), lambda b, hh, r: (b * nrb + r, hh))],
        out_shape=[jax.ShapeDtypeStruct((n, d), BF16)],
        args=[qkv, qkv, qkv, km, vm, bias],
        compiler_params=_params("arbitrary", "arbitrary", "arbitrary"),
        name="nbr_attention",
    )


def _meta_attn_kernel(q_ref, k_ref, v_ref, mb_ref, o_ref):
    s = _dot_nt(q_ref[...], k_ref[...]) + mb_ref[0]
    p = jnp.exp(s - jnp.max(s, axis=-1, keepdims=True))
    l = jnp.sum(p, axis=-1, keepdims=True)
    o_ref[...] = (_dot(p.astype(BF16), v_ref[...]) / l).astype(BF16)


def _meta_attention(qkv_m, meta_bias, *, batch):
    d = qkv_m.shape[1] // 3
    h = d // HEAD_DIM
    spec = lambda off: pl.BlockSpec((N_META, HEAD_DIM), lambda b, hh: (b, off + hh))
    return pl.pallas_call(
        _meta_attn_kernel,
        grid=(batch, h),
        in_specs=[spec(0), spec(h), spec(2 * h),
                  pl.BlockSpec((1, 1, N_META), lambda b, hh: (hh, 0, 0))],
        out_specs=spec(0),
        out_shape=jax.ShapeDtypeStruct((batch * N_META, d), BF16),
        compiler_params=_params("arbitrary", "arbitrary"),
        name="meta_attention",
    )(qkv_m, qkv_m, qkv_m, meta_bias.reshape(h, 1, N_META))


def _attention_bias(rel_bias, meta_bias):
    h = rel_bias.shape[0]
    cq = np.arange(GRID_W)
    c0 = np.clip(cq - WIN_COLS // 2, 0, GRID_W - WIN_COLS)
    ck = np.arange(GRID_W)
    in_win = (ck[None, :] >= c0[:, None]) & (ck[None, :] < c0[:, None] + WIN_COLS)
    dc = ck[None, :] - cq[:, None] + WIN_COLS - 1
    by_col = jnp.full((h, 2 * WIN_ROWS - 1, GRID_W, GRID_W), MASK_VALUE, F32)
    for j in range(2 * WIN_COLS - 1):
        by_col = jnp.where((in_win & (dc == j))[None, None], rel_bias[:, :, j, None, None], by_col)
    loc = jnp.stack([by_col[:, WIN_ROWS - 1 - off:2 * WIN_ROWS - 1 - off] for off in range(WIN_ROWS)])
    loc = loc.transpose(0, 1, 3, 2, 4).reshape(WIN_ROWS, h, GRID_W, N_LOC)
    met = jnp.concatenate([meta_bias, jnp.full((h, LANES - N_META), MASK_VALUE, F32)], axis=-1)
    met = jnp.broadcast_to(met[None, :, None, :], (WIN_ROWS, h, GRID_W, LANES))
    return jnp.concatenate([loc, met], axis=-1).astype(F32)


def _proj_kernel(a_ref, w_ref, r_ref, o_ref):
    o_ref[...] = r_ref[...] + _dot(a_ref[...], w_ref[...])


def _proj_residual(a, w, r, *, bm, rider=None):
    m, d = r.shape
    bm = min(bm, m)
    row = pl.BlockSpec((bm, d), lambda i: (i, 0))
    return _hosted_call(
        _proj_kernel,
        rider=rider, rider_blocks=(m // bm, 1), rider_index=lambda i: (i, 0),
        grid=(m // bm,),
        in_specs=[row, pl.BlockSpec((d, d), lambda i: (0, 0)), row],
        out_specs=[row],
        out_shape=[jax.ShapeDtypeStruct((m, d), F32)],
        args=[a, w, r],
        compiler_params=_params("arbitrary"),
        name="attn_out_proj",
    )


def _conv_proj_kernel(bg_ref, z_ref, zp_ref, zn_ref, zm_ref, cw_ref, w_ref, r_ref, gr_ref, wr_hi_ref, wr_lo_ref,
                      o_ref, route_ref, *, blocks_per_seq, n_experts):
    i = pl.program_id(0)
    z = z_ref[...].astype(F32)
    bm = z.shape[0]
    first = (i % blocks_per_seq) == 0
    last = (i % blocks_per_seq) == blocks_per_seq - 1
    prev_row = jnp.where(first, zm_ref[0], zp_ref[BF16_SUBLANES - 1:BF16_SUBLANES, :].astype(F32))
    next_row = jnp.where(last, 0.0, zn_ref[0:1, :].astype(F32))
    row = lax.broadcasted_iota(jnp.int32, z.shape, 0)
    z_prev = jnp.where(row == 0, prev_row, pltpu.roll(z, 1, axis=0))
    z_next = jnp.where(row == bm - 1, next_row, pltpu.roll(z, bm - 1, axis=0))
    y = z_prev * cw_ref[0:1, :] + z * cw_ref[1:2, :] + z_next * cw_ref[2:3, :]
    a = (bg_ref[...].astype(F32) * y).astype(BF16)
    for s in range(0, bm, bm // CONV_ROW_CHUNKS):
        rows = slice(s, s + bm // CONV_ROW_CHUNKS)
        h = r_ref[rows, :] + _dot(a[rows, :], w_ref[...])
        o_ref[rows, :] = h
        route_ref[rows, :] = _route(h, gr_ref[...], wr_hi_ref[...], wr_lo_ref[...], n_experts)


def _conv_proj_residual(bg, z, z_meta_last, conv_w_t, w, r, g_route, wr_hi, wr_lo, *, seq, bm, n_experts):
    m, d = r.shape
    bm = min(bm, seq)
    bps = seq // bm
    halo = bm // BF16_SUBLANES
    n_halo = m // BF16_SUBLANES
    row = pl.BlockSpec((bm, d), lambda i: (i, 0))
    const = lambda shape: pl.BlockSpec(shape, lambda i: (0,) * len(shape), pipeline_mode=pl.Buffered(1))
    return pl.pallas_call(
        functools.partial(_conv_proj_kernel, blocks_per_seq=bps, n_experts=n_experts),
        grid=(m // bm,),
        in_specs=[
            row, row,
            pl.BlockSpec((BF16_SUBLANES, d), lambda i: (jnp.maximum(i * halo - 1, 0), 0)),
            pl.BlockSpec((BF16_SUBLANES, d), lambda i: (jnp.minimum((i + 1) * halo, n_halo - 1), 0)),
            pl.BlockSpec((1, 1, d), lambda i: (i // bps, 0, 0)),
            const((3, d)),
            const((d, d)),
            row,
            const((1, d)),
            const((d, LANES)),
            const((d, LANES)),
        ],
        out_specs=[row, pl.BlockSpec((bm, LANES), lambda i: (i, 0))],
        out_shape=[jax.ShapeDtypeStruct((m, d), F32), jax.ShapeDtypeStruct((m, LANES), F32)],
        compiler_params=_params("arbitrary"),
        name="conv_out_proj",
    )(bg, z, z, z, z_meta_last, conv_w_t, w, r, g_route, wr_hi, wr_lo)


def _win_kernel(h_ref, g_ref, wb_ref, wc_ref, wu_ref, bg_ref, z_ref, xn_ref):
    @pl.when(pl.program_id(1) == 0)
    def _():
        xn_ref[...] = _rms(h_ref[...], g_ref[...]).astype(BF16)

    xn = xn_ref[...]
    bg_ref[...] = _dot(xn, wb_ref[...]).astype(BF16)
    z_ref[...] = (_dot(xn, wc_ref[...]) * _dot(xn, wu_ref[...])).astype(BF16)


def _conv_in_proj(h, g, w_in, *, bm, bn, rider=None):
    m, d = h.shape
    bm, bn = min(bm, m), min(bn, d)
    nb = d // bn
    wspec = lambda part: pl.BlockSpec((d, bn), lambda i, j: (0, part * nb + j))
    out = pl.BlockSpec((bm, bn), lambda i, j: (i, j))
    return _hosted_call(
        _win_kernel,
        rider=rider, rider_blocks=(m // bm, nb), rider_index=lambda i, j: (i, j),
        grid=(m // bm, nb),
        in_specs=[pl.BlockSpec((bm, d), lambda i, j: (i, 0)), pl.BlockSpec((1, d), lambda i, j: (0, 0)),
                  wspec(0), wspec(1), wspec(2)],
        out_specs=[out, out],
        out_shape=[jax.ShapeDtypeStruct((m, d), BF16), jax.ShapeDtypeStruct((m, d), BF16)],
        args=[h, g, w_in, w_in, w_in],
        scratch_shapes=[pltpu.VMEM((bm, d), BF16)],
        compiler_params=_params("arbitrary", "arbitrary"),
        name="conv_in_proj",
    )


def _ffn_kernel(h_ref, g_ref, wg_ref, wu_ref, wd_ref, o_ref, xn_ref):
    f = pl.program_id(1)

    @pl.when(f == 0)
    def _():
        xn_ref[...] = _rms(h_ref[...], g_ref[...]).astype(BF16)
        o_ref[...] = h_ref[...]

    xn = xn_ref[...]
    a = (jax.nn.silu(_dot(xn, wg_ref[...])) * _dot(xn, wu_ref[...])).astype(BF16)
    o_ref[...] += _dot(a, wd_ref[...])


def _dense_ffn(h, g, wg, wu, wd, *, bm, bf, rider=None):
    m, d = h.shape
    ff = wg.shape[1]
    bm, bf = min(bm, m), min(bf, ff)
    row = pl.BlockSpec((bm, d), lambda i, f: (i, 0))
    return _hosted_call(
        _ffn_kernel,
        rider=rider, rider_blocks=(m // bm, ff // bf), rider_index=lambda i, f: (i, f),
        grid=(m // bm, ff // bf),
        in_specs=[row, pl.BlockSpec((1, d), lambda i, f: (0, 0)),
                  pl.BlockSpec((d, bf), lambda i, f: (0, f)),
                  pl.BlockSpec((d, bf), lambda i, f: (0, f)),
                  pl.BlockSpec((bf, d), lambda i, f: (f, 0))],
        out_specs=[row],
        out_shape=[jax.ShapeDtypeStruct((m, d), F32)],
        args=[h, g, wg, wu, wd],
        scratch_shapes=[pltpu.VMEM((bm, d), BF16)],
        compiler_params=_params("arbitrary", "arbitrary"),
        name="dense_swiglu",
    )


def _route(h, g, wr_hi, wr_lo, n_experts):
    xn = _rms(h, g)
    x_hi = xn.astype(BF16)
    x_lo = (xn - x_hi.astype(F32)).astype(BF16)
    logits = _dot(x_hi, wr_hi) + (_dot(x_lo, wr_hi) + _dot(x_hi, wr_lo))
    lane = lax.broadcasted_iota(jnp.int32, logits.shape, 1)
    neg = -jnp.inf
    l1 = jnp.where(lane < n_experts, logits, neg)
    m1 = jnp.max(l1, axis=-1, keepdims=True)
    i1 = jnp.min(jnp.where(l1 == m1, lane, LANES), axis=-1, keepdims=True)
    l2 = jnp.where(lane == i1, neg, l1)
    m2 = jnp.max(l2, axis=-1, keepdims=True)
    i2 = jnp.min(jnp.where(l2 == m2, lane, LANES), axis=-1, keepdims=True)
    t = jnp.exp(m2 - m1)
    g1 = 1.0 / (1.0 + t)
    g2 = t / (1.0 + t)
    return jnp.where(lane == 0, i1.astype(F32),
                     jnp.where(lane == 1, i2.astype(F32),
                               jnp.where(lane == 2, g1, jnp.where(lane == 3, g2, 0.0))))


def _expert_kernel(be_ref, nreal_ref, tok_ref, tok_next_ref, h_ref, g_ref, wg_ref, wu_ref, wd_ref, o_ref,
                   xg_ref, xn_ref, sem, *, rows_per_step):
    i = pl.program_id(0)
    f = pl.program_id(1)
    nblk, nf = pl.num_programs(0), pl.num_programs(1)
    n_real = nreal_ref[0]
    bm = xn_ref.shape[0]
    n_gather = tok_ref.shape[2]

    def row_copy(tok, r):
        return pltpu.make_async_copy(h_ref.at[pl.ds(tok, 1)], xg_ref.at[pl.ds(r, 1)], sem)

    def wait_gather():
        def body(r, c):
            row_copy(0, r).wait()
            return c
        lax.fori_loop(0, n_gather, body, 0, unroll=DMA_LOOP_UNROLL)

    @pl.when(jnp.logical_and(i == 0, f == 0))
    def _():
        def body(r, c):
            row_copy(tok_ref[0, 0, r], r).start()
            return c
        lax.fori_loop(0, n_gather, body, 0, unroll=DMA_LOOP_UNROLL)

    @pl.when(jnp.logical_and(f == 0, i <= n_real))
    def _():
        wait_gather()

    @pl.when(i < n_real)
    def _():
        @pl.when(f == 0)
        def _():
            xn_ref[...] = _rms(xg_ref[:bm, :], g_ref[...]).astype(BF16)
            o_ref[...] = jnp.zeros_like(o_ref)

        for u in range(rows_per_step):
            r = f * rows_per_step + u
            row_copy(tok_next_ref[0, 0, r], r).start()

        xn = xn_ref[...]
        a = (jax.nn.silu(_dot(xn, wg_ref[0])) * _dot(xn, wu_ref[0])).astype(BF16)
        o_ref[...] += _dot(a, wd_ref[0])

        @pl.when(jnp.logical_and(i == nblk - 1, f == nf - 1))
        def _():
            wait_gather()

    @pl.when(jnp.logical_and(i >= n_real, f == nf - 1))
    def _():
        o_ref[...] = jnp.zeros_like(o_ref)


def _expert_ffn(block_expert, n_real, buf_tok, h, g, wg, wu, wd, *, bm, bf):
    d = h.shape[1]
    ff = wg.shape[2]
    bf = min(bf, ff)
    nblk, nf = buf_tok.shape[0] // bm, ff // bf
    rows_per_step = pl.cdiv(bm, nf)
    n_gather = rows_per_step * nf
    tok = jnp.pad(buf_tok.reshape(nblk, 1, bm), ((0, 0), (0, 0), (0, n_gather - bm)))

    def blk(i, nr):
        return jnp.minimum(i, nr[0] - 1)

    def fblk(i, f, nr):
        return jnp.where(i < nr[0], f, nf - 1)

    return pl.pallas_call(
        functools.partial(_expert_kernel, rows_per_step=rows_per_step),
        grid_spec=pltpu.PrefetchScalarGridSpec(
            num_scalar_prefetch=2,
            grid=(nblk, nf),
            in_specs=[
                pl.BlockSpec((1, 1, n_gather), lambda i, f, be, nr: (i, 0, 0), memory_space=pltpu.SMEM),
                pl.BlockSpec((1, 1, n_gather), lambda i, f, be, nr: (jnp.minimum(i + 1, nblk - 1), 0, 0),
                             memory_space=pltpu.SMEM),
                pl.BlockSpec(memory_space=pl.ANY),
                pl.BlockSpec((1, d), lambda i, f, be, nr: (0, 0)),
                pl.BlockSpec((1, d, bf), lambda i, f, be, nr: (be[blk(i, nr)], 0, fblk(i, f, nr))),
                pl.BlockSpec((1, d, bf), lambda i, f, be, nr: (be[blk(i, nr)], 0, fblk(i, f, nr))),
                pl.BlockSpec((1, bf, d), lambda i, f, be, nr: (be[blk(i, nr)], fblk(i, f, nr), 0)),
            ],
            out_specs=pl.BlockSpec((bm, d), lambda i, f, be, nr: (i, 0)),
            scratch_shapes=[pltpu.VMEM((pl.cdiv(n_gather, F32_SUBLANES) * F32_SUBLANES, d), F32), pltpu.VMEM((bm, d), BF16),
                            pltpu.SemaphoreType.DMA(())],
        ),
        out_shape=jax.ShapeDtypeStruct((nblk * bm, d), F32),
        compiler_params=_params("arbitrary", "arbitrary"),
        name="moe_experts",
    )(block_expert, n_real, tok, tok, h, g, wg, wu, wd)


def _combine_kernel(p0_ref, p1_ref, p0_next_ref, p1_next_ref, h_ref, route_ref, y_ref, o_ref, ybuf, sem, *, tm):
    i = pl.program_id(0)
    slot = i % 2

    def row_copy(src_row, s, k, t, u):
        return pltpu.make_async_copy(y_ref.at[pl.ds(src_row, 1)], ybuf.at[s, k, t, pl.ds(u, 1)], sem.at[s])

    def start_gather(a_ref, b_ref, s):
        def body(t, c):
            for u in range(F32_SUBLANES):
                row_copy(a_ref[0, 0, t * F32_SUBLANES + u], s, 0, t, u).start()
                row_copy(b_ref[0, 0, t * F32_SUBLANES + u], s, 1, t, u).start()
            return c
        lax.fori_loop(0, tm // F32_SUBLANES, body, 0)

    def wait_gather(s):
        def body(t, c):
            for u in range(F32_SUBLANES):
                row_copy(0, s, 0, t, u).wait()
                row_copy(0, s, 1, t, u).wait()
            return c
        lax.fori_loop(0, tm // F32_SUBLANES, body, 0)

    @pl.when(i == 0)
    def _():
        start_gather(p0_ref, p1_ref, 0)

    @pl.when(i + 1 < pl.num_programs(0))
    def _():
        start_gather(p0_next_ref, p1_next_ref, 1 - slot)

    wait_gather(slot)
    g0 = route_ref[:, TOP_K:TOP_K + 1]
    g1 = route_ref[:, TOP_K + 1:TOP_K + 2]
    y0 = ybuf[slot, 0].reshape(tm, -1)
    y1 = ybuf[slot, 1].reshape(tm, -1)
    o_ref[...] = h_ref[...] + (y0 * g0 + y1 * g1)


def _combine(p0, p1, h, route, y, *, tm):
    m, d = h.shape
    tm = min(tm, m)
    nt = m // tm
    idx = pl.BlockSpec((1, 1, tm), lambda i: (i, 0, 0), memory_space=pltpu.SMEM)
    idx_next = pl.BlockSpec((1, 1, tm), lambda i: (jnp.minimum(i + 1, nt - 1), 0, 0), memory_space=pltpu.SMEM)
    row = pl.BlockSpec((tm, d), lambda i: (i, 0))
    p0, p1 = p0.reshape(nt, 1, tm), p1.reshape(nt, 1, tm)
    return pl.pallas_call(
        functools.partial(_combine_kernel, tm=tm),
        grid=(nt,),
        in_specs=[idx, idx, idx_next, idx_next, row, pl.BlockSpec((tm, LANES), lambda i: (i, 0)),
                  pl.BlockSpec(memory_space=pl.ANY)],
        out_specs=row,
        out_shape=jax.ShapeDtypeStruct((m, d), F32),
        scratch_shapes=[pltpu.VMEM((2, 2, tm // F32_SUBLANES, F32_SUBLANES, d), F32),
                        pltpu.SemaphoreType.DMA((2,))],
        compiler_params=_params("arbitrary"),
        name="moe_combine",
    )(p0, p1, p0, p1, h, route, y)


def _route_plan(route, *, n_experts, bm):
    n = route.shape[0]
    flat_e = route[:, :TOP_K].astype(jnp.int32).T.reshape(-1)
    onehot = (flat_e[:, None] == jnp.arange(n_experts, dtype=jnp.int32)[None, :]).astype(jnp.int32)
    csum = jnp.cumsum(onehot, axis=0)
    counts = csum[-1]
    blocks = (counts + bm - 1) // bm
    blk_end = jnp.cumsum(blocks)
    blk_start = blk_end - blocks
    rows = jnp.sum(onehot * (csum - 1 + (blk_start * bm)[None, :]), axis=1)
    nblk = -(-(n * TOP_K) // bm) + n_experts
    block_expert = jnp.minimum(jnp.searchsorted(blk_end, jnp.arange(nblk), side="right"),
                               n_experts - 1).astype(jnp.int32)
    buf_tok = jnp.zeros((nblk * bm,), jnp.int32).at[rows].set(jnp.tile(jnp.arange(n, dtype=jnp.int32), TOP_K))
    n_real = blk_end[-1:].astype(jnp.int32)
    return block_expert, n_real, buf_tok, rows[:n], rows[n:]


def kernel(x, meta_tokens, attn_norm, w_qkv, q_gain, k_gain, rel_bias, meta_bias, w_o, conv_norm, w_in, conv_w,
           w_out, dense_norm, w_gate, w_up, w_down, moe_norm, w_router, moe_gate, moe_up, moe_down):
    batch, seq, d = x.shape
    n = batch * seq
    h = d // HEAD_DIM
    n_experts = w_router.shape[-1]
    assert d % HEAD_DIM == 0 and seq % GRID_W == 0 and seq // GRID_W >= WIN_ROWS
    assert meta_tokens.shape[0] == N_META and rel_bias.shape[1:] == (h, 2 * WIN_ROWS - 1, 2 * WIN_COLS - 1)
    assert all(p.shape[0] == 1 for p in (attn_norm, conv_norm, dense_norm, moe_norm)), "two-layer trunk only"

    xt = x.reshape(n, d)
    mt = jnp.broadcast_to(meta_tokens[None].astype(x.dtype), (batch, N_META, d)).reshape(batch * N_META, d)
    nm = batch * N_META

    g_attn = attn_norm[0].reshape(1, d)
    wqkv = w_qkv[0].astype(BF16)
    head_gain = jnp.concatenate([jnp.tile(q_gain[0] * (HEAD_DIM ** -0.5), h), jnp.tile(k_gain[0], h),
                                 jnp.ones((d,), F32)]).reshape(1, 3 * d)
    qkv, wg = _qkv(xt, g_attn, wqkv, head_gain, bm=QKV_BLOCK[0], bn=QKV_BLOCK[1], rider=w_gate[0])
    (qkv_m,) = _qkv(mt, g_attn, wqkv, head_gain, bm=nm, bn=QKV_BLOCK[1])

    def meta_heads(part):
        t = qkv_m[:, part * d:(part + 1) * d].reshape(batch, N_META, h, HEAD_DIM).transpose(0, 2, 1, 3)
        return jnp.pad(t, ((0, 0), (0, 0), (0, LANES - N_META), (0, 0)))

    bias = _attention_bias(rel_bias[0], meta_bias[0])
    n_e, _, d_e = moe_gate.shape[1:]
    o, moe_gate_bf = _attention(qkv, meta_heads(1), meta_heads(2), bias, batch=batch, seq=seq, rb=ATTN_ROWS_PER_STEP,
                                rider=moe_gate[0].reshape(n_e * d, d_e))
    o_m = _meta_attention(qkv_m, meta_bias[0], batch=batch)
    wo = w_o[0].astype(BF16)
    h1, wu = _proj_residual(o, wo, xt, bm=PROJ_BLOCK_M, rider=w_up[0])
    (h1_m,) = _proj_residual(o_m, wo, mt, bm=nm)

    g_dense = dense_norm[0].reshape(1, d)
    wd = w_down[0].astype(BF16)
    h2, moe_up_bf = _dense_ffn(h1, g_dense, wg, wu, wd, bm=FFN_BLOCK_M, bf=FFN_BLOCK_F,
                               rider=moe_up[0].reshape(n_e * d, d_e))
    (h2_m,) = _dense_ffn(h1_m, g_dense, wg, wu, wd, bm=nm, bf=FFN_BLOCK_F)

    g_conv = conv_norm[0].reshape(1, d)
    win = w_in[0].astype(BF16)
    bg, z, moe_down_bf = _conv_in_proj(h2, g_conv, win, bm=CONV_IN_BLOCK[0], bn=CONV_IN_BLOCK[1],
                                       rider=moe_down[0].reshape(n_e * d_e, d))
    _, z_m = _conv_in_proj(h2_m, g_conv, win, bm=nm, bn=CONV_IN_BLOCK[1])
    z_meta_last = z_m.reshape(batch, N_META, d)[:, N_META - 1:, :].astype(F32)
    g_moe = moe_norm[0].reshape(1, d)
    wr_pad = jnp.pad(w_router[0], ((0, 0), (0, LANES - n_experts)))
    wr_hi = wr_pad.astype(BF16)
    wr_lo = (wr_pad - wr_hi.astype(F32)).astype(BF16)
    h3, route = _conv_proj_residual(bg, z, z_meta_last, conv_w[0].T, w_out[0].astype(BF16), h2, g_moe, wr_hi, wr_lo,
                                    seq=seq, bm=PROJ_BLOCK_M, n_experts=n_experts)

    block_expert, n_real, buf_tok, p0, p1 = _route_plan(route, n_experts=n_experts, bm=FFN_BLOCK_M)
    ybuf = _expert_ffn(block_expert, n_real, buf_tok, h3, g_moe, moe_gate_bf.reshape(n_e, d, d_e),
                       moe_up_bf.reshape(n_e, d, d_e), moe_down_bf.reshape(n_e, d_e, d),
                       bm=FFN_BLOCK_M, bf=FFN_BLOCK_F)
    out = _combine(p0, p1, h3, route, ybuf, tm=COMBINE_BLOCK_M)
    return out.reshape(batch, seq, d)
```
